```python
import math
import jax, jax.numpy as jnp
from jax import lax
import numpy as np


D_MODEL = 1024
BATCH = 8
SEQ = 2048
DEPTH = 4

GRID_W = 64
CTX_LEN = 256
N_EVEN = (DEPTH + 1) // 2
N_ODD = DEPTH // 2
NORM_EPS = 1e-6
N_MOD = 6

SSD_WIDTH = D_MODEL
SSD_HEAD_DIM = 64
SSD_HEADS = SSD_WIDTH // SSD_HEAD_DIM
SSD_STATE = 128
SSD_GROUPS = 2
SSD_CONV = 3
SSD_CHUNK = 128
SSD_XBC = SSD_WIDTH + 2 * SSD_GROUPS * SSD_STATE
DT_PROJ_SCALE = 0.1
HY_WIDTH = D_MODEL
HY_SHORT = 3
HY_BANDS = 16
HY_EMB = 1 + 2 * HY_BANDS
HY_FFN = 64
HY_FAST_DECAY = 0.3
HY_SLOW_DECAY = 1.5
HY_DECAY_TARGET = 1e-2
COL_XBC = SSD_WIDTH
COL_DT = COL_XBC + SSD_XBC
COL_HY = COL_DT + 2 * SSD_HEADS
IN_COLS = COL_HY + 3 * HY_WIDTH
MIX_WIDTH = SSD_WIDTH + HY_WIDTH
ATTN_HEAD_DIM = 64
ATTN_Q_HEADS = 16
ATTN_KV_HEADS = 4
ATTN_GROUP = ATTN_Q_HEADS // ATTN_KV_HEADS
ATTN_Q_WIDTH = ATTN_Q_HEADS * ATTN_HEAD_DIM
ATTN_KV_WIDTH = ATTN_KV_HEADS * ATTN_HEAD_DIM
QKV_COLS = ATTN_Q_WIDTH + 2 * ATTN_KV_WIDTH
ATTN_SCALE = ATTN_HEAD_DIM ** -0.5
WINDOW = 128
ATTN_BLOCK = 128
ROPE_BASE = 10000.0
FFN_HIDDEN = -(-8 * D_MODEL // (3 * 256)) * 256

kernel_name = 'hybrid_ssd_hyena_swa_flow_block'


def rmsnorm(t, g):
    t32 = t.astype(jnp.float32)
    y = t32 * lax.rsqrt(jnp.mean(t32 * t32, axis=-1, keepdims=True) + NORM_EPS)
    return (y * g.astype(jnp.float32)).astype(t.dtype)


def conv_centred(t, w, b):
    k_w = w.shape[0]
    pad = (k_w - 1) // 2
    L = t.shape[1]
    tp = jnp.pad(t, ((0, 0), (pad, k_w - 1 - pad), (0, 0)))
    out = tp[:, 0:L] * w[0]
    for k in range(1, k_w):
        out = out + tp[:, k:k + L] * w[k]
    return out + b


def axial_rope(t):
    L = t.shape[1]
    rows = L // GRID_W
    row = jnp.repeat(jnp.arange(rows), GRID_W).astype(jnp.float32)
    col = jnp.tile(jnp.arange(GRID_W), rows).astype(jnp.float32)
    half = t.shape[-1] // 2
    nf = half // 2
    inv = ROPE_BASE ** (-jnp.arange(nf, dtype=jnp.float32) / nf)

    def rot(u, pos):
        ang = (pos[:, None] * inv[None, :])[None, :, None, :]
        cos, sin = jnp.cos(ang), jnp.sin(ang)
        u1, u2 = u[..., :nf], u[..., nf:]
        return jnp.concatenate([u1 * cos - u2 * sin, u2 * cos + u1 * sin], axis=-1)

    t32 = t.astype(jnp.float32)
    return jnp.concatenate([rot(t32[..., :half], row), rot(t32[..., half:], col)], axis=-1).astype(t.dtype)


def ssd_chunked(x, dt, a_neg, bm, cm, h0):
    b, L, H, P = x.shape
    G, N = bm.shape[2], bm.shape[3]
    R = H // G
    Q = SSD_CHUNK
    nc = L // Q
    xc = (x.astype(jnp.float32) * dt[..., None]).reshape(b, nc, Q, G, R, P)
    acum = jnp.cumsum((dt * a_neg).reshape(b, nc, Q, G, R), axis=2)
    bc = bm.astype(jnp.float32).reshape(b, nc, Q, G, N)
    cc = cm.astype(jnp.float32).reshape(b, nc, Q, G, N)
    tri = jnp.tril(jnp.ones((Q, Q), dtype=bool))
    seg = acum[:, :, :, None] - acum[:, :, None, :]
    lmat = jnp.exp(jnp.where(tri[None, None, :, :, None, None], seg, -jnp.inf))
    cb = jnp.einsum('bcign,bcjgn->bcijg', cc, bc)
    y_diag = jnp.einsum('bcijg,bcijgr,bcjgrp->bcigrp', cb, lmat, xc)
    decay_to_end = jnp.exp(acum[:, :, -1:] - acum)
    states = jnp.einsum('bcjgn,bcjgr,bcjgrp->bcgrpn', bc, decay_to_end, xc)
    chunk_decay = jnp.exp(acum[:, :, -1])

    def step(h, inp):
        st, dec = inp
        return h * dec[..., None, None] + st, h

    h_last, h_in = lax.scan(step, h0.reshape(b, G, R, P, N).astype(jnp.float32),
                            (jnp.moveaxis(states, 1, 0), jnp.moveaxis(chunk_decay, 1, 0)))
    h_in = jnp.moveaxis(h_in, 0, 1)
    y_off = jnp.einsum('bcign,bcigr,bcgrpn->bcigrp', cc, jnp.exp(acum), h_in)
    return (y_diag + y_off).reshape(b, L, H, P), h_last.reshape(b, H, P, N)


def bidir_ssd(seq_c, seq_l, a_f, a_b, d_skip):
    xc, dfc, dbc, bmc, cmc = seq_c
    xl, dfl, dbl, bml, cml = seq_l
    b, _, H, P = xc.shape
    h0 = jnp.zeros((b, H, P, bmc.shape[-1]), jnp.float32)
    fl = lambda t: jnp.flip(t, axis=1)
    y_cf, s_cf = ssd_chunked(xc, dfc, a_f, bmc, cmc, h0)
    y_lf, _ = ssd_chunked(xl, dfl, a_f, bml, cml, s_cf)
    y_cb, s_cb = ssd_chunked(fl(xc), fl(dbc), a_b, fl(bmc), fl(cmc), h0)
    y_lb, _ = ssd_chunked(fl(xl), fl(dbl), a_b, fl(bml), fl(cml), s_cb)
    y_c = y_cf + fl(y_cb) + d_skip[:, None] * xc.astype(jnp.float32)
    y_l = y_lf + fl(y_lb) + d_skip[:, None] * xl.astype(jnp.float32)
    return y_c, y_l


def hyena_filter(L, w1, b1, w2, b2, w3, b3, w4, freq):
    t = jnp.arange(L, dtype=jnp.float32)
    t_unit = t / (L - 1)
    ang = 2.0 * math.pi * t / L
    f = jnp.linspace(1e-4, HY_BANDS - 1, HY_BANDS, dtype=jnp.float32)
    z = jnp.concatenate([t_unit[:, None], jnp.cos(ang[:, None] * f), -jnp.sin(ang[:, None] * f)], axis=-1)
    h = jnp.sin(freq[0] * (z @ w1 + b1))
    h = jnp.sin(freq[1] * (h @ w2 + b2))
    h = jnp.sin(freq[2] * (h @ w3 + b3))
    h = (h @ w4).astype(jnp.float32)
    max_decay = math.log(HY_DECAY_TARGET) / HY_FAST_DECAY
    min_decay = math.log(HY_DECAY_TARGET) / HY_SLOW_DECAY
    deltas = jnp.linspace(min_decay, max_decay, HY_WIDTH, dtype=jnp.float32)
    window = jnp.exp(-t_unit[:, None] * jnp.abs(deltas)[None, :])
    h_fwd = h[:, :HY_WIDTH] * window
    h_bwd = h[:, HY_WIDTH:] * window
    k = jnp.concatenate([h_fwd, jnp.zeros((1, HY_WIDTH), jnp.float32), h_bwd[1:][::-1]], axis=0)
    return k * lax.rsqrt(jnp.sum(k * k, axis=0, keepdims=True) + 1e-6)


def long_conv(u, k):
    L = u.shape[1]
    uf = jnp.fft.rfft(u.astype(jnp.float32), n=2 * L, axis=1)
    kf = jnp.fft.rfft(k, n=2 * L, axis=0)
    return jnp.fft.irfft(uf * kf[None], n=2 * L, axis=1)[:, :L]


def hyena_op(hp, conv_w, conv_b, filt, bias):
    hp = conv_centred(hp, conv_w, conv_b)
    x0, x1, v = jnp.split(hp, 3, axis=-1)
    u = (x1 * v).astype(jnp.float32)
    k = hyena_filter(hp.shape[1], *filt)
    y = long_conv(u, k) + bias.astype(jnp.float32) * u
    return (x0.astype(jnp.float32) * y).astype(hp.dtype)


def ssd_hyena_mixer(h_c, h_l, w_in, conv_w, conv_b, dt_bias, a_log, d_skip, norm_g,
                    hy_conv_w, hy_conv_b, hy_filt, hy_bias, w_out, need_ctx):
    G, N = SSD_GROUPS, SSD_STATE

    def prep(h):
        b, L = h.shape[:2]
        p = h @ w_in
        z = p[..., :COL_XBC]
        xbc = jax.nn.silu(conv_centred(p[..., COL_XBC:COL_DT], conv_w, conv_b))
        dt = p[..., COL_DT:COL_HY].astype(jnp.float32)
        hy = p[..., COL_HY:]
        xs = xbc[..., :SSD_WIDTH].reshape(b, L, SSD_HEADS, SSD_HEAD_DIM)
        bm = xbc[..., SSD_WIDTH:SSD_WIDTH + G * N].reshape(b, L, G, N)
        cm = xbc[..., SSD_WIDTH + G * N:].reshape(b, L, G, N)
        dt_f = jax.nn.softplus(dt[..., :SSD_HEADS] + dt_bias[0].astype(jnp.float32))
        dt_b = jax.nn.softplus(dt[..., SSD_HEADS:] + dt_bias[1].astype(jnp.float32))
        return z, (xs, dt_f, dt_b, bm, cm), hy

    z_c, seq_c, hy_c = prep(h_c)
    z_l, seq_l, hy_l = prep(h_l)
    a_f = -jnp.exp(a_log[0].astype(jnp.float32))
    a_b = -jnp.exp(a_log[1].astype(jnp.float32))
    y_c, y_l = bidir_ssd(seq_c, seq_l, a_f, a_b, d_skip.astype(jnp.float32))

    def finish(y, z, hy):
        b, L = z.shape[:2]
        y = rmsnorm(y.reshape(b, L, SSD_WIDTH) * jax.nn.silu(z.astype(jnp.float32)), norm_g).astype(z.dtype)
        g = hyena_op(hy, hy_conv_w, hy_conv_b, hy_filt, hy_bias)
        return jnp.concatenate([y, g], axis=-1) @ w_out

    out_l = finish(y_l, z_l, hy_l)
    if not need_ctx:
        return None, out_l
    return finish(y_c, z_c, hy_c), out_l


def sink_softmax(scores, sink):
    s_sink = jnp.broadcast_to(sink.astype(jnp.float32)[None, :, :, None, None], scores.shape[:-1] + (1,))
    p = jax.nn.softmax(jnp.concatenate([scores, s_sink], axis=-1), axis=-1)
    return p[..., :-1]


def windowed_attention(q, k, v, k_c, v_c, sink):
    b, L = q.shape[:2]
    nb = L // ATTN_BLOCK
    span = ATTN_BLOCK + 2 * WINDOW
    kp = jnp.pad(k, ((0, 0), (WINDOW, WINDOW), (0, 0), (0, 0)))
    vp = jnp.pad(v, ((0, 0), (WINDOW, WINDOW), (0, 0), (0, 0)))

    def block(i):
        start = i * ATTN_BLOCK
        qb = lax.dynamic_slice_in_dim(q, start, ATTN_BLOCK, axis=1)
        kb = lax.dynamic_slice_in_dim(kp, start, span, axis=1)
        vb = lax.dynamic_slice_in_dim(vp, start, span, axis=1)
        qpos = start + jnp.arange(ATTN_BLOCK)
        kpos = start - WINDOW + jnp.arange(span)
        ok = (jnp.abs(qpos[:, None] - kpos[None, :]) <= WINDOW) & (kpos[None, :] >= 0) & (kpos[None, :] < L)
        s_loc = jnp.einsum('bqgrd,bkgd->bgrqk', qb, kb).astype(jnp.float32) * ATTN_SCALE
        s_loc = jnp.where(ok[None, None, None], s_loc, -jnp.inf)
        s_ctx = jnp.einsum('bqgrd,bkgd->bgrqk', qb, k_c).astype(jnp.float32) * ATTN_SCALE
        p = sink_softmax(jnp.concatenate([s_loc, s_ctx], axis=-1), sink).astype(vb.dtype)
        return (jnp.einsum('bgrqk,bkgd->bqgrd', p[..., :span], vb)
                + jnp.einsum('bgrqk,bkgd->bqgrd', p[..., span:], v_c))

    o = lax.map(block, jnp.arange(nb))
    return jnp.moveaxis(o, 0, 1).reshape(b, L, ATTN_Q_WIDTH)


def attn_mixer(h_c, h_l, w_qkv, sinks, w_out, need_ctx):
    def proj(h):
        b, L = h.shape[:2]
        p = h @ w_qkv
        q = p[..., :ATTN_Q_WIDTH].reshape(b, L, ATTN_Q_HEADS, ATTN_HEAD_DIM)
        k = p[..., ATTN_Q_WIDTH:ATTN_Q_WIDTH + ATTN_KV_WIDTH].reshape(b, L, ATTN_KV_HEADS, ATTN_HEAD_DIM)
        v = p[..., ATTN_Q_WIDTH + ATTN_KV_WIDTH:].reshape(b, L, ATTN_KV_HEADS, ATTN_HEAD_DIM)
        return q, k, v

    q_c, k_c, v_c = proj(h_c)
    q_l, k_l, v_l = proj(h_l)
    q_l, k_l = axial_rope(q_l), axial_rope(k_l)
    sink = sinks.reshape(ATTN_KV_HEADS, ATTN_GROUP)
    b, L = h_l.shape[:2]
    o_l = windowed_attention(q_l.reshape(b, L, ATTN_KV_HEADS, ATTN_GROUP, ATTN_HEAD_DIM), k_l, v_l, k_c, v_c, sink)
    out_l = o_l @ w_out
    if not need_ctx:
        return None, out_l
    lc = h_c.shape[1]
    qg = q_c.reshape(b, lc, ATTN_KV_HEADS, ATTN_GROUP, ATTN_HEAD_DIM)
    s = jnp.einsum('bqgrd,bkgd->bgrqk', qg, k_c).astype(jnp.float32) * ATTN_SCALE
    p = sink_softmax(s, sink).astype(v_c.dtype)
    o_c = jnp.einsum('bgrqk,bkgd->bqgrd', p, v_c).reshape(b, lc, ATTN_Q_WIDTH)
    return o_c @ w_out, out_l


def swiglu(h, w_in, w_out):
    gu = h @ w_in
    return (jax.nn.silu(gu[..., :FFN_HIDDEN]) * gu[..., FFN_HIDDEN:]) @ w_out


def setup_inputs(seed: int = 0) -> dict:
    key = jax.random.key(seed)
    k = jax.random.split(key, 40)
    d = D_MODEL

    def nrm(kk, shape, scale):
        return scale * jax.random.normal(kk, shape, jnp.float32)

    hy_in_w = nrm(k[8], (N_EVEN, d, IN_COLS), d ** -0.5)
    hy_in_w = hy_in_w.at[:, :, COL_DT:COL_HY].multiply(DT_PROJ_SCALE)
    dt0 = jnp.exp(jax.random.uniform(k[11], (N_EVEN, 2, SSD_HEADS), jnp.float32, math.log(1e-3), math.log(1e-1)))
    ssd_dt_bias = dt0 + jnp.log(-jnp.expm1(-dt0))
    ssd_a_log = jnp.log(jax.random.uniform(k[12], (N_EVEN, 2, SSD_HEADS), jnp.float32, 1.0, 16.0))
    return {
        'x': nrm(k[0], (BATCH, SEQ, d), 1.0),
        'c': nrm(k[1], (BATCH, d), 1.0),
        'ctx': nrm(k[2], (BATCH, CTX_LEN, d), 1.0),
        'c_ctx': nrm(k[3], (d,), 1.0),
        'ada_w': nrm(k[4], (DEPTH, d, N_MOD * d), 0.5 * d ** -0.5),
        'ada_b': nrm(k[5], (DEPTH, N_MOD * d), 0.02),
        'norm1_g': 1.0 + nrm(k[6], (DEPTH, d), 0.05),
        'norm2_g': 1.0 + nrm(k[7], (DEPTH, d), 0.05),
        'hy_in_w': hy_in_w,
        'ssd_conv_w': nrm(k[9], (N_EVEN, SSD_CONV, SSD_XBC), SSD_CONV ** -0.5),
        'ssd_conv_b': nrm(k[10], (N_EVEN, SSD_XBC), 0.02),
        'ssd_dt_bias': ssd_dt_bias,
        'ssd_a_log': ssd_a_log,
        'ssd_d': 1.0 + nrm(k[13], (N_EVEN, SSD_HEADS), 0.1),
        'ssd_norm_g': 1.0 + nrm(k[14], (N_EVEN, SSD_WIDTH), 0.05),
        'hy_conv_w': nrm(k[15], (N_EVEN, HY_SHORT, 3 * HY_WIDTH), HY_SHORT ** -0.5),
        'hy_conv_b': nrm(k[16], (N_EVEN, 3 * HY_WIDTH), 0.02),
        'hy_w1': nrm(k[17], (N_EVEN, HY_EMB, HY_FFN), HY_EMB ** -0.5),
        'hy_b1': nrm(k[18], (N_EVEN, HY_FFN), 0.02),
        'hy_w2': nrm(k[19], (N_EVEN, HY_FFN, HY_FFN), HY_FFN ** -0.5),
        'hy_b2': nrm(k[20], (N_EVEN, HY_FFN), 0.02),
        'hy_w3': nrm(k[21], (N_EVEN, HY_FFN, HY_FFN), HY_FFN ** -0.5),
        'hy_b3': nrm(k[22], (N_EVEN, HY_FFN), 0.02),
        'hy_w4': nrm(k[23], (N_EVEN, HY_FFN, 2 * HY_WIDTH), HY_FFN ** -0.5),
        'hy_freq': 1.0 + nrm(k[24], (N_EVEN, 3, HY_FFN), 0.1),
        'hy_bias': nrm(k[25], (N_EVEN, HY_WIDTH), 0.1),
        'hy_out_w': nrm(k[26], (N_EVEN, MIX_WIDTH, d), MIX_WIDTH ** -0.5),
        'attn_qkv_w': nrm(k[27], (N_ODD, d, QKV_COLS), d ** -0.5),
        'attn_sinks': nrm(k[28], (N_ODD, ATTN_Q_HEADS), 0.5),
        'attn_out_w': nrm(k[29], (N_ODD, ATTN_Q_WIDTH, d), ATTN_Q_WIDTH ** -0.5),
        'ffn_w_in': nrm(k[30], (DEPTH, d, 2 * FFN_HIDDEN), d ** -0.5),
        'ffn_w_out': nrm(k[31], (DEPTH, FFN_HIDDEN, d), FFN_HIDDEN ** -0.5),
        'final_g': 1.0 + nrm(k[32], (d,), 0.05),
    }


def reference(x, c, ctx, c_ctx, ada_w, ada_b, norm1_g, norm2_g, hy_in_w, ssd_conv_w, ssd_conv_b,
              ssd_dt_bias, ssd_a_log, ssd_d, ssd_norm_g, hy_conv_w, hy_conv_b, hy_w1, hy_b1, hy_w2,
              hy_b2, hy_w3, hy_b3, hy_w4, hy_freq, hy_bias, hy_out_w, attn_qkv_w, attn_sinks,
              attn_out_w, ffn_w_in, ffn_w_out, final_g):
    sc = jax.nn.silu(c)
    scc = jax.nn.silu(c_ctx)
    for layer in range(DEPTH):
        need_ctx = layer < DEPTH - 1
        j = layer // 2
        mod_l = jnp.split((sc @ ada_w[layer] + ada_b[layer])[:, None, :], N_MOD, axis=-1)
        mod_c = jnp.split((scc @ ada_w[layer] + ada_b[layer])[None, None, :], N_MOD, axis=-1)
        h_l = rmsnorm(x, norm1_g[layer]) * (1.0 + mod_l[1]) + mod_l[0]
        h_c = rmsnorm(ctx, norm1_g[layer]) * (1.0 + mod_c[1]) + mod_c[0]
        if layer % 2 == 0:
            hy_filt = (hy_w1[j], hy_b1[j], hy_w2[j], hy_b2[j], hy_w3[j], hy_b3[j], hy_w4[j], hy_freq[j])
            o_c, o_l = ssd_hyena_mixer(h_c, h_l, hy_in_w[j], ssd_conv_w[j], ssd_conv_b[j], ssd_dt_bias[j],
                                       ssd_a_log[j], ssd_d[j], ssd_norm_g[j], hy_conv_w[j], hy_conv_b[j],
                                       hy_filt, hy_bias[j], hy_out_w[j], need_ctx)
        else:
            o_c, o_l = attn_mixer(h_c, h_l, attn_qkv_w[j], attn_sinks[j], attn_out_w[j], need_ctx)
        x = x + mod_l[2] * o_l
        h_l = rmsnorm(x, norm2_g[layer]) * (1.0 + mod_l[4]) + mod_l[3]
        x = x + mod_l[5] * swiglu(h_l, ffn_w_in[layer], ffn_w_out[layer])
        if need_ctx:
            ctx = ctx + mod_c[2] * o_c
            h_c = rmsnorm(ctx, norm2_g[layer]) * (1.0 + mod_c[4]) + mod_c[3]
            ctx = ctx + mod_c[5] * swiglu(h_c, ffn_w_in[layer], ffn_w_out[layer])
    return rmsnorm(x, final_g)
```

```python
import functools
import math

import jax
import jax.numpy as jnp
from jax import lax
from jax.experimental import pallas as pl
from jax.experimental.pallas import tpu as pltpu

f32 = jnp.float32
bf16 = jnp.bfloat16

D = 1024
BATCH = 8
SEQ = 2048
DEPTH = 4
GRID_W = 64
CTX = 256
S = SEQ + CTX
EPS = 1e-6
N_MOD = 6
MOD_ROWS = 16
CTX_MOD = 8

HEADS = 16
HEAD_P = 64
STATE = 128
GROUPS = 2
CHUNK = 128
XBC = D + 2 * GROUPS * STATE
ZX = D + XBC
DT_LANES = 128
HY_BANDS = 16
HY_EMB = 1 + 2 * HY_BANDS
HY_FFN = 64
COL_XBC = D
COL_DT = COL_XBC + XBC
COL_HY = COL_DT + 2 * HEADS

Q_HEADS = 16
KV_HEADS = 4
ATT_GROUP = Q_HEADS // KV_HEADS
QW = Q_HEADS * HEAD_P
KVW = KV_HEADS * HEAD_P
WINDOW = 128
ABLK = 128
ROPE_BASE = 10000.0
FFN_H = -(-8 * D // (3 * 256)) * 256

RC = 256
VMEM_CAP = 56 * 1024 * 1024


def _cparams(sem, vmem_mb):
    return pltpu.CompilerParams(dimension_semantics=sem,
                                vmem_limit_bytes=min(int(vmem_mb * 1024 * 1024), VMEM_CAP))


def _silu(v):
    return v * jax.nn.sigmoid(v)


def _softplus(v):
    return jnp.maximum(v, 0.0) + jnp.log1p(jnp.exp(-jnp.abs(v)))


def _bdot(a, b):
    return jnp.dot(a, b, preferred_element_type=f32)


def _bdot_nt(a, b):
    return lax.dot_general(a, b, (((1,), (1,)), ((), ())), preferred_element_type=f32)


def _row_chunks(n_rows, body):
    def step(c, carry):
        body(pl.multiple_of(c * RC, RC))
        return carry
    lax.fori_loop(0, n_rows // RC, step, 0)


def _norm_mod(x, g, mod, shift_row, scale_row, row0):
    ms = jnp.mean(x * x, axis=-1, keepdims=True)
    y = x * lax.rsqrt(ms + EPS) * g
    row = row0 + lax.broadcasted_iota(jnp.int32, x.shape, 0)
    is_ctx = row >= SEQ
    scale = jnp.where(is_ctx, mod[CTX_MOD + scale_row:CTX_MOD + scale_row + 1],
                      mod[scale_row:scale_row + 1])
    shift = jnp.where(is_ctx, mod[CTX_MOD + shift_row:CTX_MOD + shift_row + 1],
                      mod[shift_row:shift_row + 1])
    return y * (1.0 + scale) + shift


def _gate_rows(mod, gate_row, shape, row0):
    row = row0 + lax.broadcasted_iota(jnp.int32, shape, 0)
    return jnp.where(row >= SEQ, mod[CTX_MOD + gate_row:CTX_MOD + gate_row + 1],
                     mod[gate_row:gate_row + 1])


def _norm_mod_to_scratch(x_ref, g_ref, mod_ref, h_scr):
    def body(r0):
        h_scr[pl.ds(r0, RC), :] = _norm_mod(x_ref[0, pl.ds(r0, RC), :], g_ref[...], mod_ref[0],
                                            0, 1, r0).astype(bf16)
    _row_chunks(S, body)


def _conv3_chunk(p_scr, r0, cw):
    v = p_scr[pl.ds(r0, RC), :]
    up = p_scr[pl.ds(pl.multiple_of(jnp.maximum(r0 - 8, 0), 8), 8), :][7:8]
    dn = p_scr[pl.ds(pl.multiple_of(jnp.minimum(r0 + RC, S - 8), 8), 8), :][0:1]
    row = r0 + lax.broadcasted_iota(jnp.int32, v.shape, 0)
    prev = jnp.where(row == r0, up, pltpu.roll(v, 1, 0))
    prev = jnp.where(jnp.logical_or(row == 0, row == SEQ), 0.0, prev)
    nxt = jnp.where(row == r0 + RC - 1, dn, pltpu.roll(v, RC - 1, 0))
    nxt = jnp.where(jnp.logical_or(row == SEQ - 1, row == S - 1), 0.0, nxt)
    return prev * cw[0:1] + v * cw[1:2] + nxt * cw[2:3] + cw[3:4]


def _mod_kernel(c_ref, w_ref, b_ref, o_ref):
    sc = _silu(c_ref[...]).astype(bf16)
    o_ref[0] = _bdot(sc, w_ref[0].astype(bf16)) + b_ref[0]


def _mod_call(cc, ada_w, ada_b):
    tn = 1024
    return pl.pallas_call(
        _mod_kernel,
        grid=(DEPTH, N_MOD * D // tn),
        in_specs=[pl.BlockSpec((MOD_ROWS, D), lambda l, j: (0, 0)),
                  pl.BlockSpec((1, D, tn), lambda l, j: (l, 0, j)),
                  pl.BlockSpec((1, 1, tn), lambda l, j: (l, 0, j))],
        out_specs=pl.BlockSpec((1, MOD_ROWS, tn), lambda l, j: (l, 0, j)),
        out_shape=jax.ShapeDtypeStruct((DEPTH, MOD_ROWS, N_MOD * D), f32),
        compiler_params=_cparams(("parallel", "parallel"), 24),
        name="adaln_mod",
    )(cc, ada_w, ada_b.reshape(DEPTH, 1, N_MOD * D))


def _nmm_kernel(x_ref, mod_ref, g_ref, w_ref, o_ref, h_scr):
    @pl.when(pl.program_id(1) == 0)
    def _():
        _norm_mod_to_scratch(x_ref, g_ref, mod_ref, h_scr)

    def body(r0):
        o_ref[0, pl.ds(r0, RC), :] = _bdot(h_scr[pl.ds(r0, RC), :], w_ref[...]).astype(o_ref.dtype)
    _row_chunks(S, body)


def _nmm_call(xs, mods_l, g, w, tn, name):
    n = w.shape[1]
    return pl.pallas_call(
        _nmm_kernel,
        grid=(BATCH, n // tn),
        in_specs=[pl.BlockSpec((1, S, D), lambda b, j: (b, 0, 0)),
                  pl.BlockSpec((1, MOD_ROWS, D), lambda b, j: (b, 0, 0)),
                  pl.BlockSpec((1, D), lambda b, j: (0, 0)),
                  pl.BlockSpec((D, tn), lambda b, j: (0, j))],
        out_specs=pl.BlockSpec((1, S, tn), lambda b, j: (b, 0, j)),
        out_shape=jax.ShapeDtypeStruct((BATCH, S, n), bf16),
        scratch_shapes=[pltpu.VMEM((S, D), bf16)],
        compiler_params=_cparams(("parallel", "arbitrary"), 40),
        name=name,
    )(xs, mods_l, g, w)


ZX_TN = 512
NZ_T = D // ZX_TN


def _inproj_ssd_kernel(x_ref, mod_ref, g_ref, w_ref, cw_ref, wdt_ref, dtb_ref, zx_ref, dt_ref,
                       h_scr, p_scr):
    j = pl.program_id(1)

    @pl.when(j == 0)
    def _():
        _norm_mod_to_scratch(x_ref, g_ref, mod_ref, h_scr)

        def dt_body(r0):
            sp = _softplus(_bdot(h_scr[pl.ds(r0, RC), :], wdt_ref[...]) + dtb_ref[...])
            lane = lax.broadcasted_iota(jnp.int32, sp.shape, 1)
            dt_ref[0, pl.ds(r0, RC), :] = jnp.where(lane < 2 * HEADS, sp, 0.0)
        _row_chunks(S, dt_body)

    @pl.when(j < NZ_T)
    def _():
        def body(r0):
            zx_ref[0, pl.ds(r0, RC), :] = _bdot(h_scr[pl.ds(r0, RC), :], w_ref[...]).astype(bf16)
        _row_chunks(S, body)

    @pl.when(j >= NZ_T)
    def _():
        def mm(r0):
            p_scr[pl.ds(r0, RC), :] = _bdot(h_scr[pl.ds(r0, RC), :], w_ref[...])
        _row_chunks(S, mm)

        def conv(r0):
            zx_ref[0, pl.ds(r0, RC), :] = _silu(_conv3_chunk(p_scr, r0, cw_ref[...])).astype(bf16)
        _row_chunks(S, conv)


def _inproj_ssd_call(xs, mods_l, g, w_zx, cw, wdt, dtb):
    return pl.pallas_call(
        _inproj_ssd_kernel,
        grid=(BATCH, ZX // ZX_TN),
        in_specs=[pl.BlockSpec((1, S, D), lambda b, j: (b, 0, 0)),
                  pl.BlockSpec((1, MOD_ROWS, D), lambda b, j: (b, 0, 0)),
                  pl.BlockSpec((1, D), lambda b, j: (0, 0)),
                  pl.BlockSpec((D, ZX_TN), lambda b, j: (0, j)),
                  pl.BlockSpec((8, ZX_TN), lambda b, j: (0, j)),
                  pl.BlockSpec((D, DT_LANES), lambda b, j: (0, 0)),
                  pl.BlockSpec((1, DT_LANES), lambda b, j: (0, 0))],
        out_specs=[pl.BlockSpec((1, S, ZX_TN), lambda b, j: (b, 0, j)),
                   pl.BlockSpec((1, S, DT_LANES), lambda b, j: (b, 0, 0))],
        out_shape=[jax.ShapeDtypeStruct((BATCH, S, ZX), bf16),
                   jax.ShapeDtypeStruct((BATCH, S, DT_LANES), f32)],
        scratch_shapes=[pltpu.VMEM((S, D), bf16), pltpu.VMEM((S, ZX_TN), f32)],
        compiler_params=_cparams(("parallel", "arbitrary"), 48),
        name="inproj_ssd",
    )(xs, mods_l, g, w_zx, cw, wdt, dtb)


HY_TN = 256


def _inproj_hy_kernel(x_ref, mod_ref, g_ref, w0_ref, w1_ref, w2_ref, c0_ref, c1_ref, c2_ref,
                      x0_ref, u_ref, h_scr, p0_scr, p1_scr, p2_scr):
    @pl.when(pl.program_id(1) == 0)
    def _():
        _norm_mod_to_scratch(x_ref, g_ref, mod_ref, h_scr)

    def mm(r0):
        h = h_scr[pl.ds(r0, RC), :]
        p0_scr[pl.ds(r0, RC), :] = _bdot(h, w0_ref[...])
        p1_scr[pl.ds(r0, RC), :] = _bdot(h, w1_ref[...])
        p2_scr[pl.ds(r0, RC), :] = _bdot(h, w2_ref[...])
    _row_chunks(S, mm)

    def conv(r0):
        x0_ref[0, pl.ds(r0, RC), :] = _conv3_chunk(p0_scr, r0, c0_ref[...]).astype(bf16)
        x1 = _conv3_chunk(p1_scr, r0, c1_ref[...])
        v = _conv3_chunk(p2_scr, r0, c2_ref[...])
        u_ref[0, pl.ds(r0, RC), :] = (x1 * v).astype(bf16)
    _row_chunks(S, conv)


def _inproj_hy_call(xs, mods_l, g, w_hy, cw_hy):
    nt = D // HY_TN
    wspec = lambda k: pl.BlockSpec((D, HY_TN), lambda b, j, k=k: (0, j + k * nt))
    cspec = lambda k: pl.BlockSpec((8, HY_TN), lambda b, j, k=k: (0, j + k * nt))
    return pl.pallas_call(
        _inproj_hy_kernel,
        grid=(BATCH, nt),
        in_specs=[pl.BlockSpec((1, S, D), lambda b, j: (b, 0, 0)),
                  pl.BlockSpec((1, MOD_ROWS, D), lambda b, j: (b, 0, 0)),
                  pl.BlockSpec((1, D), lambda b, j: (0, 0)),
                  wspec(0), wspec(1), wspec(2), cspec(0), cspec(1), cspec(2)],
        out_specs=[pl.BlockSpec((1, S, HY_TN), lambda b, j: (b, 0, j)),
                   pl.BlockSpec((1, S, HY_TN), lambda b, j: (b, 0, j))],
        out_shape=[jax.ShapeDtypeStruct((BATCH, S, D), bf16),
                   jax.ShapeDtypeStruct((BATCH, S, D), bf16)],
        scratch_shapes=[pltpu.VMEM((S, D), bf16)] + [pltpu.VMEM((S, HY_TN), f32)] * 3,
        compiler_params=_cparams(("parallel", "arbitrary"), 48),
        name="inproj_hyena",
    )(xs, mods_l, g, w_hy, w_hy, w_hy, cw_hy, cw_hy, cw_hy)


N_CHUNK = S // CHUNK
LAT_CHUNKS = SEQ // CHUNK


def _split3_dot(t, a):
    a1 = a.astype(bf16)
    r1 = a - a1.astype(f32)
    a2 = r1.astype(bf16)
    a3 = (r1 - a2.astype(f32)).astype(bf16)
    return _bdot(t, a1) + _bdot(t, a2) + _bdot(t, a3)


def _ssd_chunk(zx_ref, dt_ref, aneg, e_ref, y_scr, st_scr, r0, backward):
    lane_off = HEADS if backward else 0
    x = zx_ref[0, pl.ds(r0, CHUNK), D:2 * D].astype(f32)
    bm = zx_ref[0, pl.ds(r0, CHUNK), 2 * D:2 * D + GROUPS * STATE]
    cm = zx_ref[0, pl.ds(r0, CHUNK), 2 * D + GROUPS * STATE:ZX]
    dt = dt_ref[0, pl.ds(r0, CHUNK), :]
    a = dt * aneg

    ri = lax.broadcasted_iota(jnp.int32, (CHUNK, CHUNK), 0)
    ci = lax.broadcasted_iota(jnp.int32, (CHUNK, CHUNK), 1)
    causal = (ci >= ri) if backward else (ci <= ri)
    tmat = jnp.where(causal, 1.0, 0.0).astype(bf16)
    acum = _split3_dot(tmat, a)
    acum_t = acum.T
    tot = acum[0:1] if backward else acum[CHUNK - 1:CHUNK]
    dte = jnp.exp(tot - acum)
    eac = jnp.exp(acum)
    cdec = jnp.broadcast_to(jnp.exp(tot), (16, DT_LANES))

    e = e_ref[...]
    dt_x = _bdot(dt.astype(bf16), e)
    dte_x = _bdot(dte.astype(bf16), e)
    eac_x = _bdot(eac.astype(bf16), e)
    cdec_x = _bdot(cdec.astype(bf16), e)[0:1]

    xdt = x * dt_x
    xdt_b = xdt.astype(bf16)
    xs_b = (xdt * dte_x).astype(bf16)
    lane = lax.broadcasted_iota(jnp.int32, (CHUNK, 2 * HEAD_P), 1)
    low = lane < HEAD_P
    hg = HEADS // GROUPS
    gw = hg * HEAD_P
    for g in range(GROUPS):
        bg = bm[:, g * STATE:(g + 1) * STATE]
        cg = cm[:, g * STATE:(g + 1) * STATE]
        cb = _bdot_nt(cg, bg)
        hin = st_scr[:, g * gw:(g + 1) * gw]
        yoff = _bdot(cg, hin.astype(bf16)) * eac_x[:, g * gw:(g + 1) * gw]
        for hp in range(hg // 2):
            h0 = g * hg + 2 * hp
            ms = []
            for h in (h0, h0 + 1):
                l0 = lane_off + h
                seg = acum[:, l0:l0 + 1] - acum_t[l0:l0 + 1, :]
                ms.append(cb * jnp.exp(jnp.where(causal, seg, -jnp.inf)))
            lhs = jnp.concatenate(ms, axis=1).astype(bf16)
            xp = xdt[:, h0 * HEAD_P:(h0 + 2) * HEAD_P]
            rhs = jnp.concatenate([jnp.where(low, xp, 0.0), jnp.where(low, 0.0, xp)],
                                  axis=0).astype(bf16)
            yd = _bdot(lhs, rhs) + yoff[:, hp * 2 * HEAD_P:(hp + 1) * 2 * HEAD_P]
            cols = slice(h0 * HEAD_P, (h0 + 2) * HEAD_P)
            y_scr[pl.ds(r0, CHUNK), cols] = y_scr[pl.ds(r0, CHUNK), cols] + yd
        bg_t = bg.astype(f32).T.astype(bf16)
        st_scr[:, g * gw:(g + 1) * gw] = (hin * cdec_x[:, g * gw:(g + 1) * gw]
                                         + _bdot(bg_t, xs_b[:, g * gw:(g + 1) * gw]))
    del xdt_b


def _ssd_kernel(zx_ref, dt_ref, alog_ref, dskip_ref, ng_ref, ef_ref, eb_ref, o_ref,
                y_scr, sf_scr, sb_scr):
    aneg = -jnp.exp(alog_ref[...])
    sf_scr[...] = jnp.zeros_like(sf_scr)
    sb_scr[...] = jnp.zeros_like(sb_scr)

    def init(c, carry):
        r0 = pl.multiple_of(c * CHUNK, CHUNK)
        y_scr[pl.ds(r0, CHUNK), :] = dskip_ref[...] * zx_ref[0, pl.ds(r0, CHUNK), D:2 * D].astype(f32)
        return carry
    lax.fori_loop(0, N_CHUNK, init, 0)

    def step(t, carry):
        cf = jnp.where(t < N_CHUNK - LAT_CHUNKS, LAT_CHUNKS + t, t - (N_CHUNK - LAT_CHUNKS))
        cbk = N_CHUNK - 1 - t
        _ssd_chunk(zx_ref, dt_ref, aneg, ef_ref, y_scr, sf_scr, pl.multiple_of(cf * CHUNK, CHUNK), False)
        _ssd_chunk(zx_ref, dt_ref, aneg, eb_ref, y_scr, sb_scr, pl.multiple_of(cbk * CHUNK, CHUNK), True)
        return carry
    lax.fori_loop(0, N_CHUNK, step, 0)

    def fin(c, carry):
        r0 = pl.multiple_of(c * CHUNK, CHUNK)
        t = y_scr[pl.ds(r0, CHUNK), :] * _silu(zx_ref[0, pl.ds(r0, CHUNK), 0:D].astype(f32))
        ms = jnp.mean(t * t, axis=-1, keepdims=True)
        o_ref[0, pl.ds(r0, CHUNK), :] = (t * lax.rsqrt(ms + EPS) * ng_ref[...]).astype(bf16)
        return carry
    lax.fori_loop(0, N_CHUNK, fin, 0)


def _ssd_call(zx, dt, alog, dskip, ng, e_f, e_b):
    return pl.pallas_call(
        _ssd_kernel,
        grid=(BATCH,),
        in_specs=[pl.BlockSpec((1, S, ZX), lambda b: (b, 0, 0)),
                  pl.BlockSpec((1, S, DT_LANES), lambda b: (b, 0, 0)),
                  pl.BlockSpec((1, DT_LANES), lambda b: (0, 0)),
                  pl.BlockSpec((1, D), lambda b: (0, 0)),
                  pl.BlockSpec((1, D), lambda b: (0, 0)),
                  pl.BlockSpec((DT_LANES, D), lambda b: (0, 0)),
                  pl.BlockSpec((DT_LANES, D), lambda b: (0, 0))],
        out_specs=pl.BlockSpec((1, S, D), lambda b: (b, 0, 0)),
        out_shape=jax.ShapeDtypeStruct((BATCH, S, D), bf16),
        scratch_shapes=[pltpu.VMEM((S, D), f32),
                        pltpu.VMEM((STATE, D), f32),
                        pltpu.VMEM((STATE, D), f32)],
        compiler_params=_cparams(("parallel",), 56),
        name="ssd_scan",
    )(zx, dt, alog, dskip, ng, e_f, e_b)


def _hp_dot(a, b):
    return jnp.dot(a, b, precision=lax.Precision.HIGHEST, preferred_element_type=f32)


def _hyfilter_kernel(z_ref, w1_ref, b1_ref, w2_ref, b2_ref, w3_ref, b3_ref, fr_ref, w4f_ref, w4b_ref,
                     win_ref, wc_ref, ws_ref, ka_ref, kb_ref, hf_scr, hb_scr, ks_scr, kd_scr, ny_scr):
    fi = pl.program_id(1)
    seg, tc = hf_scr.shape

    @pl.when(fi == 0)
    def _():
        fr = fr_ref[...]

        def taps(c, carry):
            ssq, nyq = carry
            r0 = pl.multiple_of(c * RC, RC)
            h = jnp.sin(fr[0:1] * (_hp_dot(z_ref[pl.ds(r0, RC), :], w1_ref[...]) + b1_ref[...]))
            h = jnp.sin(fr[1:2] * (_hp_dot(h, w2_ref[...]) + b2_ref[...]))
            h = jnp.sin(fr[2:3] * (_hp_dot(h, w3_ref[...]) + b3_ref[...]))
            win = win_ref[pl.ds(r0, RC), :]
            hf = _hp_dot(h, w4f_ref[...]) * win
            hb = _hp_dot(h, w4b_ref[...]) * win
            row = r0 + lax.broadcasted_iota(jnp.int32, hf.shape, 0)
            hb = jnp.where(row == 0, 0.0, hb)
            hf_scr[pl.ds(r0, RC), :] = hf
            hb_scr[pl.ds(r0, RC), :] = hb
            sign = jnp.where((row & 1) == 0, 1.0, -1.0)
            ssq = ssq + jnp.sum(hf * hf + hb * hb, axis=0, keepdims=True)
            nyq = nyq + jnp.sum((hf + hb) * sign, axis=0, keepdims=True)
            return ssq, nyq
        zero = jnp.zeros((1, tc), f32)
        ssq, nyq = lax.fori_loop(0, seg // RC, taps, (zero, zero))
        nrm = lax.rsqrt(ssq + 1e-6)
        ny_scr[...] = jnp.broadcast_to(nyq * nrm, ny_scr.shape)

        def scale(c, carry):
            r0 = pl.multiple_of(c * RC, RC)
            hf, hb = hf_scr[pl.ds(r0, RC), :], hb_scr[pl.ds(r0, RC), :]
            ks_scr[pl.ds(r0, RC), :] = ((hf + hb) * nrm).astype(bf16)
            kd_scr[pl.ds(r0, RC), :] = ((hf - hb) * nrm).astype(bf16)
            return carry
        lax.fori_loop(0, seg // RC, scale, 0)

    ka_ref[...] = _bdot(wc_ref[...], ks_scr[...])
    kb = _bdot(ws_ref[...], kd_scr[...])
    row = fi * kb.shape[0] + lax.broadcasted_iota(jnp.int32, kb.shape, 0)
    kb_ref[...] = jnp.where(row == 0, ny_scr[0:1], kb)


def _hyfilter_call(zf, w1, b1, w2, b2, w3, b3, fr, w4, win, wtab, seg, tc, tf):
    nct, nf = D // tc, seg // tf
    const = lambda shape: pl.BlockSpec(shape, lambda c, f: (0, 0))
    return pl.pallas_call(
        _hyfilter_kernel,
        grid=(nct, nf),
        in_specs=[const((seg, 128)), const((128, HY_FFN)), const((1, HY_FFN)),
                  const((HY_FFN, HY_FFN)), const((1, HY_FFN)),
                  const((HY_FFN, HY_FFN)), const((1, HY_FFN)), const((8, HY_FFN)),
                  pl.BlockSpec((HY_FFN, tc), lambda c, f: (0, c)),
                  pl.BlockSpec((HY_FFN, tc), lambda c, f: (0, c + nct)),
                  pl.BlockSpec((seg, tc), lambda c, f: (0, c)),
                  pl.BlockSpec((tf, seg), lambda c, f: (f, 0)),
                  pl.BlockSpec((tf, seg), lambda c, f: (f + nf, 0))],
        out_specs=[pl.BlockSpec((tf, tc), lambda c, f: (f, c)),
                   pl.BlockSpec((tf, tc), lambda c, f: (f, c))],
        out_shape=[jax.ShapeDtypeStruct((seg, D), f32), jax.ShapeDtypeStruct((seg, D), f32)],
        scratch_shapes=[pltpu.VMEM((seg, tc), f32), pltpu.VMEM((seg, tc), f32),
                        pltpu.VMEM((seg, tc), bf16), pltpu.VMEM((seg, tc), bf16),
                        pltpu.VMEM((8, tc), f32)],
        compiler_params=_cparams(("parallel", "arbitrary"), 48),
        name=f"hyena_filter_{seg}",
    )(zf, w1, b1, w2, b2, w3, b3, fr, w4, w4, win, wtab, wtab)


def _hyconv_kernel(u_ref, x0_ref, wc_ref, ws_ref, wtc_ref, wts_ref, ka_ref, kb_ref, bias_ref, *rest,
                   nf, inv_n):
    o_ref, acc = rest[-2], rest[-1]
    fi = pl.program_id(2)
    seg = acc.shape[0]

    @pl.when(fi == 0)
    def _():
        acc[...] = jnp.zeros_like(acc)

    u = u_ref[0]
    a = _bdot(wc_ref[...], u)
    b = _bdot(ws_ref[...], u)
    ka, kb = ka_ref[...], kb_ref[...]
    is0 = (fi * a.shape[0] + lax.broadcasted_iota(jnp.int32, a.shape, 0)) == 0
    scale = jnp.where(is0, inv_n, 2.0 * inv_n)
    ya = (jnp.where(is0, a * ka, a * ka - b * kb) * scale).astype(bf16)
    yb = (jnp.where(is0, b * kb, a * kb + b * ka) * scale).astype(bf16)

    def inv(r0):
        acc[pl.ds(r0, RC), :] += (_bdot(wtc_ref[pl.ds(r0, RC), :], ya)
                                  + _bdot(wts_ref[pl.ds(r0, RC), :], yb))
    _row_chunks(seg, inv)

    @pl.when(fi == nf - 1)
    def _():
        def fin(r0):
            y = acc[pl.ds(r0, RC), :] + bias_ref[...] * u_ref[0, pl.ds(r0, RC), :].astype(f32)
            o_ref[0, pl.ds(r0, RC), :] = (x0_ref[0, pl.ds(r0, RC), :].astype(f32) * y).astype(bf16)
        _row_chunks(seg, fin)


def _hyconv_call(u, x0, wtab, wtab_t, ka, kb, bias, seg, row_blk, tc, tf, prev=None):
    nct, nf = D // tc, seg // tf
    in_specs = [pl.BlockSpec((1, seg, tc), lambda b, c, f: (b, row_blk, c)),
                pl.BlockSpec((1, seg, tc), lambda b, c, f: (b, row_blk, c)),
                pl.BlockSpec((tf, seg), lambda b, c, f: (f, 0)),
                pl.BlockSpec((tf, seg), lambda b, c, f: (f + nf, 0)),
                pl.BlockSpec((seg, tf), lambda b, c, f: (0, f)),
                pl.BlockSpec((seg, tf), lambda b, c, f: (0, f + nf)),
                pl.BlockSpec((tf, tc), lambda b, c, f: (f, c)),
                pl.BlockSpec((tf, tc), lambda b, c, f: (f, c)),
                pl.BlockSpec((1, tc), lambda b, c, f: (0, c))]
    args = [u, x0, wtab, wtab, wtab_t, wtab_t, ka, kb, bias]
    aliases = {}
    if prev is not None:
        in_specs.append(pl.BlockSpec(memory_space=pl.ANY))
        args.append(prev)
        aliases = {len(args) - 1: 0}
    return pl.pallas_call(
        functools.partial(_hyconv_kernel, nf=nf, inv_n=1.0 / (2 * seg)),
        grid=(BATCH, nct, nf),
        in_specs=in_specs,
        out_specs=pl.BlockSpec((1, seg, tc), lambda b, c, f: (b, row_blk, c)),
        out_shape=jax.ShapeDtypeStruct((BATCH, S, D), bf16),
        scratch_shapes=[pltpu.VMEM((seg, tc), f32)],
        input_output_aliases=aliases,
        compiler_params=_cparams(("parallel", "parallel", "arbitrary"), 48),
        name=f"hyena_conv_{seg}",
    )(*args)


def _rope(x, cos_t, sin_t):
    n = x.shape[1]
    lane = lax.broadcasted_iota(jnp.int32, x.shape, 1)
    first = (lane & 16) == 0
    sw = jnp.where(first, pltpu.roll(x, n - 16, 1), pltpu.roll(x, 16, 1))
    return x * cos_t + sw * sin_t


def _attn_kernel(sink_ref, q_ref, kv_ref, cos_ref, sin_ref, o_ref, q_scr, k_scr, v_scr):
    g = pl.program_id(1)
    gw = ATT_GROUP * HEAD_P
    qscale = HEAD_P ** -0.5

    def prep(r0, with_rope):
        q = q_ref[0, pl.ds(r0, RC), :].astype(f32) * qscale
        kv = kv_ref[0, pl.ds(r0, RC), :].astype(f32)
        sw = pltpu.roll(kv, HEAD_P, 1)
        lane = lax.broadcasted_iota(jnp.int32, kv.shape, 1)
        kk = jnp.where(lane < HEAD_P, kv, sw)
        vv = jnp.where(lane < HEAD_P, sw, kv)
        krep = jnp.concatenate([kk, kk], axis=1)
        if with_rope:
            cos_t, sin_t = cos_ref[pl.ds(r0, RC), :], sin_ref[pl.ds(r0, RC), :]
            q = _rope(q, cos_t, sin_t)
            krep = _rope(krep, cos_t, sin_t)
        q_scr[pl.ds(r0, RC), :] = q.astype(bf16)
        k_scr[pl.ds(r0, RC), :] = krep.astype(bf16)
        v_scr[pl.ds(r0, RC), :] = jnp.concatenate([vv, vv], axis=1).astype(bf16)
    _row_chunks(SEQ, lambda r0: prep(r0, True))
    for r0 in range(SEQ, S, RC):
        prep(r0, False)

    hlane = lax.broadcasted_iota(jnp.int32, (ABLK, gw), 1) >> 6
    k_ctx = k_scr[SEQ:S, :]
    v_ctx = v_scr[SEQ:S, :]

    def softmax_pv(qb, parts, r):
        sink = sink_ref[g * ATT_GROUP + r]
        qh = jnp.where(hlane == r, qb, jnp.zeros_like(qb))
        ss = []
        m = jnp.full((ABLK, 1), sink, f32)
        for ok, keys, _ in parts:
            s = _bdot_nt(qh, keys)
            if ok is not None:
                s = jnp.where(ok, s, -jnp.inf)
            ss.append(s)
            m = jnp.maximum(m, jnp.max(s, axis=-1, keepdims=True))
        den = jnp.exp(sink - m)
        pv = jnp.zeros((ABLK, gw), f32)
        for s, (_, _, vals) in zip(ss, parts):
            p = jnp.exp(s - m)
            den = den + jnp.sum(p, axis=-1, keepdims=True)
            pv = pv + _bdot(p.astype(bf16), vals)
        return jnp.where(hlane == r, pv / den, 0.0)

    span = ABLK + 2 * WINDOW

    def lat_block(i, carry):
        q0 = pl.multiple_of(i * ABLK, ABLK)
        start = pl.multiple_of(jnp.clip(q0 - WINDOW, 0, SEQ - span), ABLK)
        qb = q_scr[pl.ds(q0, ABLK), :]
        k_loc = k_scr[pl.ds(start, span), :]
        v_loc = v_scr[pl.ds(start, span), :]
        dpos = (lax.broadcasted_iota(jnp.int32, (ABLK, span), 1)
                - lax.broadcasted_iota(jnp.int32, (ABLK, span), 0)) + (start - q0)
        ok = jnp.abs(dpos) <= WINDOW
        o = jnp.zeros((ABLK, gw), f32)
        for r in range(ATT_GROUP):
            o = o + softmax_pv(qb, [(ok, k_loc, v_loc), (None, k_ctx, v_ctx)], r)
        o_ref[0, pl.ds(q0, ABLK), :] = o.astype(bf16)
        return carry
    lax.fori_loop(0, SEQ // ABLK, lat_block, 0)

    for q0 in range(SEQ, S, ABLK):
        qb = q_scr[q0:q0 + ABLK, :]
        o = jnp.zeros((ABLK, gw), f32)
        for r in range(ATT_GROUP):
            o = o + softmax_pv(qb, [(None, k_ctx, v_ctx)], r)
        o_ref[0, q0:q0 + ABLK, :] = o.astype(bf16)


def _attn_call(qkv, sinks, cos_t, sin_t):
    gw = ATT_GROUP * HEAD_P
    return pl.pallas_call(
        _attn_kernel,
        grid_spec=pltpu.PrefetchScalarGridSpec(
            num_scalar_prefetch=1,
            grid=(BATCH, KV_HEADS),
            in_specs=[pl.BlockSpec((1, S, gw), lambda b, g, s: (b, 0, g)),
                      pl.BlockSpec((1, S, 2 * HEAD_P), lambda b, g, s: (b, 0, QW // (2 * HEAD_P) + g)),
                      pl.BlockSpec((SEQ, gw), lambda b, g, s: (0, 0)),
                      pl.BlockSpec((SEQ, gw), lambda b, g, s: (0, 0))],
            out_specs=pl.BlockSpec((1, S, gw), lambda b, g, s: (b, 0, g)),
            scratch_shapes=[pltpu.VMEM((S, gw), bf16), pltpu.VMEM((S, gw), bf16),
                            pltpu.VMEM((S, gw), bf16)]),
        out_shape=jax.ShapeDtypeStruct((BATCH, S, QW), bf16),
        compiler_params=_cparams(("parallel", "parallel"), 40),
        name="window_attn",
    )(sinks, qkv, qkv, cos_t, sin_t)


RES_TM = 768


def _outproj_kernel(x_ref, mod_ref, *rest, n_in):
    a_refs, w_refs, o_ref = rest[:n_in], rest[n_in:2 * n_in], rest[2 * n_in]
    for r0 in range(0, RES_TM, RC):
        acc = _bdot(a_refs[0][0, r0:r0 + RC, :], w_refs[0][...])
        for a_ref, w_ref in zip(a_refs[1:], w_refs[1:]):
            acc = acc + _bdot(a_ref[0, r0:r0 + RC, :], w_ref[...])
        gate = _gate_rows(mod_ref[0], 2, acc.shape, pl.program_id(1) * RES_TM + r0)
        o_ref[0, r0:r0 + RC, :] = x_ref[0, r0:r0 + RC, :] + gate * acc


def _outproj_call(xs, mods_l, acts, ws, name):
    n_in = len(acts)
    in_specs = [pl.BlockSpec((1, RES_TM, D), lambda b, i: (b, i, 0)),
                pl.BlockSpec((1, MOD_ROWS, D), lambda b, i: (b, 0, 0))]
    in_specs += [pl.BlockSpec((1, RES_TM, a.shape[2]), lambda b, i: (b, i, 0)) for a in acts]
    in_specs += [pl.BlockSpec(w.shape, lambda b, i: (0, 0)) for w in ws]
    return pl.pallas_call(
        functools.partial(_outproj_kernel, n_in=n_in),
        grid=(BATCH, S // RES_TM),
        in_specs=in_specs,
        out_specs=pl.BlockSpec((1, RES_TM, D), lambda b, i: (b, i, 0)),
        out_shape=jax.ShapeDtypeStruct((BATCH, S, D), f32),
        input_output_aliases={0: 0},
        compiler_params=_cparams(("parallel", "parallel"), 40),
        name=name,
    )(xs, mods_l, *acts, *ws)


FFN_TM = 768
FFN_TH = 256


def _ffn_kernel(x_ref, mod_ref, g_ref, wg_ref, wu_ref, wo_ref, o_ref, h_scr, acc):
    k = pl.program_id(2)
    row0 = pl.program_id(1) * FFN_TM

    @pl.when(k == 0)
    def _():
        for r0 in range(0, FFN_TM, RC):
            h_scr[r0:r0 + RC, :] = _norm_mod(x_ref[0, r0:r0 + RC, :], g_ref[...], mod_ref[0],
                                             3, 4, row0 + r0).astype(bf16)
        acc[...] = jnp.zeros_like(acc)

    for r0 in range(0, FFN_TM, RC):
        h = h_scr[r0:r0 + RC, :]
        act = (_silu(_bdot(h, wg_ref[...])) * _bdot(h, wu_ref[...])).astype(bf16)
        acc[r0:r0 + RC, :] += _bdot(act, wo_ref[...])

    @pl.when(k == FFN_H // FFN_TH - 1)
    def _():
        for r0 in range(0, FFN_TM, RC):
            gate = _gate_rows(mod_ref[0], 5, (RC, D), row0 + r0)
            o_ref[0, r0:r0 + RC, :] = x_ref[0, r0:r0 + RC, :] + gate * acc[r0:r0 + RC, :]


def _ffn_call(xs, mods_l, g, w_in, w_out):
    nh = FFN_H // FFN_TH
    return pl.pallas_call(
        _ffn_kernel,
        grid=(BATCH, S // FFN_TM, nh),
        in_specs=[pl.BlockSpec((1, FFN_TM, D), lambda b, i, k: (b, i, 0)),
                  pl.BlockSpec((1, MOD_ROWS, D), lambda b, i, k: (b, 0, 0)),
                  pl.BlockSpec((1, D), lambda b, i, k: (0, 0)),
                  pl.BlockSpec((D, FFN_TH), lambda b, i, k: (0, k)),
                  pl.BlockSpec((D, FFN_TH), lambda b, i, k: (0, k + nh)),
                  pl.BlockSpec((FFN_TH, D), lambda b, i, k: (k, 0))],
        out_specs=pl.BlockSpec((1, FFN_TM, D), lambda b, i, k: (b, i, 0)),
        out_shape=jax.ShapeDtypeStruct((BATCH, S, D), f32),
        scratch_shapes=[pltpu.VMEM((FFN_TM, D), bf16), pltpu.VMEM((FFN_TM, D), f32)],
        input_output_aliases={0: 0},
        compiler_params=_cparams(("parallel", "parallel", "arbitrary"), 40),
        name="swiglu_ffn",
    )(xs, mods_l, g, w_in, w_in, w_out)


FIN_TM = 256


def _final_kernel(x_ref, g_ref, o_ref):
    x = x_ref[0]
    ms = jnp.mean(x * x, axis=-1, keepdims=True)
    o_ref[0] = x * lax.rsqrt(ms + EPS) * g_ref[...]


def _final_call(xs, g):
    return pl.pallas_call(
        _final_kernel,
        grid=(BATCH, SEQ // FIN_TM),
        in_specs=[pl.BlockSpec((1, FIN_TM, D), lambda b, i: (b, i, 0)),
                  pl.BlockSpec((1, D), lambda b, i: (0, 0))],
        out_specs=pl.BlockSpec((1, FIN_TM, D), lambda b, i: (b, i, 0)),
        out_shape=jax.ShapeDtypeStruct((BATCH, SEQ, D), f32),
        compiler_params=_cparams(("parallel", "parallel"), 16),
        name="final_norm",
    )(xs, g)


def _dft_table(seg):
    n = 2 * seg
    f = jnp.arange(seg, dtype=jnp.int32)[:, None]
    t = jnp.arange(seg, dtype=jnp.int32)[None, :]
    ang = (2.0 * math.pi / n) * ((f * t) % n).astype(f32)
    nyq = jnp.where((t & 1) == 0, 1.0, -1.0).astype(f32)
    sin_rows = jnp.where(f == 0, nyq, jnp.sin(ang))
    return jnp.concatenate([jnp.cos(ang), sin_rows], axis=0)


def _hy_features(seg):
    t = jnp.arange(seg, dtype=f32)
    t_unit = t / (seg - 1)
    ang = 2.0 * math.pi * t / seg
    fb = jnp.linspace(1e-4, HY_BANDS - 1, HY_BANDS, dtype=f32)
    z = jnp.concatenate([t_unit[:, None], jnp.cos(ang[:, None] * fb), -jnp.sin(ang[:, None] * fb)], axis=-1)
    z = jnp.pad(z, ((0, 0), (0, 128 - HY_EMB)))
    deltas = jnp.linspace(math.log(1e-2) / 1.5, math.log(1e-2) / 0.3, D, dtype=f32)
    win = jnp.exp(-t_unit[:, None] * jnp.abs(deltas)[None, :])
    return z, win


def _rope_tables():
    nf = HEAD_P // 4
    inv = ROPE_BASE ** (-jnp.arange(nf, dtype=f32) / nf)
    pos = jnp.arange(SEQ)
    row = (pos // GRID_W).astype(f32)
    col = (pos % GRID_W).astype(f32)
    ar = row[:, None] * inv[None, :]
    ac = col[:, None] * inv[None, :]
    ang = jnp.concatenate([ar, ar, ac, ac], axis=-1)
    sign = jnp.tile(jnp.concatenate([-jnp.ones(nf, f32), jnp.ones(nf, f32)]), 2)
    cos_t = jnp.tile(jnp.cos(ang), (1, ATT_GROUP))
    sin_t = jnp.tile(jnp.sin(ang) * sign[None, :], (1, ATT_GROUP))
    return cos_t, sin_t


def _head_expand(lane_off):
    r = jnp.arange(DT_LANES)[:, None]
    c = jnp.arange(D)[None, :]
    return (c // HEAD_P + lane_off == r).astype(bf16)


def _pad_rows(a, rows):
    return jnp.pad(a, ((0, rows - a.shape[0]), (0, 0)))


def _pad_lanes(a, lanes):
    return jnp.pad(a, ((0, 0), (0, lanes - a.shape[1])))


def kernel(x, c, ctx, c_ctx, ada_w, ada_b, norm1_g, norm2_g, hy_in_w, ssd_conv_w, ssd_conv_b, ssd_dt_bias, ssd_a_log, ssd_d, ssd_norm_g, hy_conv_w, hy_conv_b, hy_w1, hy_b1, hy_w2, hy_b2, hy_w3, hy_b3, hy_w4, hy_freq, hy_bias, hy_out_w, attn_qkv_w, attn_sinks, attn_out_w, ffn_w_in, ffn_w_out, final_g):
    xs = jnp.concatenate([x, ctx], axis=1)

    cc = jnp.concatenate([c, c_ctx[None, :], jnp.zeros((MOD_ROWS - BATCH - 1, D), f32)], axis=0)
    modv = _mod_call(cc, ada_w, ada_b)
    lat = modv[:, :BATCH].reshape(DEPTH, BATCH, N_MOD, D)
    cmod = jnp.broadcast_to(modv[:, BATCH].reshape(DEPTH, 1, N_MOD, D), (DEPTH, BATCH, N_MOD, D))
    pad = jnp.zeros((DEPTH, BATCH, CTX_MOD - N_MOD, D), f32)
    mods = jnp.concatenate([lat, pad, cmod, pad], axis=2)

    tabs = {}
    for seg in (SEQ, CTX):
        w = _dft_table(seg)
        tabs[seg] = (w.astype(bf16), w.T.astype(bf16)) + _hy_features(seg)
    e_f, e_b = _head_expand(0), _head_expand(HEADS)
    cos_t, sin_t = _rope_tables()

    for layer in range(DEPTH):
        j = layer // 2
        mods_l = mods[layer]
        g1 = norm1_g[layer][None, :]
        if layer % 2 == 0:
            w_in = hy_in_w[j]
            w_zx = w_in[:, :COL_DT].astype(bf16)
            wdt = _pad_lanes(w_in[:, COL_DT:COL_HY], DT_LANES).astype(bf16)
            dtb = _pad_lanes(ssd_dt_bias[j].reshape(1, 2 * HEADS), DT_LANES)
            cw = jnp.concatenate([jnp.zeros((4, D), f32),
                                  jnp.concatenate([ssd_conv_w[j], ssd_conv_b[j][None, :]], axis=0)], axis=1)
            zx, dt = _inproj_ssd_call(xs, mods_l, g1, w_zx, _pad_rows(cw, 8), wdt, dtb)
            cw_hy = _pad_rows(jnp.concatenate([hy_conv_w[j], hy_conv_b[j][None, :]], axis=0), 8)
            x0, u = _inproj_hy_call(xs, mods_l, g1, w_in[:, COL_HY:].astype(bf16), cw_hy)

            alog = _pad_lanes(ssd_a_log[j].reshape(1, 2 * HEADS), DT_LANES)
            dskip = jnp.repeat(ssd_d[j], HEAD_P)[None, :]
            yn = _ssd_call(zx, dt, alog, dskip, ssd_norm_g[j][None, :], e_f, e_b)

            gh = None
            for seg, row_blk, tc, tf in ((SEQ, 0, 512, 512), (CTX, SEQ // CTX, 512, CTX)):
                wtab, wtab_t, zf, win = tabs[seg]
                ka, kb = _hyfilter_call(zf, _pad_rows(hy_w1[j], 128), hy_b1[j][None, :], hy_w2[j],
                                        hy_b2[j][None, :], hy_w3[j], hy_b3[j][None, :],
                                        _pad_rows(hy_freq[j], 8), hy_w4[j], win, wtab, seg, tc, tf)
                gh = _hyconv_call(u, x0, wtab, wtab_t, ka, kb, hy_bias[j][None, :], seg, row_blk, tc, tf,
                                  prev=gh)
            w_out = hy_out_w[j].astype(bf16)
            xs = _outproj_call(xs, mods_l, [yn, gh], [w_out[:D], w_out[D:]], "outproj_even")
        else:
            wq = attn_qkv_w[j]
            wk = wq[:, QW:QW + KVW].reshape(D, KV_HEADS, HEAD_P)
            wv = wq[:, QW + KVW:].reshape(D, KV_HEADS, HEAD_P)
            w_kv = jnp.concatenate([wk, wv], axis=2).reshape(D, 2 * KVW)
            w_qkv = jnp.concatenate([wq[:, :QW], w_kv], axis=1).astype(bf16)
            qkv = _nmm_call(xs, mods_l, g1, w_qkv, 512, "qkv_proj")
            o = _attn_call(qkv, attn_sinks[j], cos_t, sin_t)
            xs = _outproj_call(xs, mods_l, [o], [attn_out_w[j].astype(bf16)], "outproj_attn")
        xs = _ffn_call(xs, mods_l, norm2_g[layer][None, :], ffn_w_in[layer].astype(bf16),
                       ffn_w_out[layer].astype(bf16))
    return _final_call(xs, final_g[None, :])
```

```python
import functools
import math

import jax
import jax.numpy as jnp
from jax import lax
from jax.experimental import pallas as pl
from jax.experimental.pallas import tpu as pltpu

f32 = jnp.float32
bf16 = jnp.bfloat16

D = 1024
BATCH = 8
SEQ = 2048
DEPTH = 4
GRID_W = 64
CTX = 256
S = SEQ + CTX
EPS = 1e-6
N_MOD = 6
MOD_ROWS = 16
CTX_MOD = 8

HEADS = 16
HEAD_P = 64
STATE = 128
GROUPS = 2
CHUNK = 128
XBC = D + 2 * GROUPS * STATE
ZX = D + XBC
DT_LANES = 128
HY_BANDS = 16
HY_EMB = 1 + 2 * HY_BANDS
HY_FFN = 64
COL_XBC = D
COL_DT = COL_XBC + XBC
COL_HY = COL_DT + 2 * HEADS

Q_HEADS = 16
KV_HEADS = 4
ATT_GROUP = Q_HEADS // KV_HEADS
QW = Q_HEADS * HEAD_P
KVW = KV_HEADS * HEAD_P
WINDOW = 128
ABLK = 128
ROPE_BASE = 10000.0
FFN_H = -(-8 * D // (3 * 256)) * 256

RC = 256
VMEM_CAP = 56 * 1024 * 1024


def _cparams(sem, vmem_mb):
    return pltpu.CompilerParams(dimension_semantics=sem,
                                vmem_limit_bytes=min(int(vmem_mb * 1024 * 1024), VMEM_CAP))


def _silu(v):
    return v * jax.nn.sigmoid(v)


def _softplus(v):
    return jnp.maximum(v, 0.0) + jnp.log1p(jnp.exp(-jnp.abs(v)))


def _bdot(a, b):
    return jnp.dot(a, b, preferred_element_type=f32)


def _bdot_nt(a, b):
    return lax.dot_general(a, b, (((1,), (1,)), ((), ())), preferred_element_type=f32)


def _row_chunks(n_rows, body):
    def step(c, carry):
        body(pl.multiple_of(c * RC, RC))
        return carry
    lax.fori_loop(0, n_rows // RC, step, 0)


def _norm_mod(x, g, mod, shift_row, scale_row, row0):
    ms = jnp.mean(x * x, axis=-1, keepdims=True)
    scale = _mod_row(mod, scale_row, row0)
    return x * lax.rsqrt(ms + EPS) * (g * (1.0 + scale)) + _mod_row(mod, shift_row, row0)


def _mod_row(mod, k, row0):
    return jnp.where(row0 >= SEQ, mod[CTX_MOD + k:CTX_MOD + k + 1], mod[k:k + 1])


def _norm_mod_to_scratch(x_ref, g_ref, mod_ref, h_scr):
    def body(r0):
        h_scr[pl.ds(r0, RC), :] = _norm_mod(x_ref[0, pl.ds(r0, RC), :], g_ref[...], mod_ref[0],
                                            0, 1, r0).astype(bf16)
    _row_chunks(S, body)


CONV_ROWS = 64


def _conv3_rows(p_scr, r0, cw):
    n = CONV_ROWS
    v = p_scr[pl.ds(r0, n), :]
    if isinstance(r0, int):
        up_row, dn_row = max(r0 - 8, 0), min(r0 + n, S - 8)
        seg_start, seg_end = r0 in (0, SEQ), r0 + n in (SEQ, S)
    else:
        up_row = pl.multiple_of(jnp.maximum(r0 - 8, 0), 8)
        dn_row = pl.multiple_of(jnp.minimum(r0 + n, S - 8), 8)
        seg_start = jnp.logical_or(r0 == 0, r0 == SEQ)
        seg_end = jnp.logical_or(r0 + n == SEQ, r0 + n == S)
    up = jnp.where(seg_start, 0.0, p_scr[pl.ds(up_row, 8), :][7:8])
    dn = jnp.where(seg_end, 0.0, p_scr[pl.ds(dn_row, 8), :][0:1])
    sub = lax.broadcasted_iota(jnp.int32, (8, v.shape[1]), 0)
    prev = pltpu.roll(v, 1, 0)
    prev = jnp.concatenate([jnp.where(sub == 0, up, prev[0:8]), prev[8:]], axis=0)
    nxt = pltpu.roll(v, n - 1, 0)
    nxt = jnp.concatenate([nxt[:n - 8], jnp.where(sub == 7, dn, nxt[n - 8:])], axis=0)
    return prev * cw[0:1] + v * cw[1:2] + nxt * cw[2:3] + cw[3:4]


def _conv_pieces(r0, piece):
    for k in range(0, RC, CONV_ROWS):
        piece(r0 + k if isinstance(r0, int) else pl.multiple_of(r0 + k, CONV_ROWS))


def _project_then_conv(mm, conv):
    mm(0)
    mm(RC)

    def step(c, carry):
        r0 = pl.multiple_of(c * RC, RC)
        conv(pl.multiple_of(r0 - 2 * RC, RC))
        mm(r0)
        return carry
    lax.fori_loop(2, S // RC, step, 0)
    conv(S - 2 * RC)
    conv(S - RC)


def _mod_kernel(c_ref, w_ref, b_ref, o_ref):
    sc = _silu(c_ref[...]).astype(bf16)
    o_ref[0] = _bdot(sc, w_ref[0].astype(bf16)) + b_ref[0]


def _mod_call(cc, ada_w, ada_b):
    tn = 1024
    return pl.pallas_call(
        _mod_kernel,
        grid=(DEPTH, N_MOD * D // tn),
        in_specs=[pl.BlockSpec((MOD_ROWS, D), lambda l, j: (0, 0)),
                  pl.BlockSpec((1, D, tn), lambda l, j: (l, 0, j)),
                  pl.BlockSpec((1, 1, tn), lambda l, j: (l, 0, j))],
        out_specs=pl.BlockSpec((1, MOD_ROWS, tn), lambda l, j: (l, 0, j)),
        out_shape=jax.ShapeDtypeStruct((DEPTH, MOD_ROWS, N_MOD * D), f32),
        compiler_params=_cparams(("parallel", "parallel"), 24),
        name="adaln_mod",
    )(cc, ada_w, ada_b.reshape(DEPTH, 1, N_MOD * D))


def _nmm_kernel(x_ref, mod_ref, g_ref, w_ref, o_ref, h_scr):
    @pl.when(pl.program_id(1) == 0)
    def _():
        _norm_mod_to_scratch(x_ref, g_ref, mod_ref, h_scr)

    def body(r0):
        o_ref[0, pl.ds(r0, RC), :] = _bdot(h_scr[pl.ds(r0, RC), :], w_ref[...]).astype(o_ref.dtype)
    _row_chunks(S, body)


def _nmm_call(xs, mods_l, g, w, tn, name):
    n = w.shape[1]
    return pl.pallas_call(
        _nmm_kernel,
        grid=(BATCH, n // tn),
        in_specs=[pl.BlockSpec((1, S, D), lambda b, j: (b, 0, 0)),
                  pl.BlockSpec((1, MOD_ROWS, D), lambda b, j: (b, 0, 0)),
                  pl.BlockSpec((1, D), lambda b, j: (0, 0)),
                  pl.BlockSpec((D, tn), lambda b, j: (0, j))],
        out_specs=pl.BlockSpec((1, S, tn), lambda b, j: (b, 0, j)),
        out_shape=jax.ShapeDtypeStruct((BATCH, S, n), bf16),
        scratch_shapes=[pltpu.VMEM((S, D), bf16)],
        compiler_params=_cparams(("parallel", "arbitrary"), 40),
        name=name,
    )(xs, mods_l, g, w)


ZX_TN = 512
NZ_T = D // ZX_TN


def _inproj_ssd_kernel(x_ref, mod_ref, g_ref, w_ref, cw_ref, wdt_ref, dtb_ref, zx_ref, dt_ref,
                       h_scr, p_scr):
    j = pl.program_id(1)

    @pl.when(j == 0)
    def _():
        _norm_mod_to_scratch(x_ref, g_ref, mod_ref, h_scr)

        def dt_body(r0):
            sp = _softplus(_bdot(h_scr[pl.ds(r0, RC), :], wdt_ref[...]) + dtb_ref[...])
            lane = lax.broadcasted_iota(jnp.int32, sp.shape, 1)
            dt_ref[0, pl.ds(r0, RC), :] = jnp.where(lane < 2 * HEADS, sp, 0.0)
        _row_chunks(S, dt_body)

    @pl.when(j < NZ_T)
    def _():
        def body(r0):
            zx_ref[0, pl.ds(r0, RC), :] = _bdot(h_scr[pl.ds(r0, RC), :], w_ref[...]).astype(bf16)
        _row_chunks(S, body)

    @pl.when(j >= NZ_T)
    def _():
        def mm(r0):
            p_scr[pl.ds(r0, RC), :] = _bdot(h_scr[pl.ds(r0, RC), :], w_ref[...])

        def conv(r0):
            def piece(a):
                zx_ref[0, pl.ds(a, CONV_ROWS), :] = _silu(_conv3_rows(p_scr, a, cw_ref[...])).astype(bf16)
            _conv_pieces(r0, piece)
        _project_then_conv(mm, conv)


def _inproj_ssd_call(xs, mods_l, g, w_zx, cw, wdt, dtb):
    return pl.pallas_call(
        _inproj_ssd_kernel,
        grid=(BATCH, ZX // ZX_TN),
        in_specs=[pl.BlockSpec((1, S, D), lambda b, j: (b, 0, 0)),
                  pl.BlockSpec((1, MOD_ROWS, D), lambda b, j: (b, 0, 0)),
                  pl.BlockSpec((1, D), lambda b, j: (0, 0)),
                  pl.BlockSpec((D, ZX_TN), lambda b, j: (0, j)),
                  pl.BlockSpec((8, ZX_TN), lambda b, j: (0, j)),
                  pl.BlockSpec((D, DT_LANES), lambda b, j: (0, 0)),
                  pl.BlockSpec((1, DT_LANES), lambda b, j: (0, 0))],
        out_specs=[pl.BlockSpec((1, S, ZX_TN), lambda b, j: (b, 0, j)),
                   pl.BlockSpec((1, S, DT_LANES), lambda b, j: (b, 0, 0))],
        out_shape=[jax.ShapeDtypeStruct((BATCH, S, ZX), bf16),
                   jax.ShapeDtypeStruct((BATCH, S, DT_LANES), f32)],
        scratch_shapes=[pltpu.VMEM((S, D), bf16), pltpu.VMEM((S, ZX_TN), f32)],
        compiler_params=_cparams(("parallel", "arbitrary"), 48),
        name="inproj_ssd",
    )(xs, mods_l, g, w_zx, cw, wdt, dtb)


HY_TN = 256


def _inproj_hy_kernel(x_ref, mod_ref, g_ref, w0_ref, w1_ref, w2_ref, c0_ref, c1_ref, c2_ref,
                      x0_ref, u_ref, h_scr, p0_scr, p1_scr, p2_scr):
    @pl.when(pl.program_id(1) == 0)
    def _():
        _norm_mod_to_scratch(x_ref, g_ref, mod_ref, h_scr)

    def mm_x0(r0):
        p0_scr[pl.ds(r0, RC), :] = _bdot(h_scr[pl.ds(r0, RC), :], w0_ref[...])

    def conv_x0(r0):
        def piece(a):
            x0_ref[0, pl.ds(a, CONV_ROWS), :] = _conv3_rows(p0_scr, a, c0_ref[...]).astype(bf16)
        _conv_pieces(r0, piece)
    _project_then_conv(mm_x0, conv_x0)

    def mm_u(r0):
        h = h_scr[pl.ds(r0, RC), :]
        p1_scr[pl.ds(r0, RC), :] = _bdot(h, w1_ref[...])
        p2_scr[pl.ds(r0, RC), :] = _bdot(h, w2_ref[...])

    def conv_u(r0):
        def piece(a):
            x1 = _conv3_rows(p1_scr, a, c1_ref[...])
            v = _conv3_rows(p2_scr, a, c2_ref[...])
            u_ref[0, pl.ds(a, CONV_ROWS), :] = (x1 * v).astype(bf16)
        _conv_pieces(r0, piece)
    _project_then_conv(mm_u, conv_u)


def _inproj_hy_call(xs, mods_l, g, w_hy, cw_hy):
    nt = D // HY_TN
    wspec = lambda k: pl.BlockSpec((D, HY_TN), lambda b, j, k=k: (0, j + k * nt))
    cspec = lambda k: pl.BlockSpec((8, HY_TN), lambda b, j, k=k: (0, j + k * nt))
    return pl.pallas_call(
        _inproj_hy_kernel,
        grid=(BATCH, nt),
        in_specs=[pl.BlockSpec((1, S, D), lambda b, j: (b, 0, 0)),
                  pl.BlockSpec((1, MOD_ROWS, D), lambda b, j: (b, 0, 0)),
                  pl.BlockSpec((1, D), lambda b, j: (0, 0)),
                  wspec(0), wspec(1), wspec(2), cspec(0), cspec(1), cspec(2)],
        out_specs=[pl.BlockSpec((1, S, HY_TN), lambda b, j: (b, 0, j)),
                   pl.BlockSpec((1, S, HY_TN), lambda b, j: (b, 0, j))],
        out_shape=[jax.ShapeDtypeStruct((BATCH, S, D), bf16),
                   jax.ShapeDtypeStruct((BATCH, S, D), bf16)],
        scratch_shapes=[pltpu.VMEM((S, D), bf16)] + [pltpu.VMEM((S, HY_TN), f32)] * 3,
        compiler_params=_cparams(("parallel", "arbitrary"), 48),
        name="inproj_hyena",
    )(xs, mods_l, g, w_hy, w_hy, w_hy, cw_hy, cw_hy, cw_hy)


N_CHUNK = S // CHUNK
LAT_CHUNKS = SEQ // CHUNK


def _split3_dot(t, a):
    a1 = a.astype(bf16)
    r1 = a - a1.astype(f32)
    a2 = r1.astype(bf16)
    a3 = (r1 - a2.astype(f32)).astype(bf16)
    return _bdot(t, a1) + _bdot(t, a2) + _bdot(t, a3)


def _ssd_chunk(zx_ref, dt_ref, aneg, e_ref, y_scr, st_scr, r0, backward):
    lane_off = HEADS if backward else 0
    x = zx_ref[0, pl.ds(r0, CHUNK), D:2 * D].astype(f32)
    bm = zx_ref[0, pl.ds(r0, CHUNK), 2 * D:2 * D + GROUPS * STATE]
    cm = zx_ref[0, pl.ds(r0, CHUNK), 2 * D + GROUPS * STATE:ZX]
    dt = dt_ref[0, pl.ds(r0, CHUNK), :]
    a = dt * aneg

    ri = lax.broadcasted_iota(jnp.int32, (CHUNK, CHUNK), 0)
    ci = lax.broadcasted_iota(jnp.int32, (CHUNK, CHUNK), 1)
    causal = (ci >= ri) if backward else (ci <= ri)
    tmat = jnp.where(causal, 1.0, 0.0).astype(bf16)
    acum = _split3_dot(tmat, a)
    acum_t = acum.T
    tot = acum[0:1] if backward else acum[CHUNK - 1:CHUNK]
    dte = jnp.exp(tot - acum)
    eac = jnp.exp(acum)
    cdec = jnp.broadcast_to(jnp.exp(tot), (16, DT_LANES))

    e = e_ref[...]
    dt_x = _bdot(dt.astype(bf16), e)
    dte_x = _bdot(dte.astype(bf16), e)
    eac_x = _bdot(eac.astype(bf16), e)
    cdec_x = _bdot(cdec.astype(bf16), e)[0:1]

    xdt = x * dt_x
    xdt_b = xdt.astype(bf16)
    xs_b = (xdt * dte_x).astype(bf16)
    lane = lax.broadcasted_iota(jnp.int32, (CHUNK, 2 * HEAD_P), 1)
    low = lane < HEAD_P
    hg = HEADS // GROUPS
    gw = hg * HEAD_P
    for g in range(GROUPS):
        bg = bm[:, g * STATE:(g + 1) * STATE]
        cg = cm[:, g * STATE:(g + 1) * STATE]
        cb = _bdot_nt(cg, bg)
        hin = st_scr[:, g * gw:(g + 1) * gw]
        yoff = _bdot(cg, hin.astype(bf16)) * eac_x[:, g * gw:(g + 1) * gw]
        for hp in range(hg // 2):
            h0 = g * hg + 2 * hp
            ms = []
            for h in (h0, h0 + 1):
                l0 = lane_off + h
                seg = acum[:, l0:l0 + 1] - acum_t[l0:l0 + 1, :]
                ms.append(cb * jnp.exp(jnp.where(causal, seg, -jnp.inf)))
            lhs = jnp.concatenate(ms, axis=1).astype(bf16)
            xp = xdt[:, h0 * HEAD_P:(h0 + 2) * HEAD_P]
            rhs = jnp.concatenate([jnp.where(low, xp, 0.0), jnp.where(low, 0.0, xp)],
                                  axis=0).astype(bf16)
            yd = _bdot(lhs, rhs) + yoff[:, hp * 2 * HEAD_P:(hp + 1) * 2 * HEAD_P]
            cols = slice(h0 * HEAD_P, (h0 + 2) * HEAD_P)
            y_scr[pl.ds(r0, CHUNK), cols] = y_scr[pl.ds(r0, CHUNK), cols] + yd
        bg_t = bg.astype(f32).T.astype(bf16)
        st_scr[:, g * gw:(g + 1) * gw] = (hin * cdec_x[:, g * gw:(g + 1) * gw]
                                         + _bdot(bg_t, xs_b[:, g * gw:(g + 1) * gw]))
    del xdt_b


def _ssd_kernel(zx_ref, dt_ref, alog_ref, dskip_ref, ng_ref, ef_ref, eb_ref, o_ref,
                y_scr, sf_scr, sb_scr):
    aneg = -jnp.exp(alog_ref[...])
    sf_scr[...] = jnp.zeros_like(sf_scr)
    sb_scr[...] = jnp.zeros_like(sb_scr)

    def init(c, carry):
        r0 = pl.multiple_of(c * CHUNK, CHUNK)
        y_scr[pl.ds(r0, CHUNK), :] = dskip_ref[...] * zx_ref[0, pl.ds(r0, CHUNK), D:2 * D].astype(f32)
        return carry
    lax.fori_loop(0, N_CHUNK, init, 0)

    def step(t, carry):
        cf = jnp.where(t < N_CHUNK - LAT_CHUNKS, LAT_CHUNKS + t, t - (N_CHUNK - LAT_CHUNKS))
        cbk = N_CHUNK - 1 - t
        _ssd_chunk(zx_ref, dt_ref, aneg, ef_ref, y_scr, sf_scr, pl.multiple_of(cf * CHUNK, CHUNK), False)
        _ssd_chunk(zx_ref, dt_ref, aneg, eb_ref, y_scr, sb_scr, pl.multiple_of(cbk * CHUNK, CHUNK), True)
        return carry
    lax.fori_loop(0, N_CHUNK, step, 0)

    def fin(c, carry):
        r0 = pl.multiple_of(c * CHUNK, CHUNK)
        t = y_scr[pl.ds(r0, CHUNK), :] * _silu(zx_ref[0, pl.ds(r0, CHUNK), 0:D].astype(f32))
        ms = jnp.mean(t * t, axis=-1, keepdims=True)
        o_ref[0, pl.ds(r0, CHUNK), :] = (t * lax.rsqrt(ms + EPS) * ng_ref[...]).astype(bf16)
        return carry
    lax.fori_loop(0, N_CHUNK, fin, 0)


def _ssd_call(zx, dt, alog, dskip, ng, e_f, e_b):
    return pl.pallas_call(
        _ssd_kernel,
        grid=(BATCH,),
        in_specs=[pl.BlockSpec((1, S, ZX), lambda b: (b, 0, 0)),
                  pl.BlockSpec((1, S, DT_LANES), lambda b: (b, 0, 0)),
                  pl.BlockSpec((1, DT_LANES), lambda b: (0, 0)),
                  pl.BlockSpec((1, D), lambda b: (0, 0)),
                  pl.BlockSpec((1, D), lambda b: (0, 0)),
                  pl.BlockSpec((DT_LANES, D), lambda b: (0, 0)),
                  pl.BlockSpec((DT_LANES, D), lambda b: (0, 0))],
        out_specs=pl.BlockSpec((1, S, D), lambda b: (b, 0, 0)),
        out_shape=jax.ShapeDtypeStruct((BATCH, S, D), bf16),
        scratch_shapes=[pltpu.VMEM((S, D), f32),
                        pltpu.VMEM((STATE, D), f32),
                        pltpu.VMEM((STATE, D), f32)],
        compiler_params=_cparams(("parallel",), 56),
        name="ssd_scan",
    )(zx, dt, alog, dskip, ng, e_f, e_b)


def _hp_dot(a, b):
    return jnp.dot(a, b, precision=lax.Precision.HIGHEST, preferred_element_type=f32)


def _hyfilter_kernel(z_ref, w1_ref, b1_ref, w2_ref, b2_ref, w3_ref, b3_ref, fr_ref, w4f_ref, w4b_ref,
                     win_ref, wc_ref, ws_ref, ka_ref, kb_ref, hf_scr, hb_scr, ks_scr, kd_scr, ny_scr):
    fi = pl.program_id(1)
    seg, tc = hf_scr.shape

    @pl.when(fi == 0)
    def _():
        fr = fr_ref[...]

        def taps(c, carry):
            ssq, nyq = carry
            r0 = pl.multiple_of(c * RC, RC)
            h = jnp.sin(fr[0:1] * (_hp_dot(z_ref[pl.ds(r0, RC), :], w1_ref[...]) + b1_ref[...]))
            h = jnp.sin(fr[1:2] * (_hp_dot(h, w2_ref[...]) + b2_ref[...]))
            h = jnp.sin(fr[2:3] * (_hp_dot(h, w3_ref[...]) + b3_ref[...]))
            win = win_ref[pl.ds(r0, RC), :]
            hf = _hp_dot(h, w4f_ref[...]) * win
            hb = _hp_dot(h, w4b_ref[...]) * win
            row = r0 + lax.broadcasted_iota(jnp.int32, hf.shape, 0)
            hb = jnp.where(row == 0, 0.0, hb)
            hf_scr[pl.ds(r0, RC), :] = hf
            hb_scr[pl.ds(r0, RC), :] = hb
            sign = jnp.where((row & 1) == 0, 1.0, -1.0)
            ssq = ssq + jnp.sum(hf * hf + hb * hb, axis=0, keepdims=True)
            nyq = nyq + jnp.sum((hf + hb) * sign, axis=0, keepdims=True)
            return ssq, nyq
        zero = jnp.zeros((1, tc), f32)
        ssq, nyq = lax.fori_loop(0, seg // RC, taps, (zero, zero))
        nrm = lax.rsqrt(ssq + 1e-6)
        ny_scr[...] = jnp.broadcast_to(nyq * nrm, ny_scr.shape)

        def scale(c, carry):
            r0 = pl.multiple_of(c * RC, RC)
            hf, hb = hf_scr[pl.ds(r0, RC), :], hb_scr[pl.ds(r0, RC), :]
            ks_scr[pl.ds(r0, RC), :] = ((hf + hb) * nrm).astype(bf16)
            kd_scr[pl.ds(r0, RC), :] = ((hf - hb) * nrm).astype(bf16)
            return carry
        lax.fori_loop(0, seg // RC, scale, 0)

    ka_ref[...] = _bdot(wc_ref[...], ks_scr[...])
    kb = _bdot(ws_ref[...], kd_scr[...])
    row = fi * kb.shape[0] + lax.broadcasted_iota(jnp.int32, kb.shape, 0)
    kb_ref[...] = jnp.where(row == 0, ny_scr[0:1], kb)


def _hyfilter_call(zf, w1, b1, w2, b2, w3, b3, fr, w4, win, wtab, seg, tc, tf):
    nct, nf = D // tc, seg // tf
    const = lambda shape: pl.BlockSpec(shape, lambda c, f: (0, 0))
    return pl.pallas_call(
        _hyfilter_kernel,
        grid=(nct, nf),
        in_specs=[const((seg, 128)), const((128, HY_FFN)), const((1, HY_FFN)),
                  const((HY_FFN, HY_FFN)), const((1, HY_FFN)),
                  const((HY_FFN, HY_FFN)), const((1, HY_FFN)), const((8, HY_FFN)),
                  pl.BlockSpec((HY_FFN, tc), lambda c, f: (0, c)),
                  pl.BlockSpec((HY_FFN, tc), lambda c, f: (0, c + nct)),
                  pl.BlockSpec((seg, tc), lambda c, f: (0, c)),
                  pl.BlockSpec((tf, seg), lambda c, f: (f, 0)),
                  pl.BlockSpec((tf, seg), lambda c, f: (f + nf, 0))],
        out_specs=[pl.BlockSpec((tf, tc), lambda c, f: (f, c)),
                   pl.BlockSpec((tf, tc), lambda c, f: (f, c))],
        out_shape=[jax.ShapeDtypeStruct((seg, D), f32), jax.ShapeDtypeStruct((seg, D), f32)],
        scratch_shapes=[pltpu.VMEM((seg, tc), f32), pltpu.VMEM((seg, tc), f32),
                        pltpu.VMEM((seg, tc), bf16), pltpu.VMEM((seg, tc), bf16),
                        pltpu.VMEM((8, tc), f32)],
        compiler_params=_cparams(("parallel", "arbitrary"), 48),
        name=f"hyena_filter_{seg}",
    )(zf, w1, b1, w2, b2, w3, b3, fr, w4, w4, win, wtab, wtab)


def _hyconv_kernel(u_ref, x0_ref, wc_ref, ws_ref, wtc_ref, wts_ref, ka_ref, kb_ref, bias_ref, *rest,
                   nf, inv_n):
    o_ref, acc = rest[-2], rest[-1]
    fi = pl.program_id(2)
    seg = acc.shape[0]

    @pl.when(fi == 0)
    def _():
        acc[...] = jnp.zeros_like(acc)

    u = u_ref[0]
    a = _bdot(wc_ref[...], u)
    b = _bdot(ws_ref[...], u)
    ka, kb = ka_ref[...], kb_ref[...]
    is0 = (fi * a.shape[0] + lax.broadcasted_iota(jnp.int32, a.shape, 0)) == 0
    scale = jnp.where(is0, inv_n, 2.0 * inv_n)
    ya = (jnp.where(is0, a * ka, a * ka - b * kb) * scale).astype(bf16)
    yb = (jnp.where(is0, b * kb, a * kb + b * ka) * scale).astype(bf16)

    def inv(r0):
        acc[pl.ds(r0, RC), :] += (_bdot(wtc_ref[pl.ds(r0, RC), :], ya)
                                  + _bdot(wts_ref[pl.ds(r0, RC), :], yb))
    _row_chunks(seg, inv)

    @pl.when(fi == nf - 1)
    def _():
        def fin(r0):
            y = acc[pl.ds(r0, RC), :] + bias_ref[...] * u_ref[0, pl.ds(r0, RC), :].astype(f32)
            o_ref[0, pl.ds(r0, RC), :] = (x0_ref[0, pl.ds(r0, RC), :].astype(f32) * y).astype(bf16)
        _row_chunks(seg, fin)


def _hyconv_call(u, x0, wtab, wtab_t, ka, kb, bias, seg, row_blk, tc, tf, prev=None):
    nct, nf = D // tc, seg // tf
    in_specs = [pl.BlockSpec((1, seg, tc), lambda b, c, f: (b, row_blk, c)),
                pl.BlockSpec((1, seg, tc), lambda b, c, f: (b, row_blk, c)),
                pl.BlockSpec((tf, seg), lambda b, c, f: (f, 0)),
                pl.BlockSpec((tf, seg), lambda b, c, f: (f + nf, 0)),
                pl.BlockSpec((seg, tf), lambda b, c, f: (0, f)),
                pl.BlockSpec((seg, tf), lambda b, c, f: (0, f + nf)),
                pl.BlockSpec((tf, tc), lambda b, c, f: (f, c)),
                pl.BlockSpec((tf, tc), lambda b, c, f: (f, c)),
                pl.BlockSpec((1, tc), lambda b, c, f: (0, c))]
    args = [u, x0, wtab, wtab, wtab_t, wtab_t, ka, kb, bias]
    aliases = {}
    if prev is not None:
        in_specs.append(pl.BlockSpec(memory_space=pl.ANY))
        args.append(prev)
        aliases = {len(args) - 1: 0}
    return pl.pallas_call(
        functools.partial(_hyconv_kernel, nf=nf, inv_n=1.0 / (2 * seg)),
        grid=(BATCH, nct, nf),
        in_specs=in_specs,
        out_specs=pl.BlockSpec((1, seg, tc), lambda b, c, f: (b, row_blk, c)),
        out_shape=jax.ShapeDtypeStruct((BATCH, S, D), bf16),
        scratch_shapes=[pltpu.VMEM((seg, tc), f32)],
        input_output_aliases=aliases,
        compiler_params=_cparams(("parallel", "parallel", "arbitrary"), 48),
        name=f"hyena_conv_{seg}",
    )(*args)


def _rope(x, cos_t, sin_t):
    n = x.shape[1]
    lane = lax.broadcasted_iota(jnp.int32, x.shape, 1)
    first = (lane & 16) == 0
    sw = jnp.where(first, pltpu.roll(x, n - 16, 1), pltpu.roll(x, 16, 1))
    return x * cos_t + sw * sin_t


assert WINDOW == ABLK
LOG2E = math.log2(math.e)
A_SPAN = ABLK + 2 * WINDOW
A_KEYS = A_SPAN + CTX
A_COLS = ATT_GROUP * ABLK


def _attn_kernel(sink_ref, q_ref, kv_ref, cos_ref, sin_ref, o_ref, q_scr, k_scr, vt_scr, s_scr, p_scr):
    g = pl.program_id(1)
    qscale = HEAD_P ** -0.5 * LOG2E

    def prep(r0, with_rope):
        q = q_ref[0, pl.ds(r0, RC), :].astype(f32) * qscale
        kv = kv_ref[0, pl.ds(r0, RC), :].astype(f32)
        k = kv
        if with_rope:
            cos_t, sin_t = cos_ref[pl.ds(r0, RC), :], sin_ref[pl.ds(r0, RC), :]
            q = _rope(q, cos_t, sin_t)
            k = _rope(kv, cos_t[:, 0:2 * HEAD_P], sin_t[:, 0:2 * HEAD_P])
        low = lax.broadcasted_iota(jnp.int32, kv.shape, 1) < HEAD_P
        k_scr[pl.ds(r0, RC), :] = jnp.where(low, k, 0.0).astype(bf16)
        for r in range(ATT_GROUP):
            slab = q[:, (r // 2) * 2 * HEAD_P:(r // 2 + 1) * 2 * HEAD_P]
            if r % 2:
                slab = pltpu.roll(slab, HEAD_P, 1)
            q_scr[r, pl.ds(r0, RC), :] = jnp.where(low, slab, 0.0).astype(bf16)
        v_t = kv.T[HEAD_P:2 * HEAD_P, :].astype(bf16)
        c0 = r0 // ABLK
        for h in range(RC // ABLK):
            vt_scr[c0 + h] = v_t[:, h * ABLK:(h + 1) * ABLK]
    _row_chunks(SEQ, lambda r0: prep(r0, True))
    for r0 in range(SEQ, S, RC):
        prep(r0, False)

    kmq = (lax.broadcasted_iota(jnp.int32, (ABLK, A_COLS), 0)
           - (lax.broadcasted_iota(jnp.int32, (ABLK, A_COLS), 1) & (ABLK - 1)))

    def block(q0, start, masks, slot):
        qstack = jnp.concatenate([q_scr[r, pl.ds(q0, ABLK), :] for r in range(ATT_GROUP)], axis=0)
        chunks = []
        if masks:
            s_loc = _bdot_nt(k_scr[pl.ds(start, len(masks) * ABLK), :], qstack)
            for c, kind in enumerate(masks):
                s_c = s_loc[c * ABLK:(c + 1) * ABLK]
                if kind is not None:
                    s_c = jnp.where(kmq >= 0 if kind == 'ge' else kmq <= 0, s_c, -jnp.inf)
                s_scr[slot, c * ABLK:(c + 1) * ABLK, :] = s_c
                chunks.append((c * ABLK, start // ABLK + c))
        n_loc = len(chunks) * ABLK
        s_scr[slot, n_loc:n_loc + CTX, :] = _bdot_nt(k_scr[SEQ:S, :], qstack)
        chunks += [(n_loc + c * ABLK, SEQ // ABLK + c) for c in range(CTX // ABLK)]

        inv = []
        for r in range(ATT_GROUP):
            cols = slice(r * ABLK, (r + 1) * ABLK)
            sink = sink_ref[g * ATT_GROUP + r] * LOG2E
            m = jnp.full((1, ABLK), sink, f32)
            for row, _ in chunks:
                m = jnp.maximum(m, jnp.max(s_scr[slot, row:row + ABLK, cols], axis=0, keepdims=True))
            den = jnp.exp2(sink - m)
            for row, _ in chunks:
                p = jnp.exp2(s_scr[slot, row:row + ABLK, cols] - m)
                den = den + jnp.sum(p, axis=0, keepdims=True)
                p_scr[slot, row:row + ABLK, cols] = p.astype(bf16)
            inv.append(1.0 / den)

        o_t = None
        for row, vc in chunks:
            part = _bdot(vt_scr[vc], p_scr[slot, row:row + ABLK, :])
            o_t = part if o_t is None else o_t + part
        o_t = jnp.concatenate([o_t[:, r * ABLK:(r + 1) * ABLK] * inv[r] for r in range(ATT_GROUP)], axis=0)
        o_ref[0, pl.ds(q0, ABLK), :] = o_t.T.astype(bf16)

    n_blk = SEQ // ABLK
    block(0, 0, (None, 'le'), 0)
    block(SEQ - ABLK, SEQ - 2 * ABLK, ('ge', None), 1)

    def lat_pair(j, carry):
        for slot in range(2):
            q0 = pl.multiple_of((1 + 2 * j + slot) * ABLK, ABLK)
            block(q0, pl.multiple_of(q0 - WINDOW, ABLK), ('ge', None, 'le'), slot)
        return carry
    lax.fori_loop(0, (n_blk - 2) // 2, lat_pair, 0)

    for slot, q0 in enumerate(range(SEQ, S, ABLK)):
        block(q0, 0, (), slot % 2)


def _attn_call(qkv, sinks, cos_t, sin_t):
    gw = ATT_GROUP * HEAD_P
    return pl.pallas_call(
        _attn_kernel,
        grid_spec=pltpu.PrefetchScalarGridSpec(
            num_scalar_prefetch=1,
            grid=(BATCH, KV_HEADS),
            in_specs=[pl.BlockSpec((1, S, gw), lambda b, g, s: (b, 0, g)),
                      pl.BlockSpec((1, S, 2 * HEAD_P), lambda b, g, s: (b, 0, QW // (2 * HEAD_P) + g)),
                      pl.BlockSpec((SEQ, gw), lambda b, g, s: (0, 0)),
                      pl.BlockSpec((SEQ, gw), lambda b, g, s: (0, 0))],
            out_specs=pl.BlockSpec((1, S, gw), lambda b, g, s: (b, 0, g)),
            scratch_shapes=[pltpu.VMEM((ATT_GROUP, S, 2 * HEAD_P), bf16),
                            pltpu.VMEM((S, 2 * HEAD_P), bf16),
                            pltpu.VMEM((S // ABLK, HEAD_P, ABLK), bf16),
                            pltpu.VMEM((2, A_KEYS, A_COLS), f32),
                            pltpu.VMEM((2, A_KEYS, A_COLS), bf16)]),
        out_shape=jax.ShapeDtypeStruct((BATCH, S, QW), bf16),
        compiler_params=_cparams(("parallel", "parallel"), 40),
        name="window_attn",
    )(sinks, qkv, qkv, cos_t, sin_t)


RES_TM = 768


def _outproj_kernel(x_ref, mod_ref, *rest, n_in):
    a_refs, w_refs, o_ref = rest[:n_in], rest[n_in:2 * n_in], rest[2 * n_in]
    for r0 in range(0, RES_TM, RC):
        acc = _bdot(a_refs[0][0, r0:r0 + RC, :], w_refs[0][...])
        for a_ref, w_ref in zip(a_refs[1:], w_refs[1:]):
            acc = acc + _bdot(a_ref[0, r0:r0 + RC, :], w_ref[...])
        gate = _mod_row(mod_ref[0], 2, pl.program_id(1) * RES_TM + r0)
        o_ref[0, r0:r0 + RC, :] = x_ref[0, r0:r0 + RC, :] + gate * acc


def _outproj_call(xs, mods_l, acts, ws, name):
    n_in = len(acts)
    in_specs = [pl.BlockSpec((1, RES_TM, D), lambda b, i: (b, i, 0)),
                pl.BlockSpec((1, MOD_ROWS, D), lambda b, i: (b, 0, 0))]
    in_specs += [pl.BlockSpec((1, RES_TM, a.shape[2]), lambda b, i: (b, i, 0)) for a in acts]
    in_specs += [pl.BlockSpec(w.shape, lambda b, i: (0, 0)) for w in ws]
    return pl.pallas_call(
        functools.partial(_outproj_kernel, n_in=n_in),
        grid=(BATCH, S // RES_TM),
        in_specs=in_specs,
        out_specs=pl.BlockSpec((1, RES_TM, D), lambda b, i: (b, i, 0)),
        out_shape=jax.ShapeDtypeStruct((BATCH, S, D), f32),
        input_output_aliases={0: 0},
        compiler_params=_cparams(("parallel", "parallel"), 40),
        name=name,
    )(xs, mods_l, *acts, *ws)


FFN_TM = 768
FFN_TH = 256
FFN_NH = FFN_H // FFN_TH


def _ffn_kernel(x_ref, mod_ref, g_ref, wg_ref, wu_ref, wo_ref, o_ref, h_scr, acc):
    row0 = pl.program_id(1) * FFN_TM
    for r0 in range(0, FFN_TM, RC):
        h_scr[r0:r0 + RC, :] = _norm_mod(x_ref[0, r0:r0 + RC, :], g_ref[...], mod_ref[0],
                                         3, 4, row0 + r0).astype(bf16)
    for k in range(FFN_NH):
        act = (_silu(_bdot(h_scr[...], wg_ref[k])) * _bdot(h_scr[...], wu_ref[k])).astype(bf16)
        down = _bdot(act, wo_ref[k])
        if k == 0:
            acc[...] = down
        else:
            acc[...] += down
    for r0 in range(0, FFN_TM, RC):
        gate = _mod_row(mod_ref[0], 5, row0 + r0)
        o_ref[0, r0:r0 + RC, :] = x_ref[0, r0:r0 + RC, :] + gate * acc[r0:r0 + RC, :]


def _ffn_call(xs, mods_l, g, wg, wu, wo):
    resident = lambda shape: pl.BlockSpec(shape, lambda b, i: (0, 0, 0), pipeline_mode=pl.Buffered(1))
    return pl.pallas_call(
        _ffn_kernel,
        grid=(BATCH, S // FFN_TM),
        in_specs=[pl.BlockSpec((1, FFN_TM, D), lambda b, i: (b, i, 0)),
                  pl.BlockSpec((1, MOD_ROWS, D), lambda b, i: (b, 0, 0)),
                  pl.BlockSpec((1, D), lambda b, i: (0, 0)),
                  resident((FFN_NH, D, FFN_TH)), resident((FFN_NH, D, FFN_TH)),
                  resident((FFN_NH, FFN_TH, D))],
        out_specs=pl.BlockSpec((1, FFN_TM, D), lambda b, i: (b, i, 0)),
        out_shape=jax.ShapeDtypeStruct((BATCH, S, D), f32),
        scratch_shapes=[pltpu.VMEM((FFN_TM, D), bf16), pltpu.VMEM((FFN_TM, D), f32)],
        input_output_aliases={0: 0},
        compiler_params=_cparams(("parallel", "parallel"), 52),
        name="swiglu_ffn",
    )(xs, mods_l, g, wg, wu, wo)


FIN_TM = 256


def _final_kernel(x_ref, g_ref, o_ref):
    x = x_ref[0]
    ms = jnp.mean(x * x, axis=-1, keepdims=True)
    o_ref[0] = x * lax.rsqrt(ms + EPS) * g_ref[...]


def _final_call(xs, g):
    return pl.pallas_call(
        _final_kernel,
        grid=(BATCH, SEQ // FIN_TM),
        in_specs=[pl.BlockSpec((1, FIN_TM, D), lambda b, i: (b, i, 0)),
                  pl.BlockSpec((1, D), lambda b, i: (0, 0))],
        out_specs=pl.BlockSpec((1, FIN_TM, D), lambda b, i: (b, i, 0)),
        out_shape=jax.ShapeDtypeStruct((BATCH, SEQ, D), f32),
        compiler_params=_cparams(("parallel", "parallel"), 16),
        name="final_norm",
    )(xs, g)


def _dft_table(seg):
    n = 2 * seg
    f = jnp.arange(seg, dtype=jnp.int32)[:, None]
    t = jnp.arange(seg, dtype=jnp.int32)[None, :]
    ang = (2.0 * math.pi / n) * ((f * t) % n).astype(f32)
    nyq = jnp.where((t & 1) == 0, 1.0, -1.0).astype(f32)
    sin_rows = jnp.where(f == 0, nyq, jnp.sin(ang))
    return jnp.concatenate([jnp.cos(ang), sin_rows], axis=0)


def _hy_features(seg):
    t = jnp.arange(seg, dtype=f32)
    t_unit = t / (seg - 1)
    ang = 2.0 * math.pi * t / seg
    fb = jnp.linspace(1e-4, HY_BANDS - 1, HY_BANDS, dtype=f32)
    z = jnp.concatenate([t_unit[:, None], jnp.cos(ang[:, None] * fb), -jnp.sin(ang[:, None] * fb)], axis=-1)
    z = jnp.pad(z, ((0, 0), (0, 128 - HY_EMB)))
    deltas = jnp.linspace(math.log(1e-2) / 1.5, math.log(1e-2) / 0.3, D, dtype=f32)
    win = jnp.exp(-t_unit[:, None] * jnp.abs(deltas)[None, :])
    return z, win


def _rope_tables():
    nf = HEAD_P // 4
    inv = ROPE_BASE ** (-jnp.arange(nf, dtype=f32) / nf)
    pos = jnp.arange(SEQ)
    row = (pos // GRID_W).astype(f32)
    col = (pos % GRID_W).astype(f32)
    ar = row[:, None] * inv[None, :]
    ac = col[:, None] * inv[None, :]
    ang = jnp.concatenate([ar, ar, ac, ac], axis=-1)
    sign = jnp.tile(jnp.concatenate([-jnp.ones(nf, f32), jnp.ones(nf, f32)]), 2)
    cos_t = jnp.tile(jnp.cos(ang), (1, ATT_GROUP))
    sin_t = jnp.tile(jnp.sin(ang) * sign[None, :], (1, ATT_GROUP))
    return cos_t, sin_t


def _head_expand(lane_off):
    r = jnp.arange(DT_LANES)[:, None]
    c = jnp.arange(D)[None, :]
    return (c // HEAD_P + lane_off == r).astype(bf16)


def _pad_rows(a, rows):
    return jnp.pad(a, ((0, rows - a.shape[0]), (0, 0)))


def _pad_lanes(a, lanes):
    return jnp.pad(a, ((0, 0), (0, lanes - a.shape[1])))


def kernel(x, c, ctx, c_ctx, ada_w, ada_b, norm1_g, norm2_g, hy_in_w, ssd_conv_w, ssd_conv_b, ssd_dt_bias, ssd_a_log, ssd_d, ssd_norm_g, hy_conv_w, hy_conv_b, hy_w1, hy_b1, hy_w2, hy_b2, hy_w3, hy_b3, hy_w4, hy_freq, hy_bias, hy_out_w, attn_qkv_w, attn_sinks, attn_out_w, ffn_w_in, ffn_w_out, final_g):
    xs = jnp.concatenate([x, ctx], axis=1)

    cc = jnp.concatenate([c, c_ctx[None, :], jnp.zeros((MOD_ROWS - BATCH - 1, D), f32)], axis=0)
    modv = _mod_call(cc, ada_w, ada_b)
    lat = modv[:, :BATCH].reshape(DEPTH, BATCH, N_MOD, D)
    cmod = jnp.broadcast_to(modv[:, BATCH].reshape(DEPTH, 1, N_MOD, D), (DEPTH, BATCH, N_MOD, D))
    pad = jnp.zeros((DEPTH, BATCH, CTX_MOD - N_MOD, D), f32)
    mods = jnp.concatenate([lat, pad, cmod, pad], axis=2)

    tabs = {}
    for seg in (SEQ, CTX):
        w = _dft_table(seg)
        tabs[seg] = (w.astype(bf16), w.T.astype(bf16)) + _hy_features(seg)
    e_f, e_b = _head_expand(0), _head_expand(HEADS)
    cos_t, sin_t = _rope_tables()

    for layer in range(DEPTH):
        j = layer // 2
        mods_l = mods[layer]
        g1 = norm1_g[layer][None, :]
        if layer % 2 == 0:
            w_in = hy_in_w[j]
            w_zx = w_in[:, :COL_DT].astype(bf16)
            wdt = _pad_lanes(w_in[:, COL_DT:COL_HY], DT_LANES).astype(bf16)
            dtb = _pad_lanes(ssd_dt_bias[j].reshape(1, 2 * HEADS), DT_LANES)
            cw = jnp.concatenate([jnp.zeros((4, D), f32),
                                  jnp.concatenate([ssd_conv_w[j], ssd_conv_b[j][None, :]], axis=0)], axis=1)
            zx, dt = _inproj_ssd_call(xs, mods_l, g1, w_zx, _pad_rows(cw, 8), wdt, dtb)
            cw_hy = _pad_rows(jnp.concatenate([hy_conv_w[j], hy_conv_b[j][None, :]], axis=0), 8)
            x0, u = _inproj_hy_call(xs, mods_l, g1, w_in[:, COL_HY:].astype(bf16), cw_hy)

            alog = _pad_lanes(ssd_a_log[j].reshape(1, 2 * HEADS), DT_LANES)
            dskip = jnp.repeat(ssd_d[j], HEAD_P)[None, :]
            yn = _ssd_call(zx, dt, alog, dskip, ssd_norm_g[j][None, :], e_f, e_b)

            gh = None
            for seg, row_blk, tc, tf in ((SEQ, 0, 512, 512), (CTX, SEQ // CTX, 512, CTX)):
                wtab, wtab_t, zf, win = tabs[seg]
                ka, kb = _hyfilter_call(zf, _pad_rows(hy_w1[j], 128), hy_b1[j][None, :], hy_w2[j],
                                        hy_b2[j][None, :], hy_w3[j], hy_b3[j][None, :],
                                        _pad_rows(hy_freq[j], 8), hy_w4[j], win, wtab, seg, tc, tf)
                gh = _hyconv_call(u, x0, wtab, wtab_t, ka, kb, hy_bias[j][None, :], seg, row_blk, tc, tf,
                                  prev=gh)
            w_out = hy_out_w[j].astype(bf16)
            xs = _outproj_call(xs, mods_l, [yn, gh], [w_out[:D], w_out[D:]], "outproj_even")
        else:
            wq = attn_qkv_w[j]
            wk = wq[:, QW:QW + KVW].reshape(D, KV_HEADS, HEAD_P)
            wv = wq[:, QW + KVW:].reshape(D, KV_HEADS, HEAD_P)
            w_kv = jnp.concatenate([wk, wv], axis=2).reshape(D, 2 * KVW)
            w_qkv = jnp.concatenate([wq[:, :QW], w_kv], axis=1).astype(bf16)
            qkv = _nmm_call(xs, mods_l, g1, w_qkv, 512, "qkv_proj")
            o = _attn_call(qkv, attn_sinks[j], cos_t, sin_t)
            xs = _outproj_call(xs, mods_l, [o], [attn_out_w[j].astype(bf16)], "outproj_attn")
        w_gu = ffn_w_in[layer].astype(bf16).reshape(D, 2, FFN_NH, FFN_TH).transpose(1, 2, 0, 3)
        xs = _ffn_call(xs, mods_l, norm2_g[layer][None, :], w_gu[0], w_gu[1],
                       ffn_w_out[layer].astype(bf16).reshape(FFN_NH, FFN_TH, D))
    return _final_call(xs, final_g[None, :])
```

```python
import functools
import math

import jax
import jax.numpy as jnp
from jax import lax
from jax.experimental import pallas as pl
from jax.experimental.pallas import tpu as pltpu

f32 = jnp.float32
bf16 = jnp.bfloat16

D = 1024
BATCH = 8
SEQ = 2048
DEPTH = 4
GRID_W = 64
CTX = 256
S = SEQ + CTX
EPS = 1e-6
N_MOD = 6
MOD_ROWS = 16
CTX_MOD = 8

HEADS = 16
HEAD_P = 64
STATE = 128
GROUPS = 2
CHUNK = 128
XBC = D + 2 * GROUPS * STATE
ZX = D + XBC
DT_LANES = 128
HY_BANDS = 16
HY_EMB = 1 + 2 * HY_BANDS
HY_FFN = 64
COL_XBC = D
COL_DT = COL_XBC + XBC
COL_HY = COL_DT + 2 * HEADS

Q_HEADS = 16
KV_HEADS = 4
ATT_GROUP = Q_HEADS // KV_HEADS
QW = Q_HEADS * HEAD_P
KVW = KV_HEADS * HEAD_P
WINDOW = 128
ABLK = 128
ROPE_BASE = 10000.0
FFN_H = -(-8 * D // (3 * 256)) * 256

RC = 256
VMEM_CAP = 56 * 1024 * 1024


def _cparams(sem, vmem_mb):
    return pltpu.CompilerParams(dimension_semantics=sem,
                                vmem_limit_bytes=min(int(vmem_mb * 1024 * 1024), VMEM_CAP))


def _silu(v):
    return v * jax.nn.sigmoid(v)


def _softplus(v):
    return jnp.maximum(v, 0.0) + jnp.log1p(jnp.exp(-jnp.abs(v)))


def _bdot(a, b):
    return jnp.dot(a, b, preferred_element_type=f32)


def _bdot_nt(a, b):
    return lax.dot_general(a, b, (((1,), (1,)), ((), ())), preferred_element_type=f32)


def _row_chunks(n_rows, body):
    def step(c, carry):
        body(pl.multiple_of(c * RC, RC))
        return carry
    lax.fori_loop(0, n_rows // RC, step, 0)


def _norm_mod(x, g, mod, shift_row, scale_row, row0):
    ms = jnp.mean(x * x, axis=-1, keepdims=True)
    scale = _mod_row(mod, scale_row, row0)
    return x * lax.rsqrt(ms + EPS) * (g * (1.0 + scale)) + _mod_row(mod, shift_row, row0)


def _mod_row(mod, k, row0):
    return jnp.where(row0 >= SEQ, mod[CTX_MOD + k:CTX_MOD + k + 1], mod[k:k + 1])


def _norm_mod_to_scratch(x_ref, g_ref, mod_ref, h_scr):
    def body(r0):
        h_scr[pl.ds(r0, RC), :] = _norm_mod(x_ref[0, pl.ds(r0, RC), :], g_ref[...], mod_ref[0],
                                            0, 1, r0).astype(bf16)
    _row_chunks(S, body)


CONV_ROWS = 64


def _conv3_rows(p_scr, r0, cw):
    n = CONV_ROWS
    v = p_scr[pl.ds(r0, n), :]
    if isinstance(r0, int):
        up_row, dn_row = max(r0 - 8, 0), min(r0 + n, S - 8)
        seg_start, seg_end = r0 in (0, SEQ), r0 + n in (SEQ, S)
    else:
        up_row = pl.multiple_of(jnp.maximum(r0 - 8, 0), 8)
        dn_row = pl.multiple_of(jnp.minimum(r0 + n, S - 8), 8)
        seg_start = jnp.logical_or(r0 == 0, r0 == SEQ)
        seg_end = jnp.logical_or(r0 + n == SEQ, r0 + n == S)
    up = jnp.where(seg_start, 0.0, p_scr[pl.ds(up_row, 8), :][7:8])
    dn = jnp.where(seg_end, 0.0, p_scr[pl.ds(dn_row, 8), :][0:1])
    sub = lax.broadcasted_iota(jnp.int32, (8, v.shape[1]), 0)
    prev = pltpu.roll(v, 1, 0)
    prev = jnp.concatenate([jnp.where(sub == 0, up, prev[0:8]), prev[8:]], axis=0)
    nxt = pltpu.roll(v, n - 1, 0)
    nxt = jnp.concatenate([nxt[:n - 8], jnp.where(sub == 7, dn, nxt[n - 8:])], axis=0)
    return prev * cw[0:1] + v * cw[1:2] + nxt * cw[2:3] + cw[3:4]


def _conv_pieces(r0, piece):
    for k in range(0, RC, CONV_ROWS):
        piece(r0 + k if isinstance(r0, int) else pl.multiple_of(r0 + k, CONV_ROWS))


def _project_then_conv(mm, conv):
    mm(0)
    mm(RC)

    def step(c, carry):
        r0 = pl.multiple_of(c * RC, RC)
        conv(pl.multiple_of(r0 - 2 * RC, RC))
        mm(r0)
        return carry
    lax.fori_loop(2, S // RC, step, 0)
    conv(S - 2 * RC)
    conv(S - RC)


def _mod_kernel(c_ref, w_ref, b_ref, o_ref):
    sc = _silu(c_ref[...]).astype(bf16)
    o_ref[0] = _bdot(sc, w_ref[0].astype(bf16)) + b_ref[0]


def _mod_call(cc, ada_w, ada_b):
    tn = 1024
    return pl.pallas_call(
        _mod_kernel,
        grid=(DEPTH, N_MOD * D // tn),
        in_specs=[pl.BlockSpec((MOD_ROWS, D), lambda l, j: (0, 0)),
                  pl.BlockSpec((1, D, tn), lambda l, j: (l, 0, j)),
                  pl.BlockSpec((1, 1, tn), lambda l, j: (l, 0, j))],
        out_specs=pl.BlockSpec((1, MOD_ROWS, tn), lambda l, j: (l, 0, j)),
        out_shape=jax.ShapeDtypeStruct((DEPTH, MOD_ROWS, N_MOD * D), f32),
        compiler_params=_cparams(("parallel", "parallel"), 24),
        name="adaln_mod",
    )(cc, ada_w, ada_b.reshape(DEPTH, 1, N_MOD * D))


def _nmm_kernel(x_ref, mod_ref, g_ref, w_ref, o_ref, h_scr):
    @pl.when(pl.program_id(1) == 0)
    def _():
        _norm_mod_to_scratch(x_ref, g_ref, mod_ref, h_scr)

    def body(r0):
        o_ref[0, pl.ds(r0, RC), :] = _bdot(h_scr[pl.ds(r0, RC), :], w_ref[...]).astype(o_ref.dtype)
    _row_chunks(S, body)


def _nmm_call(xs, mods_l, g, w, tn, name):
    n = w.shape[1]
    return pl.pallas_call(
        _nmm_kernel,
        grid=(BATCH, n // tn),
        in_specs=[pl.BlockSpec((1, S, D), lambda b, j: (b, 0, 0)),
                  pl.BlockSpec((1, MOD_ROWS, D), lambda b, j: (b, 0, 0)),
                  pl.BlockSpec((1, D), lambda b, j: (0, 0)),
                  pl.BlockSpec((D, tn), lambda b, j: (0, j))],
        out_specs=pl.BlockSpec((1, S, tn), lambda b, j: (b, 0, j)),
        out_shape=jax.ShapeDtypeStruct((BATCH, S, n), bf16),
        scratch_shapes=[pltpu.VMEM((S, D), bf16)],
        compiler_params=_cparams(("parallel", "arbitrary"), 40),
        name=name,
    )(xs, mods_l, g, w)


ZX_TN = 512
NZ_T = D // ZX_TN


def _inproj_ssd_kernel(x_ref, mod_ref, g_ref, w_ref, cw_ref, wdt_ref, dtb_ref, zx_ref, dt_ref,
                       h_scr, p_scr):
    j = pl.program_id(1)

    @pl.when(j == 0)
    def _():
        _norm_mod_to_scratch(x_ref, g_ref, mod_ref, h_scr)

        def dt_body(r0):
            sp = _softplus(_bdot(h_scr[pl.ds(r0, RC), :], wdt_ref[...]) + dtb_ref[...])
            lane = lax.broadcasted_iota(jnp.int32, sp.shape, 1)
            dt_ref[0, pl.ds(r0, RC), :] = jnp.where(lane < 2 * HEADS, sp, 0.0)
        _row_chunks(S, dt_body)

    @pl.when(j < NZ_T)
    def _():
        def body(r0):
            zx_ref[0, pl.ds(r0, RC), :] = _bdot(h_scr[pl.ds(r0, RC), :], w_ref[...]).astype(bf16)
        _row_chunks(S, body)

    @pl.when(j >= NZ_T)
    def _():
        def mm(r0):
            p_scr[pl.ds(r0, RC), :] = _bdot(h_scr[pl.ds(r0, RC), :], w_ref[...])

        def conv(r0):
            def piece(a):
                zx_ref[0, pl.ds(a, CONV_ROWS), :] = _silu(_conv3_rows(p_scr, a, cw_ref[...])).astype(bf16)
            _conv_pieces(r0, piece)
        _project_then_conv(mm, conv)


def _inproj_ssd_call(xs, mods_l, g, w_zx, cw, wdt, dtb):
    return pl.pallas_call(
        _inproj_ssd_kernel,
        grid=(BATCH, ZX // ZX_TN),
        in_specs=[pl.BlockSpec((1, S, D), lambda b, j: (b, 0, 0)),
                  pl.BlockSpec((1, MOD_ROWS, D), lambda b, j: (b, 0, 0)),
                  pl.BlockSpec((1, D), lambda b, j: (0, 0)),
                  pl.BlockSpec((D, ZX_TN), lambda b, j: (0, j)),
                  pl.BlockSpec((8, ZX_TN), lambda b, j: (0, j)),
                  pl.BlockSpec((D, DT_LANES), lambda b, j: (0, 0)),
                  pl.BlockSpec((1, DT_LANES), lambda b, j: (0, 0))],
        out_specs=[pl.BlockSpec((1, S, ZX_TN), lambda b, j: (b, 0, j)),
                   pl.BlockSpec((1, S, DT_LANES), lambda b, j: (b, 0, 0))],
        out_shape=[jax.ShapeDtypeStruct((BATCH, S, ZX), bf16),
                   jax.ShapeDtypeStruct((BATCH, S, DT_LANES), f32)],
        scratch_shapes=[pltpu.VMEM((S, D), bf16), pltpu.VMEM((S, ZX_TN), f32)],
        compiler_params=_cparams(("parallel", "arbitrary"), 48),
        name="inproj_ssd",
    )(xs, mods_l, g, w_zx, cw, wdt, dtb)


HY_TN = 256


def _inproj_hy_kernel(x_ref, mod_ref, g_ref, w0_ref, w1_ref, w2_ref, c0_ref, c1_ref, c2_ref,
                      x0_ref, u_ref, h_scr, p0_scr, p1_scr, p2_scr):
    @pl.when(pl.program_id(1) == 0)
    def _():
        _norm_mod_to_scratch(x_ref, g_ref, mod_ref, h_scr)

    def mm_x0(r0):
        p0_scr[pl.ds(r0, RC), :] = _bdot(h_scr[pl.ds(r0, RC), :], w0_ref[...])

    def conv_x0(r0):
        def piece(a):
            x0_ref[0, pl.ds(a, CONV_ROWS), :] = _conv3_rows(p0_scr, a, c0_ref[...]).astype(bf16)
        _conv_pieces(r0, piece)
    _project_then_conv(mm_x0, conv_x0)

    def mm_u(r0):
        h = h_scr[pl.ds(r0, RC), :]
        p1_scr[pl.ds(r0, RC), :] = _bdot(h, w1_ref[...])
        p2_scr[pl.ds(r0, RC), :] = _bdot(h, w2_ref[...])

    def conv_u(r0):
        def piece(a):
            x1 = _conv3_rows(p1_scr, a, c1_ref[...])
            v = _conv3_rows(p2_scr, a, c2_ref[...])
            u_ref[0, pl.ds(a, CONV_ROWS), :] = (x1 * v).astype(bf16)
        _conv_pieces(r0, piece)
    _project_then_conv(mm_u, conv_u)


def _inproj_hy_call(xs, mods_l, g, w_hy, cw_hy):
    nt = D // HY_TN
    wspec = lambda k: pl.BlockSpec((D, HY_TN), lambda b, j, k=k: (0, j + k * nt))
    cspec = lambda k: pl.BlockSpec((8, HY_TN), lambda b, j, k=k: (0, j + k * nt))
    return pl.pallas_call(
        _inproj_hy_kernel,
        grid=(BATCH, nt),
        in_specs=[pl.BlockSpec((1, S, D), lambda b, j: (b, 0, 0)),
                  pl.BlockSpec((1, MOD_ROWS, D), lambda b, j: (b, 0, 0)),
                  pl.BlockSpec((1, D), lambda b, j: (0, 0)),
                  wspec(0), wspec(1), wspec(2), cspec(0), cspec(1), cspec(2)],
        out_specs=[pl.BlockSpec((1, S, HY_TN), lambda b, j: (b, 0, j)),
                   pl.BlockSpec((1, S, HY_TN), lambda b, j: (b, 0, j))],
        out_shape=[jax.ShapeDtypeStruct((BATCH, S, D), bf16),
                   jax.ShapeDtypeStruct((BATCH, S, D), bf16)],
        scratch_shapes=[pltpu.VMEM((S, D), bf16)] + [pltpu.VMEM((S, HY_TN), f32)] * 3,
        compiler_params=_cparams(("parallel", "arbitrary"), 48),
        name="inproj_hyena",
    )(xs, mods_l, g, w_hy, w_hy, w_hy, cw_hy, cw_hy, cw_hy)


N_CHUNK = S // CHUNK
LAT_CHUNKS = SEQ // CHUNK


def _split3_dot(t, a):
    a1 = a.astype(bf16)
    r1 = a - a1.astype(f32)
    a2 = r1.astype(bf16)
    a3 = (r1 - a2.astype(f32)).astype(bf16)
    return _bdot(t, a1) + _bdot(t, a2) + _bdot(t, a3)


def _ssd_chunk(zx_ref, dt_ref, aneg, e_ref, y_scr, st_scr, r0, backward):
    lane_off = HEADS if backward else 0
    x = zx_ref[0, pl.ds(r0, CHUNK), D:2 * D].astype(f32)
    bm = zx_ref[0, pl.ds(r0, CHUNK), 2 * D:2 * D + GROUPS * STATE]
    cm = zx_ref[0, pl.ds(r0, CHUNK), 2 * D + GROUPS * STATE:ZX]
    dt = dt_ref[0, pl.ds(r0, CHUNK), :]
    a = dt * aneg

    ri = lax.broadcasted_iota(jnp.int32, (CHUNK, CHUNK), 0)
    ci = lax.broadcasted_iota(jnp.int32, (CHUNK, CHUNK), 1)
    causal = (ci >= ri) if backward else (ci <= ri)
    tmat = jnp.where(causal, 1.0, 0.0).astype(bf16)
    acum = _split3_dot(tmat, a)
    acum_t = acum.T
    tot = acum[0:1] if backward else acum[CHUNK - 1:CHUNK]
    dte = jnp.exp(tot - acum)
    eac = jnp.exp(acum)
    cdec = jnp.broadcast_to(jnp.exp(tot), (16, DT_LANES))

    e = e_ref[...]
    dt_x = _bdot(dt.astype(bf16), e)
    dte_x = _bdot(dte.astype(bf16), e)
    eac_x = _bdot(eac.astype(bf16), e)
    cdec_x = _bdot(cdec.astype(bf16), e)[0:1]

    xdt = x * dt_x
    xdt_b = xdt.astype(bf16)
    xs_b = (xdt * dte_x).astype(bf16)
    lane = lax.broadcasted_iota(jnp.int32, (CHUNK, 2 * HEAD_P), 1)
    low = lane < HEAD_P
    hg = HEADS // GROUPS
    gw = hg * HEAD_P
    for g in range(GROUPS):
        bg = bm[:, g * STATE:(g + 1) * STATE]
        cg = cm[:, g * STATE:(g + 1) * STATE]
        cb = _bdot_nt(cg, bg)
        hin = st_scr[:, g * gw:(g + 1) * gw]
        yoff = _bdot(cg, hin.astype(bf16)) * eac_x[:, g * gw:(g + 1) * gw]
        for hp in range(hg // 2):
            h0 = g * hg + 2 * hp
            ms = []
            for h in (h0, h0 + 1):
                l0 = lane_off + h
                seg = acum[:, l0:l0 + 1] - acum_t[l0:l0 + 1, :]
                ms.append(cb * jnp.exp(jnp.where(causal, seg, -jnp.inf)))
            lhs = jnp.concatenate(ms, axis=1).astype(bf16)
            xp = xdt[:, h0 * HEAD_P:(h0 + 2) * HEAD_P]
            rhs = jnp.concatenate([jnp.where(low, xp, 0.0), jnp.where(low, 0.0, xp)],
                                  axis=0).astype(bf16)
            yd = _bdot(lhs, rhs) + yoff[:, hp * 2 * HEAD_P:(hp + 1) * 2 * HEAD_P]
            cols = slice(h0 * HEAD_P, (h0 + 2) * HEAD_P)
            y_scr[pl.ds(r0, CHUNK), cols] = y_scr[pl.ds(r0, CHUNK), cols] + yd
        bg_t = bg.astype(f32).T.astype(bf16)
        st_scr[:, g * gw:(g + 1) * gw] = (hin * cdec_x[:, g * gw:(g + 1) * gw]
                                         + _bdot(bg_t, xs_b[:, g * gw:(g + 1) * gw]))
    del xdt_b


def _ssd_kernel(zx_ref, dt_ref, alog_ref, dskip_ref, ng_ref, ef_ref, eb_ref, o_ref,
                y_scr, sf_scr, sb_scr):
    aneg = -jnp.exp(alog_ref[...])
    sf_scr[...] = jnp.zeros_like(sf_scr)
    sb_scr[...] = jnp.zeros_like(sb_scr)

    def init(c, carry):
        r0 = pl.multiple_of(c * CHUNK, CHUNK)
        y_scr[pl.ds(r0, CHUNK), :] = dskip_ref[...] * zx_ref[0, pl.ds(r0, CHUNK), D:2 * D].astype(f32)
        return carry
    lax.fori_loop(0, N_CHUNK, init, 0)

    def step(t, carry):
        cf = jnp.where(t < N_CHUNK - LAT_CHUNKS, LAT_CHUNKS + t, t - (N_CHUNK - LAT_CHUNKS))
        cbk = N_CHUNK - 1 - t
        _ssd_chunk(zx_ref, dt_ref, aneg, ef_ref, y_scr, sf_scr, pl.multiple_of(cf * CHUNK, CHUNK), False)
        _ssd_chunk(zx_ref, dt_ref, aneg, eb_ref, y_scr, sb_scr, pl.multiple_of(cbk * CHUNK, CHUNK), True)
        return carry
    lax.fori_loop(0, N_CHUNK, step, 0)

    def fin(c, carry):
        r0 = pl.multiple_of(c * CHUNK, CHUNK)
        t = y_scr[pl.ds(r0, CHUNK), :] * _silu(zx_ref[0, pl.ds(r0, CHUNK), 0:D].astype(f32))
        ms = jnp.mean(t * t, axis=-1, keepdims=True)
        o_ref[0, pl.ds(r0, CHUNK), :] = (t * lax.rsqrt(ms + EPS) * ng_ref[...]).astype(bf16)
        return carry
    lax.fori_loop(0, N_CHUNK, fin, 0)


def _ssd_call(zx, dt, alog, dskip, ng, e_f, e_b):
    return pl.pallas_call(
        _ssd_kernel,
        grid=(BATCH,),
        in_specs=[pl.BlockSpec((1, S, ZX), lambda b: (b, 0, 0)),
                  pl.BlockSpec((1, S, DT_LANES), lambda b: (b, 0, 0)),
                  pl.BlockSpec((1, DT_LANES), lambda b: (0, 0)),
                  pl.BlockSpec((1, D), lambda b: (0, 0)),
                  pl.BlockSpec((1, D), lambda b: (0, 0)),
                  pl.BlockSpec((DT_LANES, D), lambda b: (0, 0)),
                  pl.BlockSpec((DT_LANES, D), lambda b: (0, 0))],
        out_specs=pl.BlockSpec((1, S, D), lambda b: (b, 0, 0)),
        out_shape=jax.ShapeDtypeStruct((BATCH, S, D), bf16),
        scratch_shapes=[pltpu.VMEM((S, D), f32),
                        pltpu.VMEM((STATE, D), f32),
                        pltpu.VMEM((STATE, D), f32)],
        compiler_params=_cparams(("parallel",), 56),
        name="ssd_scan",
    )(zx, dt, alog, dskip, ng, e_f, e_b)


def _hp_dot(a, b):
    return jnp.dot(a, b, precision=lax.Precision.HIGHEST, preferred_element_type=f32)


def _hyfilter_kernel(z_ref, w1_ref, b1_ref, w2_ref, b2_ref, w3_ref, b3_ref, fr_ref, w4f_ref, w4b_ref,
                     win_ref, w_ref, ka_ref, kb_ref, h_scr, tap_scr, tb_scr, pp_scr, pq_scr, *, seg, nb):
    lb = seg // nb
    tc = tap_scr.shape[1]

    @pl.when(pl.program_id(0) == 0)
    def _():
        fr = fr_ref[...]

        def hidden(c, carry):
            r0 = pl.multiple_of(c * RC, RC)
            h = jnp.sin(fr[0:1] * (_hp_dot(z_ref[pl.ds(r0, RC), :], w1_ref[...]) + b1_ref[...]))
            h = jnp.sin(fr[1:2] * (_hp_dot(h, w2_ref[...]) + b2_ref[...]))
            h_scr[pl.ds(r0, RC), :] = jnp.sin(fr[2:3] * (_hp_dot(h, w3_ref[...]) + b3_ref[...]))
            return carry
        lax.fori_loop(0, 2 * seg // RC, hidden, 0)

    def taps(w4_ref):
        def body(c, ssq):
            r0 = pl.multiple_of(c * RC, RC)
            t = _hp_dot(h_scr[pl.ds(r0, RC), :], w4_ref[...]) * win_ref[pl.ds(r0, RC), :]
            row = r0 + lax.broadcasted_iota(jnp.int32, t.shape, 0)
            t = jnp.where(row == 0, 0.0, t)
            tap_scr[pl.ds(r0, RC), :] = t
            return ssq + jnp.sum(t * t, axis=0, keepdims=True)
        return body
    ssq = lax.fori_loop(0, seg // RC, taps(w4b_ref), jnp.zeros((1, tc), f32))
    nrm = lax.rsqrt(lax.fori_loop(seg // RC, 2 * seg // RC, taps(w4f_ref), ssq) + 1e-6)

    def scale(c, carry):
        r0 = pl.multiple_of(c * RC, RC)
        tb_scr[pl.ds(r0, RC), :] = (tap_scr[pl.ds(r0, RC), :] * nrm).astype(bf16)
        return carry
    lax.fori_loop(0, 2 * seg // RC, scale, 0)

    row = lax.broadcasted_iota(jnp.int32, (lb, tc), 0)
    sgn = jnp.where((row & 1) == 0, 1.0, -1.0)
    for s in range(2 * nb):
        blk = tb_scr[s * lb:(s + 1) * lb, :]
        p = _bdot(w_ref[0:lb, :], blk)
        q = _bdot(w_ref[lb:2 * lb, :], blk)
        if s > 0:
            t0 = tb_scr[(s - 1) * lb:(s - 1) * lb + 16, :][0:1].astype(f32)
            ka_ref[s - 1] = p + sgn * (pp_scr[...] - t0)
            kb_ref[s - 1] = jnp.where(row == 0, q + pq_scr[...] - t0, q + sgn * pq_scr[...])
        pp_scr[...] = p
        pq_scr[...] = q


HY_TC = 256
HY_NB = 4


def _hyfilter_call(zt, w1, b1, w2, b2, w3, b3, fr, w4, win, wtab, seg, nb):
    tc, lb = HY_TC, seg // nb
    nct = D // tc
    const = lambda shape: pl.BlockSpec(shape, lambda c: (0, 0))
    return pl.pallas_call(
        functools.partial(_hyfilter_kernel, seg=seg, nb=nb),
        grid=(nct,),
        in_specs=[const((2 * seg, 128)), const((128, HY_FFN)), const((1, HY_FFN)),
                  const((HY_FFN, HY_FFN)), const((1, HY_FFN)),
                  const((HY_FFN, HY_FFN)), const((1, HY_FFN)), const((8, HY_FFN)),
                  pl.BlockSpec((HY_FFN, tc), lambda c: (0, c)),
                  pl.BlockSpec((HY_FFN, tc), lambda c: (0, c + nct)),
                  pl.BlockSpec((2 * seg, tc), lambda c: (0, c)),
                  const((2 * lb, lb))],
        out_specs=[pl.BlockSpec((2 * nb - 1, lb, tc), lambda c: (0, 0, c)),
                   pl.BlockSpec((2 * nb - 1, lb, tc), lambda c: (0, 0, c))],
        out_shape=[jax.ShapeDtypeStruct((2 * nb - 1, lb, D), f32)] * 2,
        scratch_shapes=[pltpu.VMEM((2 * seg, HY_FFN), f32),
                        pltpu.VMEM((2 * seg, tc), f32), pltpu.VMEM((2 * seg, tc), bf16),
                        pltpu.VMEM((lb, tc), f32), pltpu.VMEM((lb, tc), f32)],
        compiler_params=_cparams(("arbitrary",), 40),
        name=f"hyena_filter_{seg}",
    )(zt, w1, b1, w2, b2, w3, b3, fr, w4, w4, win, wtab)


HY_PIECE = 64


def _hyconv_segment(u_ref, x0_ref, bias_ref, o_ref, w_ref, wt_ref, ka_ref, kb_ref,
                    ua_scr, ub_scr, ya_scr, yb_scr, row0, nb, lb):
    for i in range(nb):
        ui = u_ref[0, row0 + i * lb:row0 + (i + 1) * lb, :]
        ua_scr[i, 0:lb, :] = _bdot(w_ref[0:lb, :], ui)
        ub_scr[i, 0:lb, :] = _bdot(w_ref[lb:2 * lb, :], ui)

    is0 = lax.broadcasted_iota(jnp.int32, (HY_PIECE, ka_ref.shape[2]), 0) == 0
    for o in range(nb):
        for r in range(0, lb, HY_PIECE):
            rows = slice(r, r + HY_PIECE)
            ya = yb = None
            for i in range(nb):
                a, b = ua_scr[i, rows, :], ub_scr[i, rows, :]
                ka, kb = ka_ref[o - i + nb - 1, rows, :], kb_ref[o - i + nb - 1, rows, :]
                if r == 0:
                    pa = jnp.where(is0, a * ka, a * ka - b * kb)
                    pb = jnp.where(is0, b * kb, a * kb + b * ka)
                else:
                    pa, pb = a * ka - b * kb, a * kb + b * ka
                ya = pa if ya is None else ya + pa
                yb = pb if yb is None else yb + pb
            scale = jnp.where(is0, 0.5 / lb, 1.0 / lb) if r == 0 else 1.0 / lb
            ya_scr[o, rows, :] = (ya * scale).astype(bf16)
            yb_scr[o, rows, :] = (yb * scale).astype(bf16)
        y = _bdot(wt_ref[:, 0:lb], ya_scr[o, 0:lb, :]) + _bdot(wt_ref[:, lb:2 * lb], yb_scr[o, 0:lb, :])
        rows = slice(row0 + o * lb, row0 + (o + 1) * lb)
        y = y + bias_ref[...] * u_ref[0, rows, :].astype(f32)
        o_ref[0, rows, :] = (x0_ref[0, rows, :].astype(f32) * y).astype(bf16)


def _hyconv_kernel(u_ref, x0_ref, bias_ref, w_ref, wt_ref, ka_ref, kb_ref, cw_ref, cwt_ref, cka_ref,
                   ckb_ref, o_ref, ua_scr, ub_scr, ya_scr, yb_scr):
    scr = (ua_scr, ub_scr, ya_scr, yb_scr)
    _hyconv_segment(u_ref, x0_ref, bias_ref, o_ref, w_ref, wt_ref, ka_ref, kb_ref, *scr,
                    row0=0, nb=HY_NB, lb=SEQ // HY_NB)
    _hyconv_segment(u_ref, x0_ref, bias_ref, o_ref, cw_ref, cwt_ref, cka_ref, ckb_ref, *scr,
                    row0=SEQ, nb=1, lb=CTX)


def _hyconv_call(u, x0, bias, lat, ctx):
    tc, lb = HY_TC, SEQ // HY_NB
    full = lambda a: pl.BlockSpec(a.shape, lambda c, b: (0,) * a.ndim)
    spec = lambda a: pl.BlockSpec(a.shape[:2] + (tc,), lambda c, b: (0, 0, c))
    seq = pl.BlockSpec((1, S, tc), lambda c, b: (b, 0, c))
    return pl.pallas_call(
        _hyconv_kernel,
        grid=(D // tc, BATCH),
        in_specs=[seq, seq, pl.BlockSpec((1, tc), lambda c, b: (0, c)),
                  full(lat[0]), full(lat[1]), spec(lat[2]), spec(lat[3]),
                  full(ctx[0]), full(ctx[1]), spec(ctx[2]), spec(ctx[3])],
        out_specs=seq,
        out_shape=jax.ShapeDtypeStruct((BATCH, S, D), bf16),
        scratch_shapes=[pltpu.VMEM((HY_NB, lb, tc), f32), pltpu.VMEM((HY_NB, lb, tc), f32),
                        pltpu.VMEM((HY_NB, lb, tc), bf16), pltpu.VMEM((HY_NB, lb, tc), bf16)],
        compiler_params=_cparams(("parallel", "parallel"), 40),
        name="hyena_conv",
    )(u, x0, bias, *lat, *ctx)


def _rope(x, cos_t, sin_t):
    n = x.shape[1]
    lane = lax.broadcasted_iota(jnp.int32, x.shape, 1)
    first = (lane & 16) == 0
    sw = jnp.where(first, pltpu.roll(x, n - 16, 1), pltpu.roll(x, 16, 1))
    return x * cos_t + sw * sin_t


assert WINDOW == ABLK
LOG2E = math.log2(math.e)
A_SPAN = ABLK + 2 * WINDOW
A_KEYS = A_SPAN + CTX
A_COLS = ATT_GROUP * ABLK


def _attn_kernel(sink_ref, q_ref, kv_ref, cos_ref, sin_ref, o_ref, q_scr, k_scr, vt_scr, s_scr, p_scr):
    g = pl.program_id(1)
    qscale = HEAD_P ** -0.5 * LOG2E

    def prep(r0, with_rope):
        q = q_ref[0, pl.ds(r0, RC), :].astype(f32) * qscale
        kv = kv_ref[0, pl.ds(r0, RC), :].astype(f32)
        k = kv
        if with_rope:
            cos_t, sin_t = cos_ref[pl.ds(r0, RC), :], sin_ref[pl.ds(r0, RC), :]
            q = _rope(q, cos_t, sin_t)
            k = _rope(kv, cos_t[:, 0:2 * HEAD_P], sin_t[:, 0:2 * HEAD_P])
        low = lax.broadcasted_iota(jnp.int32, kv.shape, 1) < HEAD_P
        k_scr[pl.ds(r0, RC), :] = jnp.where(low, k, 0.0).astype(bf16)
        for r in range(ATT_GROUP):
            slab = q[:, (r // 2) * 2 * HEAD_P:(r // 2 + 1) * 2 * HEAD_P]
            if r % 2:
                slab = pltpu.roll(slab, HEAD_P, 1)
            q_scr[r, pl.ds(r0, RC), :] = jnp.where(low, slab, 0.0).astype(bf16)
        v_t = kv.T[HEAD_P:2 * HEAD_P, :].astype(bf16)
        c0 = r0 // ABLK
        for h in range(RC // ABLK):
            vt_scr[c0 + h] = v_t[:, h * ABLK:(h + 1) * ABLK]
    _row_chunks(SEQ, lambda r0: prep(r0, True))
    for r0 in range(SEQ, S, RC):
        prep(r0, False)

    kmq = (lax.broadcasted_iota(jnp.int32, (ABLK, A_COLS), 0)
           - (lax.broadcasted_iota(jnp.int32, (ABLK, A_COLS), 1) & (ABLK - 1)))

    def block(q0, start, masks, slot):
        qstack = jnp.concatenate([q_scr[r, pl.ds(q0, ABLK), :] for r in range(ATT_GROUP)], axis=0)
        chunks = []
        if masks:
            s_loc = _bdot_nt(k_scr[pl.ds(start, len(masks) * ABLK), :], qstack)
            for c, kind in enumerate(masks):
                s_c = s_loc[c * ABLK:(c + 1) * ABLK]
                if kind is not None:
                    s_c = jnp.where(kmq >= 0 if kind == 'ge' else kmq <= 0, s_c, -jnp.inf)
                s_scr[slot, c * ABLK:(c + 1) * ABLK, :] = s_c
                chunks.append((c * ABLK, start // ABLK + c))
        n_loc = len(chunks) * ABLK
        s_scr[slot, n_loc:n_loc + CTX, :] = _bdot_nt(k_scr[SEQ:S, :], qstack)
        chunks += [(n_loc + c * ABLK, SEQ // ABLK + c) for c in range(CTX // ABLK)]

        inv = []
        for r in range(ATT_GROUP):
            cols = slice(r * ABLK, (r + 1) * ABLK)
            sink = sink_ref[g * ATT_GROUP + r] * LOG2E
            m = jnp.full((1, ABLK), sink, f32)
            for row, _ in chunks:
                m = jnp.maximum(m, jnp.max(s_scr[slot, row:row + ABLK, cols], axis=0, keepdims=True))
            den = jnp.exp2(sink - m)
            for row, _ in chunks:
                p = jnp.exp2(s_scr[slot, row:row + ABLK, cols] - m)
                den = den + jnp.sum(p, axis=0, keepdims=True)
                p_scr[slot, row:row + ABLK, cols] = p.astype(bf16)
            inv.append(1.0 / den)

        o_t = None
        for row, vc in chunks:
            part = _bdot(vt_scr[vc], p_scr[slot, row:row + ABLK, :])
            o_t = part if o_t is None else o_t + part
        o_t = jnp.concatenate([o_t[:, r * ABLK:(r + 1) * ABLK] * inv[r] for r in range(ATT_GROUP)], axis=0)
        o_ref[0, pl.ds(q0, ABLK), :] = o_t.T.astype(bf16)

    n_blk = SEQ // ABLK
    block(0, 0, (None, 'le'), 0)
    block(SEQ - ABLK, SEQ - 2 * ABLK, ('ge', None), 1)

    def lat_pair(j, carry):
        for slot in range(2):
            q0 = pl.multiple_of((1 + 2 * j + slot) * ABLK, ABLK)
            block(q0, pl.multiple_of(q0 - WINDOW, ABLK), ('ge', None, 'le'), slot)
        return carry
    lax.fori_loop(0, (n_blk - 2) // 2, lat_pair, 0)

    for slot, q0 in enumerate(range(SEQ, S, ABLK)):
        block(q0, 0, (), slot % 2)


def _attn_call(qkv, sinks, cos_t, sin_t):
    gw = ATT_GROUP * HEAD_P
    return pl.pallas_call(
        _attn_kernel,
        grid_spec=pltpu.PrefetchScalarGridSpec(
            num_scalar_prefetch=1,
            grid=(BATCH, KV_HEADS),
            in_specs=[pl.BlockSpec((1, S, gw), lambda b, g, s: (b, 0, g)),
                      pl.BlockSpec((1, S, 2 * HEAD_P), lambda b, g, s: (b, 0, QW // (2 * HEAD_P) + g)),
                      pl.BlockSpec((SEQ, gw), lambda b, g, s: (0, 0)),
                      pl.BlockSpec((SEQ, gw), lambda b, g, s: (0, 0))],
            out_specs=pl.BlockSpec((1, S, gw), lambda b, g, s: (b, 0, g)),
            scratch_shapes=[pltpu.VMEM((ATT_GROUP, S, 2 * HEAD_P), bf16),
                            pltpu.VMEM((S, 2 * HEAD_P), bf16),
                            pltpu.VMEM((S // ABLK, HEAD_P, ABLK), bf16),
                            pltpu.VMEM((2, A_KEYS, A_COLS), f32),
                            pltpu.VMEM((2, A_KEYS, A_COLS), bf16)]),
        out_shape=jax.ShapeDtypeStruct((BATCH, S, QW), bf16),
        compiler_params=_cparams(("parallel", "parallel"), 40),
        name="window_attn",
    )(sinks, qkv, qkv, cos_t, sin_t)


RES_TM = 768


def _outproj_kernel(x_ref, mod_ref, *rest, n_in):
    a_refs, w_refs, o_ref = rest[:n_in], rest[n_in:2 * n_in], rest[2 * n_in]
    for r0 in range(0, RES_TM, RC):
        acc = _bdot(a_refs[0][0, r0:r0 + RC, :], w_refs[0][...])
        for a_ref, w_ref in zip(a_refs[1:], w_refs[1:]):
            acc = acc + _bdot(a_ref[0, r0:r0 + RC, :], w_ref[...])
        gate = _mod_row(mod_ref[0], 2, pl.program_id(1) * RES_TM + r0)
        o_ref[0, r0:r0 + RC, :] = x_ref[0, r0:r0 + RC, :] + gate * acc


def _outproj_call(xs, mods_l, acts, ws, name):
    n_in = len(acts)
    in_specs = [pl.BlockSpec((1, RES_TM, D), lambda b, i: (b, i, 0)),
                pl.BlockSpec((1, MOD_ROWS, D), lambda b, i: (b, 0, 0))]
    in_specs += [pl.BlockSpec((1, RES_TM, a.shape[2]), lambda b, i: (b, i, 0)) for a in acts]
    in_specs += [pl.BlockSpec(w.shape, lambda b, i: (0, 0)) for w in ws]
    return pl.pallas_call(
        functools.partial(_outproj_kernel, n_in=n_in),
        grid=(BATCH, S // RES_TM),
        in_specs=in_specs,
        out_specs=pl.BlockSpec((1, RES_TM, D), lambda b, i: (b, i, 0)),
        out_shape=jax.ShapeDtypeStruct((BATCH, S, D), f32),
        input_output_aliases={0: 0},
        compiler_params=_cparams(("parallel", "parallel"), 40),
        name=name,
    )(xs, mods_l, *acts, *ws)


FFN_TM = 768
FFN_TH = 256
FFN_NH = FFN_H // FFN_TH


def _ffn_kernel(x_ref, mod_ref, g_ref, wg_ref, wu_ref, wo_ref, o_ref, h_scr, acc):
    row0 = pl.program_id(1) * FFN_TM
    for r0 in range(0, FFN_TM, RC):
        h_scr[r0:r0 + RC, :] = _norm_mod(x_ref[0, r0:r0 + RC, :], g_ref[...], mod_ref[0],
                                         3, 4, row0 + r0).astype(bf16)
    for k in range(FFN_NH):
        act = (_silu(_bdot(h_scr[...], wg_ref[k])) * _bdot(h_scr[...], wu_ref[k])).astype(bf16)
        down = _bdot(act, wo_ref[k])
        if k == 0:
            acc[...] = down
        else:
            acc[...] += down
    for r0 in range(0, FFN_TM, RC):
        gate = _mod_row(mod_ref[0], 5, row0 + r0)
        o_ref[0, r0:r0 + RC, :] = x_ref[0, r0:r0 + RC, :] + gate * acc[r0:r0 + RC, :]


def _ffn_call(xs, mods_l, g, wg, wu, wo):
    resident = lambda shape: pl.BlockSpec(shape, lambda b, i: (0, 0, 0), pipeline_mode=pl.Buffered(1))
    return pl.pallas_call(
        _ffn_kernel,
        grid=(BATCH, S // FFN_TM),
        in_specs=[pl.BlockSpec((1, FFN_TM, D), lambda b, i: (b, i, 0)),
                  pl.BlockSpec((1, MOD_ROWS, D), lambda b, i: (b, 0, 0)),
                  pl.BlockSpec((1, D), lambda b, i: (0, 0)),
                  resident((FFN_NH, D, FFN_TH)), resident((FFN_NH, D, FFN_TH)),
                  resident((FFN_NH, FFN_TH, D))],
        out_specs=pl.BlockSpec((1, FFN_TM, D), lambda b, i: (b, i, 0)),
        out_shape=jax.ShapeDtypeStruct((BATCH, S, D), f32),
        scratch_shapes=[pltpu.VMEM((FFN_TM, D), bf16), pltpu.VMEM((FFN_TM, D), f32)],
        input_output_aliases={0: 0},
        compiler_params=_cparams(("parallel", "parallel"), 52),
        name="swiglu_ffn",
    )(xs, mods_l, g, wg, wu, wo)


FIN_TM = 256


def _final_kernel(x_ref, g_ref, o_ref):
    x = x_ref[0]
    ms = jnp.mean(x * x, axis=-1, keepdims=True)
    o_ref[0] = x * lax.rsqrt(ms + EPS) * g_ref[...]


def _final_call(xs, g):
    return pl.pallas_call(
        _final_kernel,
        grid=(BATCH, SEQ // FIN_TM),
        in_specs=[pl.BlockSpec((1, FIN_TM, D), lambda b, i: (b, i, 0)),
                  pl.BlockSpec((1, D), lambda b, i: (0, 0))],
        out_specs=pl.BlockSpec((1, FIN_TM, D), lambda b, i: (b, i, 0)),
        out_shape=jax.ShapeDtypeStruct((BATCH, SEQ, D), f32),
        compiler_params=_cparams(("parallel", "parallel"), 16),
        name="final_norm",
    )(xs, g)


def _dft_table(lb):
    n = 2 * lb
    f = jnp.arange(lb, dtype=jnp.int32)[:, None]
    t = jnp.arange(lb, dtype=jnp.int32)[None, :]
    ang = (2.0 * math.pi / n) * ((f * t) % n).astype(f32)
    nyq = jnp.where((t & 1) == 0, 1.0, -1.0).astype(f32)
    sin_rows = jnp.where(f == 0, nyq, jnp.sin(ang))
    return jnp.concatenate([jnp.cos(ang), sin_rows], axis=0)


def _hy_features(seg):
    t = jnp.abs(jnp.arange(2 * seg) - seg).astype(f32)
    t_unit = t / (seg - 1)
    ang = 2.0 * math.pi * t / seg
    fb = jnp.linspace(1e-4, HY_BANDS - 1, HY_BANDS, dtype=f32)
    z = jnp.concatenate([t_unit[:, None], jnp.cos(ang[:, None] * fb), -jnp.sin(ang[:, None] * fb)], axis=-1)
    z = jnp.pad(z, ((0, 0), (0, 128 - HY_EMB)))
    deltas = jnp.linspace(math.log(1e-2) / 1.5, math.log(1e-2) / 0.3, D, dtype=f32)
    win = jnp.exp(-t_unit[:, None] * jnp.abs(deltas)[None, :])
    return z, win


def _rope_tables():
    nf = HEAD_P // 4
    inv = ROPE_BASE ** (-jnp.arange(nf, dtype=f32) / nf)
    pos = jnp.arange(SEQ)
    row = (pos // GRID_W).astype(f32)
    col = (pos % GRID_W).astype(f32)
    ar = row[:, None] * inv[None, :]
    ac = col[:, None] * inv[None, :]
    ang = jnp.concatenate([ar, ar, ac, ac], axis=-1)
    sign = jnp.tile(jnp.concatenate([-jnp.ones(nf, f32), jnp.ones(nf, f32)]), 2)
    cos_t = jnp.tile(jnp.cos(ang), (1, ATT_GROUP))
    sin_t = jnp.tile(jnp.sin(ang) * sign[None, :], (1, ATT_GROUP))
    return cos_t, sin_t


def _head_expand(lane_off):
    r = jnp.arange(DT_LANES)[:, None]
    c = jnp.arange(D)[None, :]
    return (c // HEAD_P + lane_off == r).astype(bf16)


def _pad_rows(a, rows):
    return jnp.pad(a, ((0, rows - a.shape[0]), (0, 0)))


def _pad_lanes(a, lanes):
    return jnp.pad(a, ((0, 0), (0, lanes - a.shape[1])))


def kernel(x, c, ctx, c_ctx, ada_w, ada_b, norm1_g, norm2_g, hy_in_w, ssd_conv_w, ssd_conv_b, ssd_dt_bias, ssd_a_log, ssd_d, ssd_norm_g, hy_conv_w, hy_conv_b, hy_w1, hy_b1, hy_w2, hy_b2, hy_w3, hy_b3, hy_w4, hy_freq, hy_bias, hy_out_w, attn_qkv_w, attn_sinks, attn_out_w, ffn_w_in, ffn_w_out, final_g):
    xs = jnp.concatenate([x, ctx], axis=1)

    cc = jnp.concatenate([c, c_ctx[None, :], jnp.zeros((MOD_ROWS - BATCH - 1, D), f32)], axis=0)
    modv = _mod_call(cc, ada_w, ada_b)
    lat = modv[:, :BATCH].reshape(DEPTH, BATCH, N_MOD, D)
    cmod = jnp.broadcast_to(modv[:, BATCH].reshape(DEPTH, 1, N_MOD, D), (DEPTH, BATCH, N_MOD, D))
    pad = jnp.zeros((DEPTH, BATCH, CTX_MOD - N_MOD, D), f32)
    mods = jnp.concatenate([lat, pad, cmod, pad], axis=2)

    tabs = {}
    for seg, nb in ((SEQ, HY_NB), (CTX, 1)):
        w = _dft_table(seg // nb)
        tabs[seg] = (w.astype(bf16), w.T.astype(bf16)) + _hy_features(seg)
    e_f, e_b = _head_expand(0), _head_expand(HEADS)
    cos_t, sin_t = _rope_tables()

    for layer in range(DEPTH):
        j = layer // 2
        mods_l = mods[layer]
        g1 = norm1_g[layer][None, :]
        if layer % 2 == 0:
            w_in = hy_in_w[j]
            w_zx = w_in[:, :COL_DT].astype(bf16)
            wdt = _pad_lanes(w_in[:, COL_DT:COL_HY], DT_LANES).astype(bf16)
            dtb = _pad_lanes(ssd_dt_bias[j].reshape(1, 2 * HEADS), DT_LANES)
            cw = jnp.concatenate([jnp.zeros((4, D), f32),
                                  jnp.concatenate([ssd_conv_w[j], ssd_conv_b[j][None, :]], axis=0)], axis=1)
            zx, dt = _inproj_ssd_call(xs, mods_l, g1, w_zx, _pad_rows(cw, 8), wdt, dtb)
            cw_hy = _pad_rows(jnp.concatenate([hy_conv_w[j], hy_conv_b[j][None, :]], axis=0), 8)
            x0, u = _inproj_hy_call(xs, mods_l, g1, w_in[:, COL_HY:].astype(bf16), cw_hy)

            alog = _pad_lanes(ssd_a_log[j].reshape(1, 2 * HEADS), DT_LANES)
            dskip = jnp.repeat(ssd_d[j], HEAD_P)[None, :]
            yn = _ssd_call(zx, dt, alog, dskip, ssd_norm_g[j][None, :], e_f, e_b)

            segs = []
            for seg, nb in ((SEQ, HY_NB), (CTX, 1)):
                wtab, wtab_t, zt, win = tabs[seg]
                ka, kb = _hyfilter_call(zt, _pad_rows(hy_w1[j], 128), hy_b1[j][None, :], hy_w2[j],
                                        hy_b2[j][None, :], hy_w3[j], hy_b3[j][None, :],
                                        _pad_rows(hy_freq[j], 8), hy_w4[j], win, wtab, seg, nb)
                segs.append((wtab, wtab_t, ka, kb))
            gh = _hyconv_call(u, x0, hy_bias[j][None, :], *segs)
            w_out = hy_out_w[j].astype(bf16)
            xs = _outproj_call(xs, mods_l, [yn, gh], [w_out[:D], w_out[D:]], "outproj_even")
        else:
            wq = attn_qkv_w[j]
            wk = wq[:, QW:QW + KVW].reshape(D, KV_HEADS, HEAD_P)
            wv = wq[:, QW + KVW:].reshape(D, KV_HEADS, HEAD_P)
            w_kv = jnp.concatenate([wk, wv], axis=2).reshape(D, 2 * KVW)
            w_qkv = jnp.concatenate([wq[:, :QW], w_kv], axis=1).astype(bf16)
            qkv = _nmm_call(xs, mods_l, g1, w_qkv, 512, "qkv_proj")
            o = _attn_call(qkv, attn_sinks[j], cos_t, sin_t)
            xs = _outproj_call(xs, mods_l, [o], [attn_out_w[j].astype(bf16)], "outproj_attn")
        w_gu = ffn_w_in[layer].astype(bf16).reshape(D, 2, FFN_NH, FFN_TH).transpose(1, 2, 0, 3)
        xs = _ffn_call(xs, mods_l, norm2_g[layer][None, :], w_gu[0], w_gu[1],
                       ffn_w_out[layer].astype(bf16).reshape(FFN_NH, FFN_TH, D))
    return _final_call(xs, final_g[None, :])
```

```python
import functools
import math

import jax
import jax.numpy as jnp
from jax import lax
from jax.experimental import pallas as pl
from jax.experimental.pallas import tpu as pltpu

f32 = jnp.float32
bf16 = jnp.bfloat16

D = 1024
BATCH = 8
SEQ = 2048
DEPTH = 4
GRID_W = 64
CTX = 256
S = SEQ + CTX
EPS = 1e-6
N_MOD = 6
MOD_ROWS = 16
CTX_MOD = 8

HEADS = 16
HEAD_P = 64
STATE = 128
GROUPS = 2
CHUNK = 128
XBC = D + 2 * GROUPS * STATE
ZX = D + XBC
DT_LANES = 128
HY_BANDS = 16
HY_EMB = 1 + 2 * HY_BANDS
HY_FFN = 64
COL_XBC = D
COL_DT = COL_XBC + XBC
COL_HY = COL_DT + 2 * HEADS

Q_HEADS = 16
KV_HEADS = 4
ATT_GROUP = Q_HEADS // KV_HEADS
QW = Q_HEADS * HEAD_P
KVW = KV_HEADS * HEAD_P
WINDOW = 128
ABLK = 128
ROPE_BASE = 10000.0
FFN_H = -(-8 * D // (3 * 256)) * 256

RC = 256
NMM_RC = 768
VMEM_CAP = 56 * 1024 * 1024


def _cparams(sem, vmem_mb):
    return pltpu.CompilerParams(dimension_semantics=sem,
                                vmem_limit_bytes=min(int(vmem_mb * 1024 * 1024), VMEM_CAP))


def _silu(v):
    return v * jax.nn.sigmoid(v)


def _softplus(v):
    return jnp.maximum(v, 0.0) + jnp.log1p(jnp.exp(-jnp.abs(v)))


def _bdot(a, b):
    return jnp.dot(a, b, preferred_element_type=f32)


def _bdot_nt(a, b):
    return lax.dot_general(a, b, (((1,), (1,)), ((), ())), preferred_element_type=f32)


def _row_chunks(n_rows, body):
    def step(c, carry):
        body(pl.multiple_of(c * RC, RC))
        return carry
    lax.fori_loop(0, n_rows // RC, step, 0)


def _norm_mod(x, g, mod, shift_row, scale_row, row0):
    ms = jnp.mean(x * x, axis=-1, keepdims=True)
    scale = _mod_row(mod, scale_row, row0)
    return x * lax.rsqrt(ms + EPS) * (g * (1.0 + scale)) + _mod_row(mod, shift_row, row0)


def _mod_row(mod, k, row0):
    return jnp.where(row0 >= SEQ, mod[CTX_MOD + k:CTX_MOD + k + 1], mod[k:k + 1])


def _norm_mod_to_scratch(x_ref, g_ref, mod_ref, h_scr):
    def body(r0):
        h_scr[pl.ds(r0, RC), :] = _norm_mod(x_ref[0, pl.ds(r0, RC), :], g_ref[...], mod_ref[0],
                                            0, 1, r0).astype(bf16)
    _row_chunks(S, body)


CONV_ROWS = 64


def _conv3_rows(p_scr, r0, cw):
    n = CONV_ROWS
    v = p_scr[pl.ds(r0, n), :]
    if isinstance(r0, int):
        up_row, dn_row = max(r0 - 8, 0), min(r0 + n, S - 8)
        seg_start, seg_end = r0 in (0, SEQ), r0 + n in (SEQ, S)
    else:
        up_row = pl.multiple_of(jnp.maximum(r0 - 8, 0), 8)
        dn_row = pl.multiple_of(jnp.minimum(r0 + n, S - 8), 8)
        seg_start = jnp.logical_or(r0 == 0, r0 == SEQ)
        seg_end = jnp.logical_or(r0 + n == SEQ, r0 + n == S)
    up = jnp.where(seg_start, 0.0, p_scr[pl.ds(up_row, 8), :][7:8])
    dn = jnp.where(seg_end, 0.0, p_scr[pl.ds(dn_row, 8), :][0:1])
    sub = lax.broadcasted_iota(jnp.int32, (8, v.shape[1]), 0)
    prev = pltpu.roll(v, 1, 0)
    prev = jnp.concatenate([jnp.where(sub == 0, up, prev[0:8]), prev[8:]], axis=0)
    nxt = pltpu.roll(v, n - 1, 0)
    nxt = jnp.concatenate([nxt[:n - 8], jnp.where(sub == 7, dn, nxt[n - 8:])], axis=0)
    return prev * cw[0:1] + v * cw[1:2] + nxt * cw[2:3] + cw[3:4]


def _conv_pieces(r0, piece):
    for k in range(0, RC, CONV_ROWS):
        piece(r0 + k if isinstance(r0, int) else pl.multiple_of(r0 + k, CONV_ROWS))


def _project_then_conv(mm, conv):
    mm(0)
    mm(RC)

    def step(c, carry):
        r0 = pl.multiple_of(c * RC, RC)
        conv(pl.multiple_of(r0 - 2 * RC, RC))
        mm(r0)
        return carry
    lax.fori_loop(2, S // RC, step, 0)
    conv(S - 2 * RC)
    conv(S - RC)


def _mod_kernel(c_ref, w_ref, b_ref, o_ref):
    sc = _silu(c_ref[...]).astype(bf16)
    o_ref[0] = _bdot(sc, w_ref[0].astype(bf16)) + b_ref[0]


def _mod_call(cc, ada_w, ada_b):
    tn = 1024
    return pl.pallas_call(
        _mod_kernel,
        grid=(DEPTH, N_MOD * D // tn),
        in_specs=[pl.BlockSpec((MOD_ROWS, D), lambda l, j: (0, 0)),
                  pl.BlockSpec((1, D, tn), lambda l, j: (l, 0, j)),
                  pl.BlockSpec((1, 1, tn), lambda l, j: (l, 0, j))],
        out_specs=pl.BlockSpec((1, MOD_ROWS, tn), lambda l, j: (l, 0, j)),
        out_shape=jax.ShapeDtypeStruct((DEPTH, MOD_ROWS, N_MOD * D), f32),
        compiler_params=_cparams(("parallel", "parallel"), 24),
        name="adaln_mod",
    )(cc, ada_w, ada_b.reshape(DEPTH, 1, N_MOD * D))


def _nmm_kernel(x_ref, mod_ref, g_ref, w_ref, o_ref, h_scr):
    @pl.when(pl.program_id(1) == 0)
    def _():
        _norm_mod_to_scratch(x_ref, g_ref, mod_ref, h_scr)

    def body(c, carry):
        r0 = pl.multiple_of(c * NMM_RC, NMM_RC)
        o_ref[0, pl.ds(r0, NMM_RC), :] = _bdot(h_scr[pl.ds(r0, NMM_RC), :], w_ref[...]).astype(o_ref.dtype)
        return carry
    lax.fori_loop(0, S // NMM_RC, body, 0)


def _nmm_call(xs, mods_l, g, w, tn, name):
    n = w.shape[1]
    return pl.pallas_call(
        _nmm_kernel,
        grid=(BATCH, n // tn),
        in_specs=[pl.BlockSpec((1, S, D), lambda b, j: (b, 0, 0)),
                  pl.BlockSpec((1, MOD_ROWS, D), lambda b, j: (b, 0, 0)),
                  pl.BlockSpec((1, D), lambda b, j: (0, 0)),
                  pl.BlockSpec((D, tn), lambda b, j: (0, j))],
        out_specs=pl.BlockSpec((1, S, tn), lambda b, j: (b, 0, j)),
        out_shape=jax.ShapeDtypeStruct((BATCH, S, n), bf16),
        scratch_shapes=[pltpu.VMEM((S, D), bf16)],
        compiler_params=_cparams(("parallel", "arbitrary"), 40),
        name=name,
    )(xs, mods_l, g, w)


ZX_TN = 512
NZ_T = D // ZX_TN


def _inproj_ssd_kernel(x_ref, mod_ref, g_ref, w_ref, cw_ref, wdt_ref, dtb_ref, zx_ref, dt_ref,
                       h_scr, p_scr):
    j = pl.program_id(1)

    @pl.when(j == 0)
    def _():
        _norm_mod_to_scratch(x_ref, g_ref, mod_ref, h_scr)

        def dt_body(r0):
            sp = _softplus(_bdot(h_scr[pl.ds(r0, RC), :], wdt_ref[...]) + dtb_ref[...])
            lane = lax.broadcasted_iota(jnp.int32, sp.shape, 1)
            dt_ref[0, pl.ds(r0, RC), :] = jnp.where(lane < 2 * HEADS, sp, 0.0)
        _row_chunks(S, dt_body)

    @pl.when(j < NZ_T)
    def _():
        def body(c, carry):
            r0 = pl.multiple_of(c * NMM_RC, NMM_RC)
            zx_ref[0, pl.ds(r0, NMM_RC), :] = _bdot(h_scr[pl.ds(r0, NMM_RC), :], w_ref[...]).astype(bf16)
            return carry
        lax.fori_loop(0, S // NMM_RC, body, 0)

    @pl.when(j >= NZ_T)
    def _():
        def mm(r0):
            p_scr[pl.ds(r0, RC), :] = _bdot(h_scr[pl.ds(r0, RC), :], w_ref[...])

        def conv(r0):
            def piece(a):
                zx_ref[0, pl.ds(a, CONV_ROWS), :] = _silu(_conv3_rows(p_scr, a, cw_ref[...])).astype(bf16)
            _conv_pieces(r0, piece)
        _project_then_conv(mm, conv)


def _inproj_ssd_call(xs, mods_l, g, w_zx, cw, wdt, dtb):
    return pl.pallas_call(
        _inproj_ssd_kernel,
        grid=(BATCH, ZX // ZX_TN),
        in_specs=[pl.BlockSpec((1, S, D), lambda b, j: (b, 0, 0)),
                  pl.BlockSpec((1, MOD_ROWS, D), lambda b, j: (b, 0, 0)),
                  pl.BlockSpec((1, D), lambda b, j: (0, 0)),
                  pl.BlockSpec((D, ZX_TN), lambda b, j: (0, j)),
                  pl.BlockSpec((8, ZX_TN), lambda b, j: (0, j)),
                  pl.BlockSpec((D, DT_LANES), lambda b, j: (0, 0)),
                  pl.BlockSpec((1, DT_LANES), lambda b, j: (0, 0))],
        out_specs=[pl.BlockSpec((1, S, ZX_TN), lambda b, j: (b, 0, j)),
                   pl.BlockSpec((1, S, DT_LANES), lambda b, j: (b, 0, 0))],
        out_shape=[jax.ShapeDtypeStruct((BATCH, S, ZX), bf16),
                   jax.ShapeDtypeStruct((BATCH, S, DT_LANES), f32)],
        scratch_shapes=[pltpu.VMEM((S, D), bf16), pltpu.VMEM((S, ZX_TN), f32)],
        compiler_params=_cparams(("parallel", "arbitrary"), 48),
        name="inproj_ssd",
    )(xs, mods_l, g, w_zx, cw, wdt, dtb)


HY_TN = 256


def _inproj_hy_kernel(x_ref, mod_ref, g_ref, w0_ref, w1_ref, w2_ref, c0_ref, c1_ref, c2_ref,
                      x0_ref, u_ref, h_scr, p0_scr, p1_scr, p2_scr):
    @pl.when(pl.program_id(1) == 0)
    def _():
        _norm_mod_to_scratch(x_ref, g_ref, mod_ref, h_scr)

    def mm_x0(r0):
        p0_scr[pl.ds(r0, RC), :] = _bdot(h_scr[pl.ds(r0, RC), :], w0_ref[...])

    def conv_x0(r0):
        def piece(a):
            x0_ref[0, pl.ds(a, CONV_ROWS), :] = _conv3_rows(p0_scr, a, c0_ref[...]).astype(bf16)
        _conv_pieces(r0, piece)
    _project_then_conv(mm_x0, conv_x0)

    def mm_u(r0):
        h = h_scr[pl.ds(r0, RC), :]
        p1_scr[pl.ds(r0, RC), :] = _bdot(h, w1_ref[...])
        p2_scr[pl.ds(r0, RC), :] = _bdot(h, w2_ref[...])

    def conv_u(r0):
        def piece(a):
            x1 = _conv3_rows(p1_scr, a, c1_ref[...])
            v = _conv3_rows(p2_scr, a, c2_ref[...])
            u_ref[0, pl.ds(a, CONV_ROWS), :] = (x1 * v).astype(bf16)
        _conv_pieces(r0, piece)
    _project_then_conv(mm_u, conv_u)


def _inproj_hy_call(xs, mods_l, g, w_hy, cw_hy):
    nt = D // HY_TN
    wspec = lambda k: pl.BlockSpec((D, HY_TN), lambda b, j, k=k: (0, j + k * nt))
    cspec = lambda k: pl.BlockSpec((8, HY_TN), lambda b, j, k=k: (0, j + k * nt))
    return pl.pallas_call(
        _inproj_hy_kernel,
        grid=(BATCH, nt),
        in_specs=[pl.BlockSpec((1, S, D), lambda b, j: (b, 0, 0)),
                  pl.BlockSpec((1, MOD_ROWS, D), lambda b, j: (b, 0, 0)),
                  pl.BlockSpec((1, D), lambda b, j: (0, 0)),
                  wspec(0), wspec(1), wspec(2), cspec(0), cspec(1), cspec(2)],
        out_specs=[pl.BlockSpec((1, S, HY_TN), lambda b, j: (b, 0, j)),
                   pl.BlockSpec((1, S, HY_TN), lambda b, j: (b, 0, j))],
        out_shape=[jax.ShapeDtypeStruct((BATCH, S, D), bf16),
                   jax.ShapeDtypeStruct((BATCH, S, D), bf16)],
        scratch_shapes=[pltpu.VMEM((S, D), bf16)] + [pltpu.VMEM((S, HY_TN), f32)] * 3,
        compiler_params=_cparams(("parallel", "arbitrary"), 48),
        name="inproj_hyena",
    )(xs, mods_l, g, w_hy, w_hy, w_hy, cw_hy, cw_hy, cw_hy)


N_CHUNK = S // CHUNK
LAT_CHUNKS = SEQ // CHUNK


def _split3_dot(t, a):
    a1 = a.astype(bf16)
    r1 = a - a1.astype(f32)
    a2 = r1.astype(bf16)
    a3 = (r1 - a2.astype(f32)).astype(bf16)
    return _bdot(t, a1) + _bdot(t, a2) + _bdot(t, a3)


def _ssd_chunk(zx_ref, dt_ref, aneg, e_ref, y_scr, st_scr, r0, backward):
    lane_off = HEADS if backward else 0
    x = zx_ref[0, pl.ds(r0, CHUNK), D:2 * D].astype(f32)
    bm = zx_ref[0, pl.ds(r0, CHUNK), 2 * D:2 * D + GROUPS * STATE]
    cm = zx_ref[0, pl.ds(r0, CHUNK), 2 * D + GROUPS * STATE:ZX]
    dt = dt_ref[0, pl.ds(r0, CHUNK), :]
    a = dt * aneg

    ri = lax.broadcasted_iota(jnp.int32, (CHUNK, CHUNK), 0)
    ci = lax.broadcasted_iota(jnp.int32, (CHUNK, CHUNK), 1)
    causal = (ci >= ri) if backward else (ci <= ri)
    tmat = jnp.where(causal, 1.0, 0.0).astype(bf16)
    acum = _split3_dot(tmat, a)
    acum_t = acum.T
    tot = acum[0:1] if backward else acum[CHUNK - 1:CHUNK]
    dte = jnp.exp(tot - acum)
    eac = jnp.exp(acum)
    cdec = jnp.broadcast_to(jnp.exp(tot), (16, DT_LANES))

    e = e_ref[...]
    dt_x = _bdot(dt.astype(bf16), e)
    dte_x = _bdot(dte.astype(bf16), e)
    eac_x = _bdot(eac.astype(bf16), e)
    cdec_x = _bdot(cdec.astype(bf16), e)[0:1]

    xdt = x * dt_x
    xdt_b = xdt.astype(bf16)
    xs_b = (xdt * dte_x).astype(bf16)
    lane = lax.broadcasted_iota(jnp.int32, (CHUNK, 2 * HEAD_P), 1)
    low = lane < HEAD_P
    hg = HEADS // GROUPS
    gw = hg * HEAD_P
    for g in range(GROUPS):
        bg = bm[:, g * STATE:(g + 1) * STATE]
        cg = cm[:, g * STATE:(g + 1) * STATE]
        cb = _bdot_nt(cg, bg)
        hin = st_scr[:, g * gw:(g + 1) * gw]
        yoff = _bdot(cg, hin.astype(bf16)) * eac_x[:, g * gw:(g + 1) * gw]
        for hp in range(hg // 2):
            h0 = g * hg + 2 * hp
            ms = []
            for h in (h0, h0 + 1):
                l0 = lane_off + h
                seg = acum[:, l0:l0 + 1] - acum_t[l0:l0 + 1, :]
                ms.append(cb * jnp.exp(jnp.where(causal, seg, -jnp.inf)))
            lhs = jnp.concatenate(ms, axis=1).astype(bf16)
            xp = xdt[:, h0 * HEAD_P:(h0 + 2) * HEAD_P]
            rhs = jnp.concatenate([jnp.where(low, xp, 0.0), jnp.where(low, 0.0, xp)],
                                  axis=0).astype(bf16)
            yd = _bdot(lhs, rhs) + yoff[:, hp * 2 * HEAD_P:(hp + 1) * 2 * HEAD_P]
            cols = slice(h0 * HEAD_P, (h0 + 2) * HEAD_P)
            y_scr[pl.ds(r0, CHUNK), cols] = y_scr[pl.ds(r0, CHUNK), cols] + yd
        bg_t = bg.astype(f32).T.astype(bf16)
        st_scr[:, g * gw:(g + 1) * gw] = (hin * cdec_x[:, g * gw:(g + 1) * gw]
                                         + _bdot(bg_t, xs_b[:, g * gw:(g + 1) * gw]))
    del xdt_b


def _ssd_kernel(zx_ref, dt_ref, alog_ref, dskip_ref, ng_ref, ef_ref, eb_ref, o_ref,
                y_scr, sf_scr, sb_scr):
    aneg = -jnp.exp(alog_ref[...])
    sf_scr[...] = jnp.zeros_like(sf_scr)
    sb_scr[...] = jnp.zeros_like(sb_scr)

    def init(c, carry):
        r0 = pl.multiple_of(c * CHUNK, CHUNK)
        y_scr[pl.ds(r0, CHUNK), :] = dskip_ref[...] * zx_ref[0, pl.ds(r0, CHUNK), D:2 * D].astype(f32)
        return carry
    lax.fori_loop(0, N_CHUNK, init, 0)

    def step(t, carry):
        cf = jnp.where(t < N_CHUNK - LAT_CHUNKS, LAT_CHUNKS + t, t - (N_CHUNK - LAT_CHUNKS))
        cbk = N_CHUNK - 1 - t
        _ssd_chunk(zx_ref, dt_ref, aneg, ef_ref, y_scr, sf_scr, pl.multiple_of(cf * CHUNK, CHUNK), False)
        _ssd_chunk(zx_ref, dt_ref, aneg, eb_ref, y_scr, sb_scr, pl.multiple_of(cbk * CHUNK, CHUNK), True)
        return carry
    lax.fori_loop(0, N_CHUNK, step, 0)

    def fin(c, carry):
        r0 = pl.multiple_of(c * CHUNK, CHUNK)
        t = y_scr[pl.ds(r0, CHUNK), :] * _silu(zx_ref[0, pl.ds(r0, CHUNK), 0:D].astype(f32))
        ms = jnp.mean(t * t, axis=-1, keepdims=True)
        o_ref[0, pl.ds(r0, CHUNK), :] = (t * lax.rsqrt(ms + EPS) * ng_ref[...]).astype(bf16)
        return carry
    lax.fori_loop(0, N_CHUNK, fin, 0)


def _ssd_call(zx, dt, alog, dskip, ng, e_f, e_b):
    return pl.pallas_call(
        _ssd_kernel,
        grid=(BATCH,),
        in_specs=[pl.BlockSpec((1, S, ZX), lambda b: (b, 0, 0)),
                  pl.BlockSpec((1, S, DT_LANES), lambda b: (b, 0, 0)),
                  pl.BlockSpec((1, DT_LANES), lambda b: (0, 0)),
                  pl.BlockSpec((1, D), lambda b: (0, 0)),
                  pl.BlockSpec((1, D), lambda b: (0, 0)),
                  pl.BlockSpec((DT_LANES, D), lambda b: (0, 0)),
                  pl.BlockSpec((DT_LANES, D), lambda b: (0, 0))],
        out_specs=pl.BlockSpec((1, S, D), lambda b: (b, 0, 0)),
        out_shape=jax.ShapeDtypeStruct((BATCH, S, D), bf16),
        scratch_shapes=[pltpu.VMEM((S, D), f32),
                        pltpu.VMEM((STATE, D), f32),
                        pltpu.VMEM((STATE, D), f32)],
        compiler_params=_cparams(("parallel",), 56),
        name="ssd_scan",
    )(zx, dt, alog, dskip, ng, e_f, e_b)


def _hp_dot(a, b):
    return jnp.dot(a, b, precision=lax.Precision.HIGHEST, preferred_element_type=f32)


def _hyfilter_kernel(z_ref, w1_ref, b1_ref, w2_ref, b2_ref, w3_ref, b3_ref, fr_ref, w4f_ref, w4b_ref,
                     win_ref, w_ref, ka_ref, kb_ref, h_scr, tap_scr, tb_scr, pp_scr, pq_scr, *, seg, nb):
    lb = seg // nb
    tc = tap_scr.shape[1]

    @pl.when(pl.program_id(0) == 0)
    def _():
        fr = fr_ref[...]

        def hidden(c, carry):
            r0 = pl.multiple_of(c * RC, RC)
            h = jnp.sin(fr[0:1] * (_hp_dot(z_ref[pl.ds(r0, RC), :], w1_ref[...]) + b1_ref[...]))
            h = jnp.sin(fr[1:2] * (_hp_dot(h, w2_ref[...]) + b2_ref[...]))
            h_scr[pl.ds(r0, RC), :] = jnp.sin(fr[2:3] * (_hp_dot(h, w3_ref[...]) + b3_ref[...]))
            return carry
        lax.fori_loop(0, 2 * seg // RC, hidden, 0)

    def taps(w4_ref):
        def body(c, ssq):
            r0 = pl.multiple_of(c * RC, RC)
            t = _hp_dot(h_scr[pl.ds(r0, RC), :], w4_ref[...]) * win_ref[pl.ds(r0, RC), :]
            row = r0 + lax.broadcasted_iota(jnp.int32, t.shape, 0)
            t = jnp.where(row == 0, 0.0, t)
            tap_scr[pl.ds(r0, RC), :] = t
            return ssq + jnp.sum(t * t, axis=0, keepdims=True)
        return body
    ssq = lax.fori_loop(0, seg // RC, taps(w4b_ref), jnp.zeros((1, tc), f32))
    nrm = lax.rsqrt(lax.fori_loop(seg // RC, 2 * seg // RC, taps(w4f_ref), ssq) + 1e-6)

    def scale(c, carry):
        r0 = pl.multiple_of(c * RC, RC)
        tb_scr[pl.ds(r0, RC), :] = (tap_scr[pl.ds(r0, RC), :] * nrm).astype(bf16)
        return carry
    lax.fori_loop(0, 2 * seg // RC, scale, 0)

    row = lax.broadcasted_iota(jnp.int32, (lb, tc), 0)
    sgn = jnp.where((row & 1) == 0, 1.0, -1.0)
    for s in range(2 * nb):
        blk = tb_scr[s * lb:(s + 1) * lb, :]
        p = _bdot(w_ref[0:lb, :], blk)
        q = _bdot(w_ref[lb:2 * lb, :], blk)
        if s > 0:
            t0 = tb_scr[(s - 1) * lb:(s - 1) * lb + 16, :][0:1].astype(f32)
            ka_ref[s - 1] = p + sgn * (pp_scr[...] - t0)
            kb_ref[s - 1] = jnp.where(row == 0, q + pq_scr[...] - t0, q + sgn * pq_scr[...])
        pp_scr[...] = p
        pq_scr[...] = q


HY_TC = 256
HY_NB = 4


def _hyfilter_call(zt, w1, b1, w2, b2, w3, b3, fr, w4, win, wtab, seg, nb):
    tc, lb = HY_TC, seg // nb
    nct = D // tc
    const = lambda shape: pl.BlockSpec(shape, lambda c: (0, 0))
    return pl.pallas_call(
        functools.partial(_hyfilter_kernel, seg=seg, nb=nb),
        grid=(nct,),
        in_specs=[const((2 * seg, 128)), const((128, HY_FFN)), const((1, HY_FFN)),
                  const((HY_FFN, HY_FFN)), const((1, HY_FFN)),
                  const((HY_FFN, HY_FFN)), const((1, HY_FFN)), const((8, HY_FFN)),
                  pl.BlockSpec((HY_FFN, tc), lambda c: (0, c)),
                  pl.BlockSpec((HY_FFN, tc), lambda c: (0, c + nct)),
                  pl.BlockSpec((2 * seg, tc), lambda c: (0, c)),
                  const((2 * lb, lb))],
        out_specs=[pl.BlockSpec((2 * nb - 1, lb, tc), lambda c: (0, 0, c)),
                   pl.BlockSpec((2 * nb - 1, lb, tc), lambda c: (0, 0, c))],
        out_shape=[jax.ShapeDtypeStruct((2 * nb - 1, lb, D), f32)] * 2,
        scratch_shapes=[pltpu.VMEM((2 * seg, HY_FFN), f32),
                        pltpu.VMEM((2 * seg, tc), f32), pltpu.VMEM((2 * seg, tc), bf16),
                        pltpu.VMEM((lb, tc), f32), pltpu.VMEM((lb, tc), f32)],
        compiler_params=_cparams(("arbitrary",), 40),
        name=f"hyena_filter_{seg}",
    )(zt, w1, b1, w2, b2, w3, b3, fr, w4, w4, win, wtab)


HY_PIECE = 64


def _hyconv_segment(u_ref, x0_ref, bias_ref, o_ref, w_ref, wt_ref, ka_ref, kb_ref,
                    ua_scr, ub_scr, ya_scr, yb_scr, row0, nb, lb):
    for i in range(nb):
        ui = u_ref[0, row0 + i * lb:row0 + (i + 1) * lb, :]
        ua_scr[i, 0:lb, :] = _bdot(w_ref[0:lb, :], ui)
        ub_scr[i, 0:lb, :] = _bdot(w_ref[lb:2 * lb, :], ui)

    is0 = lax.broadcasted_iota(jnp.int32, (HY_PIECE, ka_ref.shape[2]), 0) == 0
    for o in range(nb):
        for r in range(0, lb, HY_PIECE):
            rows = slice(r, r + HY_PIECE)
            ya = yb = None
            for i in range(nb):
                a, b = ua_scr[i, rows, :], ub_scr[i, rows, :]
                ka, kb = ka_ref[o - i + nb - 1, rows, :], kb_ref[o - i + nb - 1, rows, :]
                if r == 0:
                    pa = jnp.where(is0, a * ka, a * ka - b * kb)
                    pb = jnp.where(is0, b * kb, a * kb + b * ka)
                else:
                    pa, pb = a * ka - b * kb, a * kb + b * ka
                ya = pa if ya is None else ya + pa
                yb = pb if yb is None else yb + pb
            scale = jnp.where(is0, 0.5 / lb, 1.0 / lb) if r == 0 else 1.0 / lb
            ya_scr[o, rows, :] = (ya * scale).astype(bf16)
            yb_scr[o, rows, :] = (yb * scale).astype(bf16)
        y = _bdot(wt_ref[:, 0:lb], ya_scr[o, 0:lb, :]) + _bdot(wt_ref[:, lb:2 * lb], yb_scr[o, 0:lb, :])
        rows = slice(row0 + o * lb, row0 + (o + 1) * lb)
        y = y + bias_ref[...] * u_ref[0, rows, :].astype(f32)
        o_ref[0, rows, :] = (x0_ref[0, rows, :].astype(f32) * y).astype(bf16)


def _hyconv_kernel(u_ref, x0_ref, bias_ref, w_ref, wt_ref, ka_ref, kb_ref, cw_ref, cwt_ref, cka_ref,
                   ckb_ref, o_ref, ua_scr, ub_scr, ya_scr, yb_scr):
    scr = (ua_scr, ub_scr, ya_scr, yb_scr)
    _hyconv_segment(u_ref, x0_ref, bias_ref, o_ref, w_ref, wt_ref, ka_ref, kb_ref, *scr,
                    row0=0, nb=HY_NB, lb=SEQ // HY_NB)
    _hyconv_segment(u_ref, x0_ref, bias_ref, o_ref, cw_ref, cwt_ref, cka_ref, ckb_ref, *scr,
                    row0=SEQ, nb=1, lb=CTX)


def _hyconv_call(u, x0, bias, lat, ctx):
    tc, lb = HY_TC, SEQ // HY_NB
    full = lambda a: pl.BlockSpec(a.shape, lambda c, b: (0,) * a.ndim)
    spec = lambda a: pl.BlockSpec(a.shape[:2] + (tc,), lambda c, b: (0, 0, c))
    seq = pl.BlockSpec((1, S, tc), lambda c, b: (b, 0, c))
    return pl.pallas_call(
        _hyconv_kernel,
        grid=(D // tc, BATCH),
        in_specs=[seq, seq, pl.BlockSpec((1, tc), lambda c, b: (0, c)),
                  full(lat[0]), full(lat[1]), spec(lat[2]), spec(lat[3]),
                  full(ctx[0]), full(ctx[1]), spec(ctx[2]), spec(ctx[3])],
        out_specs=seq,
        out_shape=jax.ShapeDtypeStruct((BATCH, S, D), bf16),
        scratch_shapes=[pltpu.VMEM((HY_NB, lb, tc), f32), pltpu.VMEM((HY_NB, lb, tc), f32),
                        pltpu.VMEM((HY_NB, lb, tc), bf16), pltpu.VMEM((HY_NB, lb, tc), bf16)],
        compiler_params=_cparams(("parallel", "parallel"), 40),
        name="hyena_conv",
    )(u, x0, bias, *lat, *ctx)


def _rope(x, cos_t, sin_t):
    n = x.shape[1]
    lane = lax.broadcasted_iota(jnp.int32, x.shape, 1)
    first = (lane & 16) == 0
    sw = jnp.where(first, pltpu.roll(x, n - 16, 1), pltpu.roll(x, 16, 1))
    return x * cos_t + sw * sin_t


assert WINDOW == ABLK
LOG2E = math.log2(math.e)
A_SPAN = ABLK + 2 * WINDOW
A_KEYS = A_SPAN + CTX
A_COLS = ATT_GROUP * ABLK


def _attn_kernel(sink_ref, q_ref, kv_ref, cos_ref, sin_ref, o_ref, q_scr, k_scr, vt_scr, s_scr, p_scr):
    g = pl.program_id(1)
    qscale = HEAD_P ** -0.5 * LOG2E

    def prep(r0, with_rope):
        q = q_ref[0, pl.ds(r0, RC), :].astype(f32) * qscale
        kv = kv_ref[0, pl.ds(r0, RC), :].astype(f32)
        k = kv
        if with_rope:
            cos_t, sin_t = cos_ref[pl.ds(r0, RC), :], sin_ref[pl.ds(r0, RC), :]
            q = _rope(q, cos_t, sin_t)
            k = _rope(kv, cos_t[:, 0:2 * HEAD_P], sin_t[:, 0:2 * HEAD_P])
        low = lax.broadcasted_iota(jnp.int32, kv.shape, 1) < HEAD_P
        k_scr[pl.ds(r0, RC), :] = jnp.where(low, k, 0.0).astype(bf16)
        for r in range(ATT_GROUP):
            slab = q[:, (r // 2) * 2 * HEAD_P:(r // 2 + 1) * 2 * HEAD_P]
            if r % 2:
                slab = pltpu.roll(slab, HEAD_P, 1)
            q_scr[r, pl.ds(r0, RC), :] = jnp.where(low, slab, 0.0).astype(bf16)
        v_t = kv.T[HEAD_P:2 * HEAD_P, :].astype(bf16)
        c0 = r0 // ABLK
        for h in range(RC // ABLK):
            vt_scr[c0 + h] = v_t[:, h * ABLK:(h + 1) * ABLK]
    _row_chunks(SEQ, lambda r0: prep(r0, True))
    for r0 in range(SEQ, S, RC):
        prep(r0, False)

    kmq = (lax.broadcasted_iota(jnp.int32, (ABLK, A_COLS), 0)
           - (lax.broadcasted_iota(jnp.int32, (ABLK, A_COLS), 1) & (ABLK - 1)))

    def block(q0, start, masks, slot):
        qstack = jnp.concatenate([q_scr[r, pl.ds(q0, ABLK), :] for r in range(ATT_GROUP)], axis=0)
        chunks = []
        if masks:
            s_loc = _bdot_nt(k_scr[pl.ds(start, len(masks) * ABLK), :], qstack)
            for c, kind in enumerate(masks):
                s_c = s_loc[c * ABLK:(c + 1) * ABLK]
                if kind is not None:
                    s_c = jnp.where(kmq >= 0 if kind == 'ge' else kmq <= 0, s_c, -jnp.inf)
                s_scr[slot, c * ABLK:(c + 1) * ABLK, :] = s_c
                chunks.append((c * ABLK, start // ABLK + c))
        n_loc = len(chunks) * ABLK
        s_scr[slot, n_loc:n_loc + CTX, :] = _bdot_nt(k_scr[SEQ:S, :], qstack)
        chunks += [(n_loc + c * ABLK, SEQ // ABLK + c) for c in range(CTX // ABLK)]

        inv = []
        for r in range(ATT_GROUP):
            cols = slice(r * ABLK, (r + 1) * ABLK)
            sink = sink_ref[g * ATT_GROUP + r] * LOG2E
            m = jnp.full((1, ABLK), sink, f32)
            for row, _ in chunks:
                m = jnp.maximum(m, jnp.max(s_scr[slot, row:row + ABLK, cols], axis=0, keepdims=True))
            den = jnp.exp2(sink - m)
            for row, _ in chunks:
                p = jnp.exp2(s_scr[slot, row:row + ABLK, cols] - m)
                den = den + jnp.sum(p, axis=0, keepdims=True)
                p_scr[slot, row:row + ABLK, cols] = p.astype(bf16)
            inv.append(1.0 / den)

        o_t = None
        for row, vc in chunks:
            part = _bdot(vt_scr[vc], p_scr[slot, row:row + ABLK, :])
            o_t = part if o_t is None else o_t + part
        o_t = jnp.concatenate([o_t[:, r * ABLK:(r + 1) * ABLK] * inv[r] for r in range(ATT_GROUP)], axis=0)
        o_ref[0, pl.ds(q0, ABLK), :] = o_t.T.astype(bf16)

    n_blk = SEQ // ABLK
    block(0, 0, (None, 'le'), 0)
    block(SEQ - ABLK, SEQ - 2 * ABLK, ('ge', None), 1)

    def lat_pair(j, carry):
        for slot in range(2):
            q0 = pl.multiple_of((1 + 2 * j + slot) * ABLK, ABLK)
            block(q0, pl.multiple_of(q0 - WINDOW, ABLK), ('ge', None, 'le'), slot)
        return carry
    lax.fori_loop(0, (n_blk - 2) // 2, lat_pair, 0, unroll=True)

    for slot, q0 in enumerate(range(SEQ, S, ABLK)):
        block(q0, 0, (), slot % 2)


def _attn_call(qkv, sinks, cos_t, sin_t):
    gw = ATT_GROUP * HEAD_P
    return pl.pallas_call(
        _attn_kernel,
        grid_spec=pltpu.PrefetchScalarGridSpec(
            num_scalar_prefetch=1,
            grid=(BATCH, KV_HEADS),
            in_specs=[pl.BlockSpec((1, S, gw), lambda b, g, s: (b, 0, g)),
                      pl.BlockSpec((1, S, 2 * HEAD_P), lambda b, g, s: (b, 0, QW // (2 * HEAD_P) + g)),
                      pl.BlockSpec((SEQ, gw), lambda b, g, s: (0, 0)),
                      pl.BlockSpec((SEQ, gw), lambda b, g, s: (0, 0))],
            out_specs=pl.BlockSpec((1, S, gw), lambda b, g, s: (b, 0, g)),
            scratch_shapes=[pltpu.VMEM((ATT_GROUP, S, 2 * HEAD_P), bf16),
                            pltpu.VMEM((S, 2 * HEAD_P), bf16),
                            pltpu.VMEM((S // ABLK, HEAD_P, ABLK), bf16),
                            pltpu.VMEM((2, A_KEYS, A_COLS), f32),
                            pltpu.VMEM((2, A_KEYS, A_COLS), bf16)]),
        out_shape=jax.ShapeDtypeStruct((BATCH, S, QW), bf16),
        compiler_params=_cparams(("parallel", "parallel"), 40),
        name="window_attn",
    )(sinks, qkv, qkv, cos_t, sin_t)


RES_TM = 768


def _outproj_kernel(x_ref, mod_ref, *rest, n_in):
    a_refs, w_refs, o_ref = rest[:n_in], rest[n_in:2 * n_in], rest[2 * n_in]
    acc = _bdot(a_refs[0][0], w_refs[0][...])
    for a_ref, w_ref in zip(a_refs[1:], w_refs[1:]):
        acc = acc + _bdot(a_ref[0], w_ref[...])
    for r0 in range(0, RES_TM, RC):
        gate = _mod_row(mod_ref[0], 2, pl.program_id(1) * RES_TM + r0)
        o_ref[0, r0:r0 + RC, :] = x_ref[0, r0:r0 + RC, :] + gate * acc[r0:r0 + RC]


def _outproj_call(xs, mods_l, acts, ws, name):
    n_in = len(acts)
    in_specs = [pl.BlockSpec((1, RES_TM, D), lambda b, i: (b, i, 0)),
                pl.BlockSpec((1, MOD_ROWS, D), lambda b, i: (b, 0, 0))]
    in_specs += [pl.BlockSpec((1, RES_TM, a.shape[2]), lambda b, i: (b, i, 0)) for a in acts]
    in_specs += [pl.BlockSpec(w.shape, lambda b, i: (0, 0)) for w in ws]
    return pl.pallas_call(
        functools.partial(_outproj_kernel, n_in=n_in),
        grid=(BATCH, S // RES_TM),
        in_specs=in_specs,
        out_specs=pl.BlockSpec((1, RES_TM, D), lambda b, i: (b, i, 0)),
        out_shape=jax.ShapeDtypeStruct((BATCH, S, D), f32),
        input_output_aliases={0: 0},
        compiler_params=_cparams(("parallel", "parallel"), 40),
        name=name,
    )(xs, mods_l, *acts, *ws)


FFN_TM = 768
FFN_TH = 256
FFN_NH = FFN_H // FFN_TH


FFN_TM_LAST = 512


def _ffn_kernel(x_ref, mod_ref, g_ref, win_ref, wo_ref, *rest, tm, final):
    gf_ref = rest[0] if final else None
    o_ref, h_scr, acc = rest[-3:]
    row0 = pl.program_id(1) * tm
    for r0 in range(0, tm, RC):
        h_scr[r0:r0 + RC, :] = _norm_mod(x_ref[0, r0:r0 + RC, :], g_ref[...], mod_ref[0],
                                         3, 4, row0 + r0).astype(bf16)
    for k in range(FFN_NH):
        gate_cols = slice(k * FFN_TH, (k + 1) * FFN_TH)
        up_cols = slice(FFN_H + k * FFN_TH, FFN_H + (k + 1) * FFN_TH)
        act = (_silu(_bdot(h_scr[...], win_ref[:, gate_cols]))
               * _bdot(h_scr[...], win_ref[:, up_cols])).astype(bf16)
        down = _bdot(act, wo_ref[gate_cols, :])
        if k == 0:
            acc[...] = down
        else:
            acc[...] += down
    for r0 in range(0, tm, RC):
        x = x_ref[0, r0:r0 + RC, :] + _mod_row(mod_ref[0], 5, row0 + r0) * acc[r0:r0 + RC, :]
        if final:
            ms = jnp.mean(x * x, axis=-1, keepdims=True)
            x = x * lax.rsqrt(ms + EPS) * gf_ref[...]
        o_ref[0, r0:r0 + RC, :] = x


def _ffn_call(xs, mods_l, g, w_in, w_out, final_g=None):
    final = final_g is not None
    tm, rows = (FFN_TM_LAST, SEQ) if final else (FFN_TM, S)
    resident = lambda a: pl.BlockSpec(a.shape, lambda b, i: (0, 0), pipeline_mode=pl.Buffered(1))
    vec = pl.BlockSpec((1, D), lambda b, i: (0, 0))
    return pl.pallas_call(
        functools.partial(_ffn_kernel, tm=tm, final=final),
        grid=(BATCH, rows // tm),
        in_specs=[pl.BlockSpec((1, tm, D), lambda b, i: (b, i, 0)),
                  pl.BlockSpec((1, MOD_ROWS, D), lambda b, i: (b, 0, 0)),
                  vec, resident(w_in), resident(w_out)] + ([vec] if final else []),
        out_specs=pl.BlockSpec((1, tm, D), lambda b, i: (b, i, 0)),
        out_shape=jax.ShapeDtypeStruct((BATCH, rows, D), f32),
        scratch_shapes=[pltpu.VMEM((tm, D), bf16), pltpu.VMEM((tm, D), f32)],
        input_output_aliases={} if final else {0: 0},
        compiler_params=_cparams(("parallel", "parallel"), 52),
        name="swiglu_ffn_final" if final else "swiglu_ffn",
    )(xs, mods_l, g, w_in, w_out, *([final_g] if final else []))


def _dft_table(lb):
    n = 2 * lb
    f = jnp.arange(lb, dtype=jnp.int32)[:, None]
    t = jnp.arange(lb, dtype=jnp.int32)[None, :]
    ang = (2.0 * math.pi / n) * ((f * t) % n).astype(f32)
    nyq = jnp.where((t & 1) == 0, 1.0, -1.0).astype(f32)
    sin_rows = jnp.where(f == 0, nyq, jnp.sin(ang))
    return jnp.concatenate([jnp.cos(ang), sin_rows], axis=0)


def _hy_features(seg):
    t = jnp.abs(jnp.arange(2 * seg) - seg).astype(f32)
    t_unit = t / (seg - 1)
    ang = 2.0 * math.pi * t / seg
    fb = jnp.linspace(1e-4, HY_BANDS - 1, HY_BANDS, dtype=f32)
    z = jnp.concatenate([t_unit[:, None], jnp.cos(ang[:, None] * fb), -jnp.sin(ang[:, None] * fb)], axis=-1)
    z = jnp.pad(z, ((0, 0), (0, 128 - HY_EMB)))
    deltas = jnp.linspace(math.log(1e-2) / 1.5, math.log(1e-2) / 0.3, D, dtype=f32)
    win = jnp.exp(-t_unit[:, None] * jnp.abs(deltas)[None, :])
    return z, win


def _rope_tables():
    nf = HEAD_P // 4
    inv = ROPE_BASE ** (-jnp.arange(nf, dtype=f32) / nf)
    pos = jnp.arange(SEQ)
    row = (pos // GRID_W).astype(f32)
    col = (pos % GRID_W).astype(f32)
    ar = row[:, None] * inv[None, :]
    ac = col[:, None] * inv[None, :]
    ang = jnp.concatenate([ar, ar, ac, ac], axis=-1)
    sign = jnp.tile(jnp.concatenate([-jnp.ones(nf, f32), jnp.ones(nf, f32)]), 2)
    cos_t = jnp.tile(jnp.cos(ang), (1, ATT_GROUP))
    sin_t = jnp.tile(jnp.sin(ang) * sign[None, :], (1, ATT_GROUP))
    return cos_t, sin_t


def _head_expand(lane_off):
    r = jnp.arange(DT_LANES)[:, None]
    c = jnp.arange(D)[None, :]
    return (c // HEAD_P + lane_off == r).astype(bf16)


def _pad_rows(a, rows):
    return jnp.pad(a, ((0, rows - a.shape[0]), (0, 0)))


def _pad_lanes(a, lanes):
    return jnp.pad(a, ((0, 0), (0, lanes - a.shape[1])))


def kernel(x, c, ctx, c_ctx, ada_w, ada_b, norm1_g, norm2_g, hy_in_w, ssd_conv_w, ssd_conv_b, ssd_dt_bias, ssd_a_log, ssd_d, ssd_norm_g, hy_conv_w, hy_conv_b, hy_w1, hy_b1, hy_w2, hy_b2, hy_w3, hy_b3, hy_w4, hy_freq, hy_bias, hy_out_w, attn_qkv_w, attn_sinks, attn_out_w, ffn_w_in, ffn_w_out, final_g):
    xs = jnp.concatenate([x, ctx], axis=1)

    cc = jnp.concatenate([c, c_ctx[None, :], jnp.zeros((MOD_ROWS - BATCH - 1, D), f32)], axis=0)
    modv = _mod_call(cc, ada_w, ada_b)
    lat = modv[:, :BATCH].reshape(DEPTH, BATCH, N_MOD, D)
    cmod = jnp.broadcast_to(modv[:, BATCH].reshape(DEPTH, 1, N_MOD, D), (DEPTH, BATCH, N_MOD, D))
    pad = jnp.zeros((DEPTH, BATCH, CTX_MOD - N_MOD, D), f32)
    mods = jnp.concatenate([lat, pad, cmod, pad], axis=2)

    tabs = {}
    for seg, nb in ((SEQ, HY_NB), (CTX, 1)):
        w = _dft_table(seg // nb)
        tabs[seg] = (w.astype(bf16), w.T.astype(bf16)) + _hy_features(seg)
    e_f, e_b = _head_expand(0), _head_expand(HEADS)
    cos_t, sin_t = _rope_tables()

    for layer in range(DEPTH):
        j = layer // 2
        mods_l = mods[layer]
        g1 = norm1_g[layer][None, :]
        if layer % 2 == 0:
            w_in = hy_in_w[j]
            w_zx = w_in[:, :COL_DT].astype(bf16)
            wdt = _pad_lanes(w_in[:, COL_DT:COL_HY], DT_LANES).astype(bf16)
            dtb = _pad_lanes(ssd_dt_bias[j].reshape(1, 2 * HEADS), DT_LANES)
            cw = jnp.concatenate([jnp.zeros((4, D), f32),
                                  jnp.concatenate([ssd_conv_w[j], ssd_conv_b[j][None, :]], axis=0)], axis=1)
            zx, dt = _inproj_ssd_call(xs, mods_l, g1, w_zx, _pad_rows(cw, 8), wdt, dtb)
            cw_hy = _pad_rows(jnp.concatenate([hy_conv_w[j], hy_conv_b[j][None, :]], axis=0), 8)
            x0, u = _inproj_hy_call(xs, mods_l, g1, w_in[:, COL_HY:].astype(bf16), cw_hy)

            alog = _pad_lanes(ssd_a_log[j].reshape(1, 2 * HEADS), DT_LANES)
            dskip = jnp.repeat(ssd_d[j], HEAD_P)[None, :]
            yn = _ssd_call(zx, dt, alog, dskip, ssd_norm_g[j][None, :], e_f, e_b)

            segs = []
            for seg, nb in ((SEQ, HY_NB), (CTX, 1)):
                wtab, wtab_t, zt, win = tabs[seg]
                ka, kb = _hyfilter_call(zt, _pad_rows(hy_w1[j], 128), hy_b1[j][None, :], hy_w2[j],
                                        hy_b2[j][None, :], hy_w3[j], hy_b3[j][None, :],
                                        _pad_rows(hy_freq[j], 8), hy_w4[j], win, wtab, seg, nb)
                segs.append((wtab, wtab_t, ka, kb))
            gh = _hyconv_call(u, x0, hy_bias[j][None, :], *segs)
            w_out = hy_out_w[j].astype(bf16)
            xs = _outproj_call(xs, mods_l, [yn, gh], [w_out[:D], w_out[D:]], "outproj_even")
        else:
            wq = attn_qkv_w[j]
            wk = wq[:, QW:QW + KVW].reshape(D, KV_HEADS, HEAD_P)
            wv = wq[:, QW + KVW:].reshape(D, KV_HEADS, HEAD_P)
            w_kv = jnp.concatenate([wk, wv], axis=2).reshape(D, 2 * KVW)
            w_qkv = jnp.concatenate([wq[:, :QW], w_kv], axis=1).astype(bf16)
            qkv = _nmm_call(xs, mods_l, g1, w_qkv, 512, "qkv_proj")
            o = _attn_call(qkv, attn_sinks[j], cos_t, sin_t)
            xs = _outproj_call(xs, mods_l, [o], [attn_out_w[j].astype(bf16)], "outproj_attn")
        xs = _ffn_call(xs, mods_l, norm2_g[layer][None, :], ffn_w_in[layer].astype(bf16),
                       ffn_w_out[layer].astype(bf16),
                       final_g=final_g[None, :] if layer == DEPTH - 1 else None)
    return xs
```

```python
import functools
import math

import jax
import jax.numpy as jnp
from jax import lax
from jax.experimental import pallas as pl
from jax.experimental.pallas import tpu as pltpu

f32 = jnp.float32
bf16 = jnp.bfloat16

D = 1024
BATCH = 8
SEQ = 2048
DEPTH = 4
GRID_W = 64
CTX = 256
S = SEQ + CTX
EPS = 1e-6
N_MOD = 6
MOD_ROWS = 16
CTX_MOD = 8

HEADS = 16
HEAD_P = 64
STATE = 128
GROUPS = 2
CHUNK = 128
XBC = D + 2 * GROUPS * STATE
ZX = D + XBC
DT_LANES = 128
HY_BANDS = 16
HY_EMB = 1 + 2 * HY_BANDS
HY_FFN = 64
COL_XBC = D
COL_DT = COL_XBC + XBC
COL_HY = COL_DT + 2 * HEADS

Q_HEADS = 16
KV_HEADS = 4
ATT_GROUP = Q_HEADS // KV_HEADS
QW = Q_HEADS * HEAD_P
KVW = KV_HEADS * HEAD_P
WINDOW = 128
ABLK = 128
ROPE_BASE = 10000.0
FFN_H = -(-8 * D // (3 * 256)) * 256

RC = 256
NMM_RC = 768
VMEM_CAP = 56 * 1024 * 1024


def _cparams(sem, vmem_mb):
    return pltpu.CompilerParams(dimension_semantics=sem,
                                vmem_limit_bytes=min(int(vmem_mb * 1024 * 1024), VMEM_CAP))


def _silu(v):
    return v * jax.nn.sigmoid(v)


def _softplus(v):
    return jnp.maximum(v, 0.0) + jnp.log1p(jnp.exp(-jnp.abs(v)))


def _bdot(a, b):
    return jnp.dot(a, b, preferred_element_type=f32)


def _bdot_nt(a, b):
    return lax.dot_general(a, b, (((1,), (1,)), ((), ())), preferred_element_type=f32)


def _row_chunks(n_rows, body):
    def step(c, carry):
        body(pl.multiple_of(c * RC, RC))
        return carry
    lax.fori_loop(0, n_rows // RC, step, 0)


def _norm_mod(x, g, mod, shift_row, scale_row, row0):
    ms = jnp.mean(x * x, axis=-1, keepdims=True)
    scale = _mod_row(mod, scale_row, row0)
    return x * lax.rsqrt(ms + EPS) * (g * (1.0 + scale)) + _mod_row(mod, shift_row, row0)


def _mod_row(mod, k, row0):
    return jnp.where(row0 >= SEQ, mod[CTX_MOD + k:CTX_MOD + k + 1], mod[k:k + 1])


def _norm_mod_to_scratch(x_ref, g_ref, mod_ref, h_scr):
    def body(r0):
        h_scr[pl.ds(r0, RC), :] = _norm_mod(x_ref[0, pl.ds(r0, RC), :], g_ref[...], mod_ref[0],
                                            0, 1, r0).astype(bf16)
    _row_chunks(S, body)


CONV_ROWS = 64


def _conv3_rows(p_scr, r0, cw):
    n = CONV_ROWS
    v = p_scr[r0:r0 + n, :]
    zero = jnp.zeros((1, v.shape[1]), f32)
    up = zero if r0 in (0, SEQ) else p_scr[r0 - 8:r0, :][7:8]
    dn = zero if r0 + n in (SEQ, S) else p_scr[r0 + n:r0 + n + 8, :][0:1]
    sub = lax.broadcasted_iota(jnp.int32, (8, v.shape[1]), 0)
    prev = pltpu.roll(v, 1, 0)
    prev = jnp.concatenate([jnp.where(sub == 0, up, prev[0:8]), prev[8:]], axis=0)
    nxt = pltpu.roll(v, n - 1, 0)
    nxt = jnp.concatenate([nxt[:n - 8], jnp.where(sub == 7, dn, nxt[n - 8:])], axis=0)
    return prev * cw[0:1] + v * cw[1:2] + nxt * cw[2:3] + cw[3:4]


def _project_then_conv(mm, piece):
    done = 0
    for r0 in range(0, S, NMM_RC):
        while done + CONV_ROWS < r0:
            piece(done)
            done += CONV_ROWS
        mm(r0)
    while done < S:
        piece(done)
        done += CONV_ROWS


def _mod_kernel(c_ref, w_ref, b_ref, o_ref):
    sc = _silu(c_ref[...]).astype(bf16)
    o_ref[0] = _bdot(sc, w_ref[0].astype(bf16)) + b_ref[0]


def _mod_call(cc, ada_w, ada_b):
    tn = 1024
    return pl.pallas_call(
        _mod_kernel,
        grid=(DEPTH, N_MOD * D // tn),
        in_specs=[pl.BlockSpec((MOD_ROWS, D), lambda l, j: (0, 0)),
                  pl.BlockSpec((1, D, tn), lambda l, j: (l, 0, j)),
                  pl.BlockSpec((1, 1, tn), lambda l, j: (l, 0, j))],
        out_specs=pl.BlockSpec((1, MOD_ROWS, tn), lambda l, j: (l, 0, j)),
        out_shape=jax.ShapeDtypeStruct((DEPTH, MOD_ROWS, N_MOD * D), f32),
        compiler_params=_cparams(("parallel", "parallel"), 24),
        name="adaln_mod",
    )(cc, ada_w, ada_b.reshape(DEPTH, 1, N_MOD * D))


def _nmm_kernel(x_ref, mod_ref, g_ref, w_ref, o_ref, h_scr):
    @pl.when(pl.program_id(1) == 0)
    def _():
        _norm_mod_to_scratch(x_ref, g_ref, mod_ref, h_scr)

    def body(c, carry):
        r0 = pl.multiple_of(c * NMM_RC, NMM_RC)
        o_ref[0, pl.ds(r0, NMM_RC), :] = _bdot(h_scr[pl.ds(r0, NMM_RC), :], w_ref[...]).astype(o_ref.dtype)
        return carry
    lax.fori_loop(0, S // NMM_RC, body, 0)


def _nmm_call(xs, mods_l, g, w, tn, name):
    n = w.shape[1]
    return pl.pallas_call(
        _nmm_kernel,
        grid=(BATCH, n // tn),
        in_specs=[pl.BlockSpec((1, S, D), lambda b, j: (b, 0, 0)),
                  pl.BlockSpec((1, MOD_ROWS, D), lambda b, j: (b, 0, 0)),
                  pl.BlockSpec((1, D), lambda b, j: (0, 0)),
                  pl.BlockSpec((D, tn), lambda b, j: (0, j))],
        out_specs=pl.BlockSpec((1, S, tn), lambda b, j: (b, 0, j)),
        out_shape=jax.ShapeDtypeStruct((BATCH, S, n), bf16),
        scratch_shapes=[pltpu.VMEM((S, D), bf16)],
        compiler_params=_cparams(("parallel", "arbitrary"), 40),
        name=name,
    )(xs, mods_l, g, w)


ZX_TN = 512
NZ_T = D // ZX_TN


def _inproj_ssd_kernel(x_ref, mod_ref, g_ref, w_ref, cw_ref, wdt_ref, dtb_ref, zx_ref, dt_ref,
                       h_scr, p_scr):
    j = pl.program_id(1)

    @pl.when(j == 0)
    def _():
        _norm_mod_to_scratch(x_ref, g_ref, mod_ref, h_scr)

        def dt_body(c, carry):
            r0 = pl.multiple_of(c * NMM_RC, NMM_RC)
            sp = _softplus(_bdot(h_scr[pl.ds(r0, NMM_RC), :], wdt_ref[...]) + dtb_ref[...])
            lane = lax.broadcasted_iota(jnp.int32, sp.shape, 1)
            dt_ref[0, pl.ds(r0, NMM_RC), :] = jnp.where(lane < 2 * HEADS, sp, 0.0)
            return carry
        lax.fori_loop(0, S // NMM_RC, dt_body, 0)

    @pl.when(j < NZ_T)
    def _():
        def body(c, carry):
            r0 = pl.multiple_of(c * NMM_RC, NMM_RC)
            zx_ref[0, pl.ds(r0, NMM_RC), :] = _bdot(h_scr[pl.ds(r0, NMM_RC), :], w_ref[...]).astype(bf16)
            return carry
        lax.fori_loop(0, S // NMM_RC, body, 0)

    @pl.when(j >= NZ_T)
    def _():
        def mm(r0):
            p_scr[r0:r0 + NMM_RC, :] = _bdot(h_scr[r0:r0 + NMM_RC, :], w_ref[...])

        def piece(a):
            zx_ref[0, a:a + CONV_ROWS, :] = _silu(_conv3_rows(p_scr, a, cw_ref[...])).astype(bf16)
        _project_then_conv(mm, piece)


def _inproj_ssd_call(xs, mods_l, g, w_zx, cw, wdt, dtb):
    return pl.pallas_call(
        _inproj_ssd_kernel,
        grid=(BATCH, ZX // ZX_TN),
        in_specs=[pl.BlockSpec((1, S, D), lambda b, j: (b, 0, 0)),
                  pl.BlockSpec((1, MOD_ROWS, D), lambda b, j: (b, 0, 0)),
                  pl.BlockSpec((1, D), lambda b, j: (0, 0)),
                  pl.BlockSpec((D, ZX_TN), lambda b, j: (0, j)),
                  pl.BlockSpec((8, ZX_TN), lambda b, j: (0, j)),
                  pl.BlockSpec((D, DT_LANES), lambda b, j: (0, 0)),
                  pl.BlockSpec((1, DT_LANES), lambda b, j: (0, 0))],
        out_specs=[pl.BlockSpec((1, S, ZX_TN), lambda b, j: (b, 0, j)),
                   pl.BlockSpec((1, S, DT_LANES), lambda b, j: (b, 0, 0))],
        out_shape=[jax.ShapeDtypeStruct((BATCH, S, ZX), bf16),
                   jax.ShapeDtypeStruct((BATCH, S, DT_LANES), f32)],
        scratch_shapes=[pltpu.VMEM((S, D), bf16), pltpu.VMEM((S, ZX_TN), f32)],
        compiler_params=_cparams(("parallel", "arbitrary"), 48),
        name="inproj_ssd",
    )(xs, mods_l, g, w_zx, cw, wdt, dtb)


HY_TN = 256


def _inproj_hy_kernel(x_ref, mod_ref, g_ref, w0_ref, w1_ref, w2_ref, c0_ref, c1_ref, c2_ref,
                      x0_ref, u_ref, h_scr, p0_scr, p1_scr, p2_scr):
    @pl.when(pl.program_id(1) == 0)
    def _():
        _norm_mod_to_scratch(x_ref, g_ref, mod_ref, h_scr)

    def mm_x0(r0):
        p0_scr[r0:r0 + NMM_RC, :] = _bdot(h_scr[r0:r0 + NMM_RC, :], w0_ref[...])

    def piece_x0(a):
        x0_ref[0, a:a + CONV_ROWS, :] = _conv3_rows(p0_scr, a, c0_ref[...]).astype(bf16)
    _project_then_conv(mm_x0, piece_x0)

    def mm_u(r0):
        p1_scr[r0:r0 + NMM_RC, :] = _bdot(h_scr[r0:r0 + NMM_RC, :], w1_ref[...])
        p2_scr[r0:r0 + NMM_RC, :] = _bdot(h_scr[r0:r0 + NMM_RC, :], w2_ref[...])

    def piece_u(a):
        x1 = _conv3_rows(p1_scr, a, c1_ref[...])
        v = _conv3_rows(p2_scr, a, c2_ref[...])
        u_ref[0, a:a + CONV_ROWS, :] = (x1 * v).astype(bf16)
    _project_then_conv(mm_u, piece_u)


def _inproj_hy_call(xs, mods_l, g, w_hy, cw_hy):
    nt = D // HY_TN
    wspec = lambda k: pl.BlockSpec((D, HY_TN), lambda b, j, k=k: (0, j + k * nt))
    cspec = lambda k: pl.BlockSpec((8, HY_TN), lambda b, j, k=k: (0, j + k * nt))
    return pl.pallas_call(
        _inproj_hy_kernel,
        grid=(BATCH, nt),
        in_specs=[pl.BlockSpec((1, S, D), lambda b, j: (b, 0, 0)),
                  pl.BlockSpec((1, MOD_ROWS, D), lambda b, j: (b, 0, 0)),
                  pl.BlockSpec((1, D), lambda b, j: (0, 0)),
                  wspec(0), wspec(1), wspec(2), cspec(0), cspec(1), cspec(2)],
        out_specs=[pl.BlockSpec((1, S, HY_TN), lambda b, j: (b, 0, j)),
                   pl.BlockSpec((1, S, HY_TN), lambda b, j: (b, 0, j))],
        out_shape=[jax.ShapeDtypeStruct((BATCH, S, D), bf16),
                   jax.ShapeDtypeStruct((BATCH, S, D), bf16)],
        scratch_shapes=[pltpu.VMEM((S, D), bf16)] + [pltpu.VMEM((S, HY_TN), f32)] * 3,
        compiler_params=_cparams(("parallel", "arbitrary"), 48),
        name="inproj_hyena",
    )(xs, mods_l, g, w_hy, w_hy, w_hy, cw_hy, cw_hy, cw_hy)


N_CHUNK = S // CHUNK
LAT_CHUNKS = SEQ // CHUNK


def _split3_dot(t, a):
    a1 = a.astype(bf16)
    r1 = a - a1.astype(f32)
    a2 = r1.astype(bf16)
    a3 = (r1 - a2.astype(f32)).astype(bf16)
    return _bdot(t, a1) + _bdot(t, a2) + _bdot(t, a3)


def _ssd_chunk(zx_ref, dt_ref, aneg, e_ref, y_scr, st_scr, r0, backward):
    lane_off = HEADS if backward else 0
    x = zx_ref[0, pl.ds(r0, CHUNK), D:2 * D].astype(f32)
    bm = zx_ref[0, pl.ds(r0, CHUNK), 2 * D:2 * D + GROUPS * STATE]
    cm = zx_ref[0, pl.ds(r0, CHUNK), 2 * D + GROUPS * STATE:ZX]
    dt = dt_ref[0, pl.ds(r0, CHUNK), :]
    a = dt * aneg

    ri = lax.broadcasted_iota(jnp.int32, (CHUNK, CHUNK), 0)
    ci = lax.broadcasted_iota(jnp.int32, (CHUNK, CHUNK), 1)
    causal = (ci >= ri) if backward else (ci <= ri)
    tmat = jnp.where(causal, 1.0, 0.0).astype(bf16)
    acum = _split3_dot(tmat, a)
    acum_t = acum.T
    tot = acum[0:1] if backward else acum[CHUNK - 1:CHUNK]
    dte = jnp.exp(tot - acum)
    eac = jnp.exp(acum)
    cdec = jnp.broadcast_to(jnp.exp(tot), (16, DT_LANES))

    e = e_ref[...]
    dt_x = _bdot(dt.astype(bf16), e)
    dte_x = _bdot(dte.astype(bf16), e)
    eac_x = _bdot(eac.astype(bf16), e)
    cdec_x = _bdot(cdec.astype(bf16), e)[0:1]

    xdt = x * dt_x
    xdt_b = xdt.astype(bf16)
    xs_b = (xdt * dte_x).astype(bf16)
    lane = lax.broadcasted_iota(jnp.int32, (CHUNK, 2 * HEAD_P), 1)
    low = lane < HEAD_P
    hg = HEADS // GROUPS
    gw = hg * HEAD_P
    for g in range(GROUPS):
        bg = bm[:, g * STATE:(g + 1) * STATE]
        cg = cm[:, g * STATE:(g + 1) * STATE]
        cb = _bdot_nt(cg, bg)
        hin = st_scr[:, g * gw:(g + 1) * gw]
        yoff = _bdot(cg, hin.astype(bf16)) * eac_x[:, g * gw:(g + 1) * gw]
        for hp in range(hg // 2):
            h0 = g * hg + 2 * hp
            ms = []
            for h in (h0, h0 + 1):
                l0 = lane_off + h
                seg = acum[:, l0:l0 + 1] - acum_t[l0:l0 + 1, :]
                ms.append(cb * jnp.exp(jnp.where(causal, seg, -jnp.inf)))
            lhs = jnp.concatenate(ms, axis=1).astype(bf16)
            xp = xdt[:, h0 * HEAD_P:(h0 + 2) * HEAD_P]
            rhs = jnp.concatenate([jnp.where(low, xp, 0.0), jnp.where(low, 0.0, xp)],
                                  axis=0).astype(bf16)
            yd = _bdot(lhs, rhs) + yoff[:, hp * 2 * HEAD_P:(hp + 1) * 2 * HEAD_P]
            cols = slice(h0 * HEAD_P, (h0 + 2) * HEAD_P)
            y_scr[pl.ds(r0, CHUNK), cols] = y_scr[pl.ds(r0, CHUNK), cols] + yd
        bg_t = bg.astype(f32).T.astype(bf16)
        st_scr[:, g * gw:(g + 1) * gw] = (hin * cdec_x[:, g * gw:(g + 1) * gw]
                                         + _bdot(bg_t, xs_b[:, g * gw:(g + 1) * gw]))
    del xdt_b


def _ssd_kernel(zx_ref, dt_ref, alog_ref, dskip_ref, ng_ref, ef_ref, eb_ref, o_ref,
                y_scr, sf_scr, sb_scr):
    aneg = -jnp.exp(alog_ref[...])
    sf_scr[...] = jnp.zeros_like(sf_scr)
    sb_scr[...] = jnp.zeros_like(sb_scr)

    def init(c, carry):
        r0 = pl.multiple_of(c * CHUNK, CHUNK)
        y_scr[pl.ds(r0, CHUNK), :] = dskip_ref[...] * zx_ref[0, pl.ds(r0, CHUNK), D:2 * D].astype(f32)
        return carry
    lax.fori_loop(0, N_CHUNK, init, 0)

    def step(t, carry):
        cf = jnp.where(t < N_CHUNK - LAT_CHUNKS, LAT_CHUNKS + t, t - (N_CHUNK - LAT_CHUNKS))
        cbk = N_CHUNK - 1 - t
        _ssd_chunk(zx_ref, dt_ref, aneg, ef_ref, y_scr, sf_scr, pl.multiple_of(cf * CHUNK, CHUNK), False)
        _ssd_chunk(zx_ref, dt_ref, aneg, eb_ref, y_scr, sb_scr, pl.multiple_of(cbk * CHUNK, CHUNK), True)
        return carry
    lax.fori_loop(0, N_CHUNK, step, 0, unroll=2)

    def fin(c, carry):
        r0 = pl.multiple_of(c * CHUNK, CHUNK)
        t = y_scr[pl.ds(r0, CHUNK), :] * _silu(zx_ref[0, pl.ds(r0, CHUNK), 0:D].astype(f32))
        ms = jnp.mean(t * t, axis=-1, keepdims=True)
        o_ref[0, pl.ds(r0, CHUNK), :] = (t * lax.rsqrt(ms + EPS) * ng_ref[...]).astype(bf16)
        return carry
    lax.fori_loop(0, N_CHUNK, fin, 0)


def _ssd_call(zx, dt, alog, dskip, ng, e_f, e_b):
    return pl.pallas_call(
        _ssd_kernel,
        grid=(BATCH,),
        in_specs=[pl.BlockSpec((1, S, ZX), lambda b: (b, 0, 0)),
                  pl.BlockSpec((1, S, DT_LANES), lambda b: (b, 0, 0)),
                  pl.BlockSpec((1, DT_LANES), lambda b: (0, 0)),
                  pl.BlockSpec((1, D), lambda b: (0, 0)),
                  pl.BlockSpec((1, D), lambda b: (0, 0)),
                  pl.BlockSpec((DT_LANES, D), lambda b: (0, 0)),
                  pl.BlockSpec((DT_LANES, D), lambda b: (0, 0))],
        out_specs=pl.BlockSpec((1, S, D), lambda b: (b, 0, 0)),
        out_shape=jax.ShapeDtypeStruct((BATCH, S, D), bf16),
        scratch_shapes=[pltpu.VMEM((S, D), f32),
                        pltpu.VMEM((STATE, D), f32),
                        pltpu.VMEM((STATE, D), f32)],
        compiler_params=_cparams(("parallel",), 56),
        name="ssd_scan",
    )(zx, dt, alog, dskip, ng, e_f, e_b)


def _hp_dot(a, b):
    return jnp.dot(a, b, precision=lax.Precision.HIGHEST, preferred_element_type=f32)


def _hyfilter_kernel(z_ref, w1_ref, b1_ref, w2_ref, b2_ref, w3_ref, b3_ref, fr_ref, w4f_ref, w4b_ref,
                     win_ref, w_ref, ka_ref, kb_ref, h_scr, tap_scr, tb_scr, pp_scr, pq_scr, *, seg, nb):
    lb = seg // nb
    tc = tap_scr.shape[1]

    @pl.when(pl.program_id(0) == 0)
    def _():
        fr = fr_ref[...]

        def hidden(c, carry):
            r0 = pl.multiple_of(c * RC, RC)
            h = jnp.sin(fr[0:1] * (_hp_dot(z_ref[pl.ds(r0, RC), :], w1_ref[...]) + b1_ref[...]))
            h = jnp.sin(fr[1:2] * (_hp_dot(h, w2_ref[...]) + b2_ref[...]))
            h_scr[pl.ds(r0, RC), :] = jnp.sin(fr[2:3] * (_hp_dot(h, w3_ref[...]) + b3_ref[...]))
            return carry
        lax.fori_loop(0, 2 * seg // RC, hidden, 0)

    def taps(w4_ref):
        def body(c, ssq):
            r0 = pl.multiple_of(c * RC, RC)
            t = _hp_dot(h_scr[pl.ds(r0, RC), :], w4_ref[...]) * win_ref[pl.ds(r0, RC), :]
            row = r0 + lax.broadcasted_iota(jnp.int32, t.shape, 0)
            t = jnp.where(row == 0, 0.0, t)
            tap_scr[pl.ds(r0, RC), :] = t
            return ssq + jnp.sum(t * t, axis=0, keepdims=True)
        return body
    ssq = lax.fori_loop(0, seg // RC, taps(w4b_ref), jnp.zeros((1, tc), f32))
    nrm = lax.rsqrt(lax.fori_loop(seg // RC, 2 * seg // RC, taps(w4f_ref), ssq) + 1e-6)

    def scale(c, carry):
        r0 = pl.multiple_of(c * RC, RC)
        tb_scr[pl.ds(r0, RC), :] = (tap_scr[pl.ds(r0, RC), :] * nrm).astype(bf16)
        return carry
    lax.fori_loop(0, 2 * seg // RC, scale, 0)

    row = lax.broadcasted_iota(jnp.int32, (lb, tc), 0)
    sgn = jnp.where((row & 1) == 0, 1.0, -1.0)
    for s in range(2 * nb):
        blk = tb_scr[s * lb:(s + 1) * lb, :]
        p = _bdot(w_ref[0:lb, :], blk)
        q = _bdot(w_ref[lb:2 * lb, :], blk)
        if s > 0:
            t0 = tb_scr[(s - 1) * lb:(s - 1) * lb + 16, :][0:1].astype(f32)
            ka_ref[s - 1] = p + sgn * (pp_scr[...] - t0)
            kb_ref[s - 1] = jnp.where(row == 0, q + pq_scr[...] - t0, q + sgn * pq_scr[...])
        pp_scr[...] = p
        pq_scr[...] = q


HY_TC = 256
HY_NB = 4


def _hyfilter_call(zt, w1, b1, w2, b2, w3, b3, fr, w4, win, wtab, seg, nb):
    tc, lb = HY_TC, seg // nb
    nct = D // tc
    const = lambda shape: pl.BlockSpec(shape, lambda c: (0, 0))
    return pl.pallas_call(
        functools.partial(_hyfilter_kernel, seg=seg, nb=nb),
        grid=(nct,),
        in_specs=[const((2 * seg, 128)), const((128, HY_FFN)), const((1, HY_FFN)),
                  const((HY_FFN, HY_FFN)), const((1, HY_FFN)),
                  const((HY_FFN, HY_FFN)), const((1, HY_FFN)), const((8, HY_FFN)),
                  pl.BlockSpec((HY_FFN, tc), lambda c: (0, c)),
                  pl.BlockSpec((HY_FFN, tc), lambda c: (0, c + nct)),
                  pl.BlockSpec((2 * seg, tc), lambda c: (0, c)),
                  const((2 * lb, lb))],
        out_specs=[pl.BlockSpec((2 * nb - 1, lb, tc), lambda c: (0, 0, c)),
                   pl.BlockSpec((2 * nb - 1, lb, tc), lambda c: (0, 0, c))],
        out_shape=[jax.ShapeDtypeStruct((2 * nb - 1, lb, D), f32)] * 2,
        scratch_shapes=[pltpu.VMEM((2 * seg, HY_FFN), f32),
                        pltpu.VMEM((2 * seg, tc), f32), pltpu.VMEM((2 * seg, tc), bf16),
                        pltpu.VMEM((lb, tc), f32), pltpu.VMEM((lb, tc), f32)],
        compiler_params=_cparams(("arbitrary",), 40),
        name=f"hyena_filter_{seg}",
    )(zt, w1, b1, w2, b2, w3, b3, fr, w4, w4, win, wtab)


HY_PIECE = 64


def _hyconv_segment(u_ref, x0_ref, bias_ref, o_ref, w_ref, wt_ref, ka_ref, kb_ref,
                    ua_scr, ub_scr, ya_scr, yb_scr, row0, nb, lb):
    for i in range(nb):
        ui = u_ref[0, row0 + i * lb:row0 + (i + 1) * lb, :]
        ua_scr[i, 0:lb, :] = _bdot(w_ref[0:lb, :], ui)
        ub_scr[i, 0:lb, :] = _bdot(w_ref[lb:2 * lb, :], ui)

    is0 = lax.broadcasted_iota(jnp.int32, (HY_PIECE, ka_ref.shape[2]), 0) == 0
    for o in range(nb):
        for r in range(0, lb, HY_PIECE):
            rows = slice(r, r + HY_PIECE)
            ya = yb = None
            for i in range(nb):
                a, b = ua_scr[i, rows, :], ub_scr[i, rows, :]
                ka, kb = ka_ref[o - i + nb - 1, rows, :], kb_ref[o - i + nb - 1, rows, :]
                if r == 0:
                    pa = jnp.where(is0, a * ka, a * ka - b * kb)
                    pb = jnp.where(is0, b * kb, a * kb + b * ka)
                else:
                    pa, pb = a * ka - b * kb, a * kb + b * ka
                ya = pa if ya is None else ya + pa
                yb = pb if yb is None else yb + pb
            scale = jnp.where(is0, 0.5 / lb, 1.0 / lb) if r == 0 else 1.0 / lb
            ya_scr[o, rows, :] = (ya * scale).astype(bf16)
            yb_scr[o, rows, :] = (yb * scale).astype(bf16)
        y = _bdot(wt_ref[:, 0:lb], ya_scr[o, 0:lb, :]) + _bdot(wt_ref[:, lb:2 * lb], yb_scr[o, 0:lb, :])
        rows = slice(row0 + o * lb, row0 + (o + 1) * lb)
        y = y + bias_ref[...] * u_ref[0, rows, :].astype(f32)
        o_ref[0, rows, :] = (x0_ref[0, rows, :].astype(f32) * y).astype(bf16)


def _hyconv_kernel(u_ref, x0_ref, bias_ref, w_ref, wt_ref, ka_ref, kb_ref, cw_ref, cwt_ref, cka_ref,
                   ckb_ref, o_ref, ua_scr, ub_scr, ya_scr, yb_scr):
    scr = (ua_scr, ub_scr, ya_scr, yb_scr)
    _hyconv_segment(u_ref, x0_ref, bias_ref, o_ref, w_ref, wt_ref, ka_ref, kb_ref, *scr,
                    row0=0, nb=HY_NB, lb=SEQ // HY_NB)
    _hyconv_segment(u_ref, x0_ref, bias_ref, o_ref, cw_ref, cwt_ref, cka_ref, ckb_ref, *scr,
                    row0=SEQ, nb=1, lb=CTX)


def _hyconv_call(u, x0, bias, lat, ctx):
    tc, lb = HY_TC, SEQ // HY_NB
    full = lambda a: pl.BlockSpec(a.shape, lambda c, b: (0,) * a.ndim)
    spec = lambda a: pl.BlockSpec(a.shape[:2] + (tc,), lambda c, b: (0, 0, c))
    seq = pl.BlockSpec((1, S, tc), lambda c, b: (b, 0, c))
    return pl.pallas_call(
        _hyconv_kernel,
        grid=(D // tc, BATCH),
        in_specs=[seq, seq, pl.BlockSpec((1, tc), lambda c, b: (0, c)),
                  full(lat[0]), full(lat[1]), spec(lat[2]), spec(lat[3]),
                  full(ctx[0]), full(ctx[1]), spec(ctx[2]), spec(ctx[3])],
        out_specs=seq,
        out_shape=jax.ShapeDtypeStruct((BATCH, S, D), bf16),
        scratch_shapes=[pltpu.VMEM((HY_NB, lb, tc), f32), pltpu.VMEM((HY_NB, lb, tc), f32),
                        pltpu.VMEM((HY_NB, lb, tc), bf16), pltpu.VMEM((HY_NB, lb, tc), bf16)],
        compiler_params=_cparams(("parallel", "parallel"), 40),
        name="hyena_conv",
    )(u, x0, bias, *lat, *ctx)


def _rope(x, cos_t, sin_t):
    n = x.shape[1]
    lane = lax.broadcasted_iota(jnp.int32, x.shape, 1)
    first = (lane & 16) == 0
    sw = jnp.where(first, pltpu.roll(x, n - 16, 1), pltpu.roll(x, 16, 1))
    return x * cos_t + sw * sin_t


assert WINDOW == ABLK
LOG2E = math.log2(math.e)
A_SPAN = ABLK + 2 * WINDOW
A_KEYS = A_SPAN + CTX
A_COLS = ATT_GROUP * ABLK


def _attn_kernel(sink_ref, q_ref, kv_ref, cos_ref, sin_ref, o_ref, q_scr, k_scr, vt_scr, s_scr, p_scr):
    g = pl.program_id(1)
    qscale = HEAD_P ** -0.5 * LOG2E

    def prep(r0, with_rope):
        q = q_ref[0, pl.ds(r0, RC), :].astype(f32) * qscale
        kv = kv_ref[0, pl.ds(r0, RC), :].astype(f32)
        k = kv
        if with_rope:
            cos_t, sin_t = cos_ref[pl.ds(r0, RC), :], sin_ref[pl.ds(r0, RC), :]
            q = _rope(q, cos_t, sin_t)
            k = _rope(kv, cos_t[:, 0:2 * HEAD_P], sin_t[:, 0:2 * HEAD_P])
        low = lax.broadcasted_iota(jnp.int32, kv.shape, 1) < HEAD_P
        k_scr[pl.ds(r0, RC), :] = jnp.where(low, k, 0.0).astype(bf16)
        for r in range(ATT_GROUP):
            slab = q[:, (r // 2) * 2 * HEAD_P:(r // 2 + 1) * 2 * HEAD_P]
            if r % 2:
                slab = pltpu.roll(slab, HEAD_P, 1)
            q_scr[r, pl.ds(r0, RC), :] = jnp.where(low, slab, 0.0).astype(bf16)
        v_t = kv.T[HEAD_P:2 * HEAD_P, :].astype(bf16)
        c0 = r0 // ABLK
        for h in range(RC // ABLK):
            vt_scr[c0 + h] = v_t[:, h * ABLK:(h + 1) * ABLK]
    _row_chunks(SEQ, lambda r0: prep(r0, True))
    for r0 in range(SEQ, S, RC):
        prep(r0, False)

    kmq = (lax.broadcasted_iota(jnp.int32, (ABLK, A_COLS), 0)
           - (lax.broadcasted_iota(jnp.int32, (ABLK, A_COLS), 1) & (ABLK - 1)))

    def block(q0, start, masks, slot):
        qstack = jnp.concatenate([q_scr[r, pl.ds(q0, ABLK), :] for r in range(ATT_GROUP)], axis=0)
        chunks = []
        if masks:
            s_loc = _bdot_nt(k_scr[pl.ds(start, len(masks) * ABLK), :], qstack)
            for c, kind in enumerate(masks):
                s_c = s_loc[c * ABLK:(c + 1) * ABLK]
                if kind is not None:
                    s_c = jnp.where(kmq >= 0 if kind == 'ge' else kmq <= 0, s_c, -jnp.inf)
                s_scr[slot, c * ABLK:(c + 1) * ABLK, :] = s_c
                chunks.append((c * ABLK, start // ABLK + c))
        n_loc = len(chunks) * ABLK
        s_scr[slot, n_loc:n_loc + CTX, :] = _bdot_nt(k_scr[SEQ:S, :], qstack)
        chunks += [(n_loc + c * ABLK, SEQ // ABLK + c) for c in range(CTX // ABLK)]

        inv = []
        for r in range(ATT_GROUP):
            cols = slice(r * ABLK, (r + 1) * ABLK)
            sink = sink_ref[g * ATT_GROUP + r] * LOG2E
            m = jnp.full((1, ABLK), sink, f32)
            for row, _ in chunks:
                m = jnp.maximum(m, jnp.max(s_scr[slot, row:row + ABLK, cols], axis=0, keepdims=True))
            den = jnp.exp2(sink - m)
            for row, _ in chunks:
                p = jnp.exp2(s_scr[slot, row:row + ABLK, cols] - m)
                den = den + jnp.sum(p, axis=0, keepdims=True)
                p_scr[slot, row:row + ABLK, cols] = p.astype(bf16)
            inv.append(1.0 / den)

        o_t = None
        for row, vc in chunks:
            part = _bdot(vt_scr[vc], p_scr[slot, row:row + ABLK, :])
            o_t = part if o_t is None else o_t + part
        o_t = jnp.concatenate([o_t[:, r * ABLK:(r + 1) * ABLK] * inv[r] for r in range(ATT_GROUP)], axis=0)
        o_ref[0, pl.ds(q0, ABLK), :] = o_t.T.astype(bf16)

    n_blk = SEQ // ABLK
    block(0, 0, (None, 'le'), 0)
    block(SEQ - ABLK, SEQ - 2 * ABLK, ('ge', None), 1)

    def lat_pair(j, carry):
        for slot in range(2):
            q0 = pl.multiple_of((1 + 2 * j + slot) * ABLK, ABLK)
            block(q0, pl.multiple_of(q0 - WINDOW, ABLK), ('ge', None, 'le'), slot)
        return carry
    lax.fori_loop(0, (n_blk - 2) // 2, lat_pair, 0, unroll=True)

    for slot, q0 in enumerate(range(SEQ, S, ABLK)):
        block(q0, 0, (), slot % 2)


def _attn_call(qkv, sinks, cos_t, sin_t):
    gw = ATT_GROUP * HEAD_P
    return pl.pallas_call(
        _attn_kernel,
        grid_spec=pltpu.PrefetchScalarGridSpec(
            num_scalar_prefetch=1,
            grid=(BATCH, KV_HEADS),
            in_specs=[pl.BlockSpec((1, S, gw), lambda b, g, s: (b, 0, g)),
                      pl.BlockSpec((1, S, 2 * HEAD_P), lambda b, g, s: (b, 0, QW // (2 * HEAD_P) + g)),
                      pl.BlockSpec((SEQ, gw), lambda b, g, s: (0, 0)),
                      pl.BlockSpec((SEQ, gw), lambda b, g, s: (0, 0))],
            out_specs=pl.BlockSpec((1, S, gw), lambda b, g, s: (b, 0, g)),
            scratch_shapes=[pltpu.VMEM((ATT_GROUP, S, 2 * HEAD_P), bf16),
                            pltpu.VMEM((S, 2 * HEAD_P), bf16),
                            pltpu.VMEM((S // ABLK, HEAD_P, ABLK), bf16),
                            pltpu.VMEM((2, A_KEYS, A_COLS), f32),
                            pltpu.VMEM((2, A_KEYS, A_COLS), bf16)]),
        out_shape=jax.ShapeDtypeStruct((BATCH, S, QW), bf16),
        compiler_params=_cparams(("parallel", "parallel"), 40),
        name="window_attn",
    )(sinks, qkv, qkv, cos_t, sin_t)


RES_TM = 768


def _outproj_kernel(x_ref, mod_ref, *rest, n_in):
    a_refs, w_refs, o_ref = rest[:n_in], rest[n_in:2 * n_in], rest[2 * n_in]
    acc = _bdot(a_refs[0][0], w_refs[0][...])
    for a_ref, w_ref in zip(a_refs[1:], w_refs[1:]):
        acc = acc + _bdot(a_ref[0], w_ref[...])
    for r0 in range(0, RES_TM, RC):
        gate = _mod_row(mod_ref[0], 2, pl.program_id(1) * RES_TM + r0)
        o_ref[0, r0:r0 + RC, :] = x_ref[0, r0:r0 + RC, :] + gate * acc[r0:r0 + RC]


def _outproj_call(xs, mods_l, acts, ws, name):
    n_in = len(acts)
    in_specs = [pl.BlockSpec((1, RES_TM, D), lambda b, i: (b, i, 0)),
                pl.BlockSpec((1, MOD_ROWS, D), lambda b, i: (b, 0, 0))]
    in_specs += [pl.BlockSpec((1, RES_TM, a.shape[2]), lambda b, i: (b, i, 0)) for a in acts]
    in_specs += [pl.BlockSpec(w.shape, lambda b, i: (0, 0)) for w in ws]
    return pl.pallas_call(
        functools.partial(_outproj_kernel, n_in=n_in),
        grid=(BATCH, S // RES_TM),
        in_specs=in_specs,
        out_specs=pl.BlockSpec((1, RES_TM, D), lambda b, i: (b, i, 0)),
        out_shape=jax.ShapeDtypeStruct((BATCH, S, D), f32),
        input_output_aliases={0: 0},
        compiler_params=_cparams(("parallel", "parallel"), 40),
        name=name,
    )(xs, mods_l, *acts, *ws)


FFN_TM = 768
FFN_TH = 256
FFN_NH = FFN_H // FFN_TH


FFN_TM_LAST = 512


def _ffn_kernel(x_ref, mod_ref, g_ref, win_ref, wo_ref, *rest, tm, final):
    gf_ref = rest[0] if final else None
    o_ref, h_scr, acc = rest[-3:]
    row0 = pl.program_id(1) * tm
    for r0 in range(0, tm, RC):
        h_scr[r0:r0 + RC, :] = _norm_mod(x_ref[0, r0:r0 + RC, :], g_ref[...], mod_ref[0],
                                         3, 4, row0 + r0).astype(bf16)
    for k in range(FFN_NH):
        gate_cols = slice(k * FFN_TH, (k + 1) * FFN_TH)
        up_cols = slice(FFN_H + k * FFN_TH, FFN_H + (k + 1) * FFN_TH)
        act = (_silu(_bdot(h_scr[...], win_ref[:, gate_cols]))
               * _bdot(h_scr[...], win_ref[:, up_cols])).astype(bf16)
        down = _bdot(act, wo_ref[gate_cols, :])
        if k == 0:
            acc[...] = down
        else:
            acc[...] += down
    for r0 in range(0, tm, RC):
        x = x_ref[0, r0:r0 + RC, :] + _mod_row(mod_ref[0], 5, row0 + r0) * acc[r0:r0 + RC, :]
        if final:
            ms = jnp.mean(x * x, axis=-1, keepdims=True)
            x = x * lax.rsqrt(ms + EPS) * gf_ref[...]
        o_ref[0, r0:r0 + RC, :] = x


def _ffn_call(xs, mods_l, g, w_in, w_out, final_g=None):
    final = final_g is not None
    tm, rows = (FFN_TM_LAST, SEQ) if final else (FFN_TM, S)
    resident = lambda a: pl.BlockSpec(a.shape, lambda b, i: (0, 0), pipeline_mode=pl.Buffered(1))
    vec = pl.BlockSpec((1, D), lambda b, i: (0, 0))
    return pl.pallas_call(
        functools.partial(_ffn_kernel, tm=tm, final=final),
        grid=(BATCH, rows // tm),
        in_specs=[pl.BlockSpec((1, tm, D), lambda b, i: (b, i, 0)),
                  pl.BlockSpec((1, MOD_ROWS, D), lambda b, i: (b, 0, 0)),
                  vec, resident(w_in), resident(w_out)] + ([vec] if final else []),
        out_specs=pl.BlockSpec((1, tm, D), lambda b, i: (b, i, 0)),
        out_shape=jax.ShapeDtypeStruct((BATCH, rows, D), f32),
        scratch_shapes=[pltpu.VMEM((tm, D), bf16), pltpu.VMEM((tm, D), f32)],
        input_output_aliases={} if final else {0: 0},
        compiler_params=_cparams(("parallel", "parallel"), 52),
        name="swiglu_ffn_final" if final else "swiglu_ffn",
    )(xs, mods_l, g, w_in, w_out, *([final_g] if final else []))


def _dft_table(lb):
    n = 2 * lb
    f = jnp.arange(lb, dtype=jnp.int32)[:, None]
    t = jnp.arange(lb, dtype=jnp.int32)[None, :]
    ang = (2.0 * math.pi / n) * ((f * t) % n).astype(f32)
    nyq = jnp.where((t & 1) == 0, 1.0, -1.0).astype(f32)
    sin_rows = jnp.where(f == 0, nyq, jnp.sin(ang))
    return jnp.concatenate([jnp.cos(ang), sin_rows], axis=0)


def _hy_features(seg):
    t = jnp.abs(jnp.arange(2 * seg) - seg).astype(f32)
    t_unit = t / (seg - 1)
    ang = 2.0 * math.pi * t / seg
    fb = jnp.linspace(1e-4, HY_BANDS - 1, HY_BANDS, dtype=f32)
    z = jnp.concatenate([t_unit[:, None], jnp.cos(ang[:, None] * fb), -jnp.sin(ang[:, None] * fb)], axis=-1)
    z = jnp.pad(z, ((0, 0), (0, 128 - HY_EMB)))
    deltas = jnp.linspace(math.log(1e-2) / 1.5, math.log(1e-2) / 0.3, D, dtype=f32)
    win = jnp.exp(-t_unit[:, None] * jnp.abs(deltas)[None, :])
    return z, win


def _rope_tables():
    nf = HEAD_P // 4
    inv = ROPE_BASE ** (-jnp.arange(nf, dtype=f32) / nf)
    pos = jnp.arange(SEQ)
    row = (pos // GRID_W).astype(f32)
    col = (pos % GRID_W).astype(f32)
    ar = row[:, None] * inv[None, :]
    ac = col[:, None] * inv[None, :]
    ang = jnp.concatenate([ar, ar, ac, ac], axis=-1)
    sign = jnp.tile(jnp.concatenate([-jnp.ones(nf, f32), jnp.ones(nf, f32)]), 2)
    cos_t = jnp.tile(jnp.cos(ang), (1, ATT_GROUP))
    sin_t = jnp.tile(jnp.sin(ang) * sign[None, :], (1, ATT_GROUP))
    return cos_t, sin_t


def _head_expand(lane_off):
    r = jnp.arange(DT_LANES)[:, None]
    c = jnp.arange(D)[None, :]
    return (c // HEAD_P + lane_off == r).astype(bf16)


def _pad_rows(a, rows):
    return jnp.pad(a, ((0, rows - a.shape[0]), (0, 0)))


def _pad_lanes(a, lanes):
    return jnp.pad(a, ((0, 0), (0, lanes - a.shape[1])))


def kernel(x, c, ctx, c_ctx, ada_w, ada_b, norm1_g, norm2_g, hy_in_w, ssd_conv_w, ssd_conv_b, ssd_dt_bias, ssd_a_log, ssd_d, ssd_norm_g, hy_conv_w, hy_conv_b, hy_w1, hy_b1, hy_w2, hy_b2, hy_w3, hy_b3, hy_w4, hy_freq, hy_bias, hy_out_w, attn_qkv_w, attn_sinks, attn_out_w, ffn_w_in, ffn_w_out, final_g):
    xs = jnp.concatenate([x, ctx], axis=1)

    cc = jnp.concatenate([c, c_ctx[None, :], jnp.zeros((MOD_ROWS - BATCH - 1, D), f32)], axis=0)
    modv = _mod_call(cc, ada_w, ada_b)
    lat = modv[:, :BATCH].reshape(DEPTH, BATCH, N_MOD, D)
    cmod = jnp.broadcast_to(modv[:, BATCH].reshape(DEPTH, 1, N_MOD, D), (DEPTH, BATCH, N_MOD, D))
    pad = jnp.zeros((DEPTH, BATCH, CTX_MOD - N_MOD, D), f32)
    mods = jnp.concatenate([lat, pad, cmod, pad], axis=2)

    tabs = {}
    for seg, nb in ((SEQ, HY_NB), (CTX, 1)):
        w = _dft_table(seg // nb)
        tabs[seg] = (w.astype(bf16), w.T.astype(bf16)) + _hy_features(seg)
    e_f, e_b = _head_expand(0), _head_expand(HEADS)
    cos_t, sin_t = _rope_tables()

    for layer in range(DEPTH):
        j = layer // 2
        mods_l = mods[layer]
        g1 = norm1_g[layer][None, :]
        if layer % 2 == 0:
            w_in = hy_in_w[j]
            w_zx = w_in[:, :COL_DT].astype(bf16)
            wdt = _pad_lanes(w_in[:, COL_DT:COL_HY], DT_LANES).astype(bf16)
            dtb = _pad_lanes(ssd_dt_bias[j].reshape(1, 2 * HEADS), DT_LANES)
            cw = jnp.concatenate([jnp.zeros((4, D), f32),
                                  jnp.concatenate([ssd_conv_w[j], ssd_conv_b[j][None, :]], axis=0)], axis=1)
            zx, dt = _inproj_ssd_call(xs, mods_l, g1, w_zx, _pad_rows(cw, 8), wdt, dtb)
            cw_hy = _pad_rows(jnp.concatenate([hy_conv_w[j], hy_conv_b[j][None, :]], axis=0), 8)
            x0, u = _inproj_hy_call(xs, mods_l, g1, w_in[:, COL_HY:].astype(bf16), cw_hy)

            alog = _pad_lanes(ssd_a_log[j].reshape(1, 2 * HEADS), DT_LANES)
            dskip = jnp.repeat(ssd_d[j], HEAD_P)[None, :]
            yn = _ssd_call(zx, dt, alog, dskip, ssd_norm_g[j][None, :], e_f, e_b)

            segs = []
            for seg, nb in ((SEQ, HY_NB), (CTX, 1)):
                wtab, wtab_t, zt, win = tabs[seg]
                ka, kb = _hyfilter_call(zt, _pad_rows(hy_w1[j], 128), hy_b1[j][None, :], hy_w2[j],
                                        hy_b2[j][None, :], hy_w3[j], hy_b3[j][None, :],
                                        _pad_rows(hy_freq[j], 8), hy_w4[j], win, wtab, seg, nb)
                segs.append((wtab, wtab_t, ka, kb))
            gh = _hyconv_call(u, x0, hy_bias[j][None, :], *segs)
            w_out = hy_out_w[j].astype(bf16)
            xs = _outproj_call(xs, mods_l, [yn, gh], [w_out[:D], w_out[D:]], "outproj_even")
        else:
            wq = attn_qkv_w[j]
            wk = wq[:, QW:QW + KVW].reshape(D, KV_HEADS, HEAD_P)
            wv = wq[:, QW + KVW:].reshape(D, KV_HEADS, HEAD_P)
            w_kv = jnp.concatenate([wk, wv], axis=2).reshape(D, 2 * KVW)
            w_qkv = jnp.concatenate([wq[:, :QW], w_kv], axis=1).astype(bf16)
            qkv = _nmm_call(xs, mods_l, g1, w_qkv, 512, "qkv_proj")
            o = _attn_call(qkv, attn_sinks[j], cos_t, sin_t)
            xs = _outproj_call(xs, mods_l, [o], [attn_out_w[j].astype(bf16)], "outproj_attn")
        xs = _ffn_call(xs, mods_l, norm2_g[layer][None, :], ffn_w_in[layer].astype(bf16),
                       ffn_w_out[layer].astype(bf16),
                       final_g=final_g[None, :] if layer == DEPTH - 1 else None)
    return xs
```

```python
import functools
import math

import jax
import jax.numpy as jnp
from jax import lax
from jax.experimental import pallas as pl
from jax.experimental.pallas import tpu as pltpu

f32 = jnp.float32
bf16 = jnp.bfloat16

D = 1024
BATCH = 8
SEQ = 2048
DEPTH = 4
GRID_W = 64
CTX = 256
S = SEQ + CTX
EPS = 1e-6
N_MOD = 6
MOD_ROWS = 16
CTX_MOD = 8

HEADS = 16
HEAD_P = 64
STATE = 128
GROUPS = 2
CHUNK = 128
XBC = D + 2 * GROUPS * STATE
ZX = D + XBC
DT_LANES = 128
HY_BANDS = 16
HY_EMB = 1 + 2 * HY_BANDS
HY_FFN = 64
COL_XBC = D
COL_DT = COL_XBC + XBC
COL_HY = COL_DT + 2 * HEADS

Q_HEADS = 16
KV_HEADS = 4
ATT_GROUP = Q_HEADS // KV_HEADS
QW = Q_HEADS * HEAD_P
KVW = KV_HEADS * HEAD_P
WINDOW = 128
ABLK = 128
ROPE_BASE = 10000.0
FFN_H = -(-8 * D // (3 * 256)) * 256

RC = 256
NMM_RC = 768
VMEM_CAP = 56 * 1024 * 1024


def _cparams(sem, vmem_mb):
    return pltpu.CompilerParams(dimension_semantics=sem,
                                vmem_limit_bytes=min(int(vmem_mb * 1024 * 1024), VMEM_CAP))


def _silu(v):
    return v * jax.nn.sigmoid(v)


def _softplus(v):
    return jnp.maximum(v, 0.0) + jnp.log1p(jnp.exp(-jnp.abs(v)))


def _bdot(a, b):
    return jnp.dot(a, b, preferred_element_type=f32)


def _bdot_nt(a, b):
    return lax.dot_general(a, b, (((1,), (1,)), ((), ())), preferred_element_type=f32)


def _row_chunks(n_rows, body, unroll=1):
    def step(c, carry):
        body(pl.multiple_of(c * RC, RC))
        return carry
    lax.fori_loop(0, n_rows // RC, step, 0, unroll=unroll)


def _norm_mod(x, g, mod, shift_row, scale_row, row0):
    ms = jnp.mean(x * x, axis=-1, keepdims=True)
    scale = _mod_row(mod, scale_row, row0)
    return x * lax.rsqrt(ms + EPS) * (g * (1.0 + scale)) + _mod_row(mod, shift_row, row0)


def _mod_row(mod, k, row0):
    return jnp.where(row0 >= SEQ, mod[CTX_MOD + k:CTX_MOD + k + 1], mod[k:k + 1])


def _norm_mod_to_scratch(x_ref, g_ref, mod_ref, h_scr):
    def body(r0):
        h_scr[pl.ds(r0, RC), :] = _norm_mod(x_ref[0, pl.ds(r0, RC), :], g_ref[...], mod_ref[0],
                                            0, 1, r0).astype(bf16)
    _row_chunks(S, body)


CONV_ROWS = 64


def _conv3_rows(p_scr, r0, cw):
    n, cols = CONV_ROWS, p_scr.shape[1]
    zero = jnp.zeros((8, cols), f32)
    tiles = [p_scr[r0 + k:r0 + k + 8, :] for k in range(0, n, 8)]
    above = zero if r0 in (0, SEQ) else p_scr[r0 - 8:r0, :]
    below = zero if r0 + n in (SEQ, S) else p_scr[r0 + n:r0 + n + 8, :]
    sub = lax.broadcasted_iota(jnp.int32, (8, cols), 0)
    down = [pltpu.roll(t, 1, 0) for t in [above] + tiles]
    up = [pltpu.roll(t, 7, 0) for t in tiles + [below]]
    out = []
    for k, t in enumerate(tiles):
        prev = jnp.where(sub == 0, down[k], down[k + 1])
        nxt = jnp.where(sub == 7, up[k + 1], up[k])
        out.append(prev * cw[0:1] + t * cw[1:2] + nxt * cw[2:3] + cw[3:4])
    return jnp.concatenate(out, axis=0)


def _project_then_conv(mm, piece):
    done = 0
    for r0 in range(0, S, NMM_RC):
        while done + CONV_ROWS < r0:
            piece(done)
            done += CONV_ROWS
        mm(r0)
    while done < S:
        piece(done)
        done += CONV_ROWS


def _mod_kernel(c_ref, w_ref, b_ref, o_ref):
    sc = _silu(c_ref[...]).astype(bf16)
    o_ref[0] = _bdot(sc, w_ref[0].astype(bf16)) + b_ref[0]


def _mod_call(cc, ada_w, ada_b):
    tn = 1024
    return pl.pallas_call(
        _mod_kernel,
        grid=(DEPTH, N_MOD * D // tn),
        in_specs=[pl.BlockSpec((MOD_ROWS, D), lambda l, j: (0, 0)),
                  pl.BlockSpec((1, D, tn), lambda l, j: (l, 0, j)),
                  pl.BlockSpec((1, 1, tn), lambda l, j: (l, 0, j))],
        out_specs=pl.BlockSpec((1, MOD_ROWS, tn), lambda l, j: (l, 0, j)),
        out_shape=jax.ShapeDtypeStruct((DEPTH, MOD_ROWS, N_MOD * D), f32),
        compiler_params=_cparams(("parallel", "parallel"), 24),
        name="adaln_mod",
    )(cc, ada_w, ada_b.reshape(DEPTH, 1, N_MOD * D))


def _nmm_kernel(x_ref, mod_ref, g_ref, w_ref, o_ref, h_scr):
    @pl.when(pl.program_id(1) == 0)
    def _():
        _norm_mod_to_scratch(x_ref, g_ref, mod_ref, h_scr)

    def body(c, carry):
        r0 = pl.multiple_of(c * NMM_RC, NMM_RC)
        o_ref[0, pl.ds(r0, NMM_RC), :] = _bdot(h_scr[pl.ds(r0, NMM_RC), :], w_ref[...]).astype(o_ref.dtype)
        return carry
    lax.fori_loop(0, S // NMM_RC, body, 0)


def _nmm_call(xs, mods_l, g, w, tn, name):
    n = w.shape[1]
    return pl.pallas_call(
        _nmm_kernel,
        grid=(BATCH, n // tn),
        in_specs=[pl.BlockSpec((1, S, D), lambda b, j: (b, 0, 0)),
                  pl.BlockSpec((1, MOD_ROWS, D), lambda b, j: (b, 0, 0)),
                  pl.BlockSpec((1, D), lambda b, j: (0, 0)),
                  pl.BlockSpec((D, tn), lambda b, j: (0, j))],
        out_specs=pl.BlockSpec((1, S, tn), lambda b, j: (b, 0, j)),
        out_shape=jax.ShapeDtypeStruct((BATCH, S, n), bf16),
        scratch_shapes=[pltpu.VMEM((S, D), bf16)],
        compiler_params=_cparams(("parallel", "arbitrary"), 40),
        name=name,
    )(xs, mods_l, g, w)


ZX_TN = 512
NZ_T = D // ZX_TN


def _inproj_ssd_kernel(x_ref, mod_ref, g_ref, w_ref, cw_ref, wdt_ref, dtb_ref, zx_ref, dt_ref,
                       h_scr, p_scr):
    j = pl.program_id(1)

    @pl.when(j == 0)
    def _():
        _norm_mod_to_scratch(x_ref, g_ref, mod_ref, h_scr)

        def dt_body(c, carry):
            r0 = pl.multiple_of(c * NMM_RC, NMM_RC)
            sp = _softplus(_bdot(h_scr[pl.ds(r0, NMM_RC), :], wdt_ref[...]) + dtb_ref[...])
            lane = lax.broadcasted_iota(jnp.int32, sp.shape, 1)
            dt_ref[0, pl.ds(r0, NMM_RC), :] = jnp.where(lane < 2 * HEADS, sp, 0.0)
            return carry
        lax.fori_loop(0, S // NMM_RC, dt_body, 0)

    @pl.when(j < NZ_T)
    def _():
        def body(c, carry):
            r0 = pl.multiple_of(c * NMM_RC, NMM_RC)
            zx_ref[0, pl.ds(r0, NMM_RC), :] = _bdot(h_scr[pl.ds(r0, NMM_RC), :], w_ref[...]).astype(bf16)
            return carry
        lax.fori_loop(0, S // NMM_RC, body, 0)

    @pl.when(j >= NZ_T)
    def _():
        def mm(r0):
            p_scr[r0:r0 + NMM_RC, :] = _bdot(h_scr[r0:r0 + NMM_RC, :], w_ref[...])

        def piece(a):
            zx_ref[0, a:a + CONV_ROWS, :] = _silu(_conv3_rows(p_scr, a, cw_ref[...])).astype(bf16)
        _project_then_conv(mm, piece)


def _inproj_ssd_call(xs, mods_l, g, w_zx, cw, wdt, dtb):
    return pl.pallas_call(
        _inproj_ssd_kernel,
        grid=(BATCH, ZX // ZX_TN),
        in_specs=[pl.BlockSpec((1, S, D), lambda b, j: (b, 0, 0)),
                  pl.BlockSpec((1, MOD_ROWS, D), lambda b, j: (b, 0, 0)),
                  pl.BlockSpec((1, D), lambda b, j: (0, 0)),
                  pl.BlockSpec((D, ZX_TN), lambda b, j: (0, j)),
                  pl.BlockSpec((8, ZX_TN), lambda b, j: (0, j)),
                  pl.BlockSpec((D, DT_LANES), lambda b, j: (0, 0)),
                  pl.BlockSpec((1, DT_LANES), lambda b, j: (0, 0))],
        out_specs=[pl.BlockSpec((1, S, ZX_TN), lambda b, j: (b, 0, j)),
                   pl.BlockSpec((1, S, DT_LANES), lambda b, j: (b, 0, 0))],
        out_shape=[jax.ShapeDtypeStruct((BATCH, S, ZX), bf16),
                   jax.ShapeDtypeStruct((BATCH, S, DT_LANES), f32)],
        scratch_shapes=[pltpu.VMEM((S, D), bf16), pltpu.VMEM((S, ZX_TN), f32)],
        compiler_params=_cparams(("parallel", "arbitrary"), 48),
        name="inproj_ssd",
    )(xs, mods_l, g, w_zx, cw, wdt, dtb)


HY_TN = 256


def _inproj_hy_kernel(x_ref, mod_ref, g_ref, w0_ref, w1_ref, w2_ref, c0_ref, c1_ref, c2_ref,
                      x0_ref, u_ref, h_scr, p0_scr, p1_scr, p2_scr):
    @pl.when(pl.program_id(1) == 0)
    def _():
        _norm_mod_to_scratch(x_ref, g_ref, mod_ref, h_scr)

    def mm_x0(r0):
        p0_scr[r0:r0 + NMM_RC, :] = _bdot(h_scr[r0:r0 + NMM_RC, :], w0_ref[...])

    def piece_x0(a):
        x0_ref[0, a:a + CONV_ROWS, :] = _conv3_rows(p0_scr, a, c0_ref[...]).astype(bf16)
    _project_then_conv(mm_x0, piece_x0)

    def mm_u(r0):
        p1_scr[r0:r0 + NMM_RC, :] = _bdot(h_scr[r0:r0 + NMM_RC, :], w1_ref[...])
        p2_scr[r0:r0 + NMM_RC, :] = _bdot(h_scr[r0:r0 + NMM_RC, :], w2_ref[...])

    def piece_u(a):
        x1 = _conv3_rows(p1_scr, a, c1_ref[...])
        v = _conv3_rows(p2_scr, a, c2_ref[...])
        u_ref[0, a:a + CONV_ROWS, :] = (x1 * v).astype(bf16)
    _project_then_conv(mm_u, piece_u)


def _inproj_hy_call(xs, mods_l, g, w_hy, cw_hy):
    nt = D // HY_TN
    wspec = lambda k: pl.BlockSpec((D, HY_TN), lambda b, j, k=k: (0, j + k * nt))
    cspec = lambda k: pl.BlockSpec((8, HY_TN), lambda b, j, k=k: (0, j + k * nt))
    return pl.pallas_call(
        _inproj_hy_kernel,
        grid=(BATCH, nt),
        in_specs=[pl.BlockSpec((1, S, D), lambda b, j: (b, 0, 0)),
                  pl.BlockSpec((1, MOD_ROWS, D), lambda b, j: (b, 0, 0)),
                  pl.BlockSpec((1, D), lambda b, j: (0, 0)),
                  wspec(0), wspec(1), wspec(2), cspec(0), cspec(1), cspec(2)],
        out_specs=[pl.BlockSpec((1, S, HY_TN), lambda b, j: (b, 0, j)),
                   pl.BlockSpec((1, S, HY_TN), lambda b, j: (b, 0, j))],
        out_shape=[jax.ShapeDtypeStruct((BATCH, S, D), bf16),
                   jax.ShapeDtypeStruct((BATCH, S, D), bf16)],
        scratch_shapes=[pltpu.VMEM((S, D), bf16)] + [pltpu.VMEM((S, HY_TN), f32)] * 3,
        compiler_params=_cparams(("parallel", "arbitrary"), 48),
        name="inproj_hyena",
    )(xs, mods_l, g, w_hy, w_hy, w_hy, cw_hy, cw_hy, cw_hy)


N_CHUNK = S // CHUNK
LAT_CHUNKS = SEQ // CHUNK


def _split3_dot(t, a):
    a1 = a.astype(bf16)
    r1 = a - a1.astype(f32)
    a2 = r1.astype(bf16)
    a3 = (r1 - a2.astype(f32)).astype(bf16)
    return _bdot(t, a1) + _bdot(t, a2) + _bdot(t, a3)


def _ssd_chunk(zx_ref, dt_ref, aneg, e_ref, y_scr, st_scr, r0, backward):
    lane_off = HEADS if backward else 0
    x = zx_ref[0, pl.ds(r0, CHUNK), D:2 * D].astype(f32)
    bm = zx_ref[0, pl.ds(r0, CHUNK), 2 * D:2 * D + GROUPS * STATE]
    cm = zx_ref[0, pl.ds(r0, CHUNK), 2 * D + GROUPS * STATE:ZX]
    dt = dt_ref[0, pl.ds(r0, CHUNK), :]
    a = dt * aneg

    ri = lax.broadcasted_iota(jnp.int32, (CHUNK, CHUNK), 0)
    ci = lax.broadcasted_iota(jnp.int32, (CHUNK, CHUNK), 1)
    causal = (ci >= ri) if backward else (ci <= ri)
    tmat = jnp.where(causal, 1.0, 0.0).astype(bf16)
    acum = _split3_dot(tmat, a)
    acum_t = acum.T
    tot = acum[0:1] if backward else acum[CHUNK - 1:CHUNK]
    dte = jnp.exp2(tot - acum)
    eac = jnp.exp2(acum)
    cdec = jnp.broadcast_to(jnp.exp2(tot), (16, DT_LANES))

    e = e_ref[...]
    dt_x = _bdot(dt.astype(bf16), e)
    dte_x = _bdot(dte.astype(bf16), e)
    eac_x = _bdot(eac.astype(bf16), e)
    cdec_x = _bdot(cdec.astype(bf16), e)[0:1]

    xdt = x * dt_x
    xdt_b = xdt.astype(bf16)
    xs_b = (xdt * dte_x).astype(bf16)
    lane = lax.broadcasted_iota(jnp.int32, (CHUNK, 2 * HEAD_P), 1)
    low = lane < HEAD_P
    hg = HEADS // GROUPS
    gw = hg * HEAD_P
    for g in range(GROUPS):
        bg = bm[:, g * STATE:(g + 1) * STATE]
        cg = cm[:, g * STATE:(g + 1) * STATE]
        cb = _bdot_nt(cg, bg)
        hin = st_scr[:, g * gw:(g + 1) * gw]
        yoff = _bdot(cg, hin.astype(bf16)) * eac_x[:, g * gw:(g + 1) * gw]
        for hp in range(hg // 2):
            h0 = g * hg + 2 * hp
            ms = []
            for h in (h0, h0 + 1):
                l0 = lane_off + h
                seg = acum[:, l0:l0 + 1] - acum_t[l0:l0 + 1, :]
                ms.append(cb * jnp.exp2(jnp.where(causal, seg, -jnp.inf)))
            lhs = jnp.concatenate(ms, axis=1).astype(bf16)
            xp = xdt[:, h0 * HEAD_P:(h0 + 2) * HEAD_P]
            rhs = jnp.concatenate([jnp.where(low, xp, 0.0), jnp.where(low, 0.0, xp)],
                                  axis=0).astype(bf16)
            yd = _bdot(lhs, rhs) + yoff[:, hp * 2 * HEAD_P:(hp + 1) * 2 * HEAD_P]
            cols = slice(h0 * HEAD_P, (h0 + 2) * HEAD_P)
            y_scr[pl.ds(r0, CHUNK), cols] = y_scr[pl.ds(r0, CHUNK), cols] + yd
        bg_t = bg.astype(f32).T.astype(bf16)
        st_scr[:, g * gw:(g + 1) * gw] = (hin * cdec_x[:, g * gw:(g + 1) * gw]
                                         + _bdot(bg_t, xs_b[:, g * gw:(g + 1) * gw]))
    del xdt_b


def _ssd_kernel(zx_ref, dt_ref, alog_ref, dskip_ref, ng_ref, ef_ref, eb_ref, o_ref,
                y_scr, sf_scr, sb_scr):
    aneg = -jnp.exp(alog_ref[...]) * LOG2E
    sf_scr[...] = jnp.zeros_like(sf_scr)
    sb_scr[...] = jnp.zeros_like(sb_scr)

    def init(c, carry):
        r0 = pl.multiple_of(c * CHUNK, CHUNK)
        y_scr[pl.ds(r0, CHUNK), :] = dskip_ref[...] * zx_ref[0, pl.ds(r0, CHUNK), D:2 * D].astype(f32)
        return carry
    lax.fori_loop(0, N_CHUNK, init, 0)

    def step(t, carry):
        cf = jnp.where(t < N_CHUNK - LAT_CHUNKS, LAT_CHUNKS + t, t - (N_CHUNK - LAT_CHUNKS))
        cbk = N_CHUNK - 1 - t
        _ssd_chunk(zx_ref, dt_ref, aneg, ef_ref, y_scr, sf_scr, pl.multiple_of(cf * CHUNK, CHUNK), False)
        _ssd_chunk(zx_ref, dt_ref, aneg, eb_ref, y_scr, sb_scr, pl.multiple_of(cbk * CHUNK, CHUNK), True)
        return carry
    lax.fori_loop(0, N_CHUNK, step, 0, unroll=2)

    def fin(c, carry):
        r0 = pl.multiple_of(c * CHUNK, CHUNK)
        t = y_scr[pl.ds(r0, CHUNK), :] * _silu(zx_ref[0, pl.ds(r0, CHUNK), 0:D].astype(f32))
        ms = jnp.mean(t * t, axis=-1, keepdims=True)
        o_ref[0, pl.ds(r0, CHUNK), :] = (t * lax.rsqrt(ms + EPS) * ng_ref[...]).astype(bf16)
        return carry
    lax.fori_loop(0, N_CHUNK, fin, 0)


def _ssd_call(zx, dt, alog, dskip, ng, e_f, e_b):
    return pl.pallas_call(
        _ssd_kernel,
        grid=(BATCH,),
        in_specs=[pl.BlockSpec((1, S, ZX), lambda b: (b, 0, 0)),
                  pl.BlockSpec((1, S, DT_LANES), lambda b: (b, 0, 0)),
                  pl.BlockSpec((1, DT_LANES), lambda b: (0, 0)),
                  pl.BlockSpec((1, D), lambda b: (0, 0)),
                  pl.BlockSpec((1, D), lambda b: (0, 0)),
                  pl.BlockSpec((DT_LANES, D), lambda b: (0, 0)),
                  pl.BlockSpec((DT_LANES, D), lambda b: (0, 0))],
        out_specs=pl.BlockSpec((1, S, D), lambda b: (b, 0, 0)),
        out_shape=jax.ShapeDtypeStruct((BATCH, S, D), bf16),
        scratch_shapes=[pltpu.VMEM((S, D), f32),
                        pltpu.VMEM((STATE, D), f32),
                        pltpu.VMEM((STATE, D), f32)],
        compiler_params=_cparams(("parallel",), 56),
        name="ssd_scan",
    )(zx, dt, alog, dskip, ng, e_f, e_b)


def _hp_dot(a, b):
    return jnp.dot(a, b, precision=lax.Precision.HIGHEST, preferred_element_type=f32)


def _hyfilter_kernel(z_ref, w1_ref, b1_ref, w2_ref, b2_ref, w3_ref, b3_ref, fr_ref, w4f_ref, w4b_ref,
                     win_ref, w_ref, ka_ref, kb_ref, h_scr, tap_scr, tb_scr, pp_scr, pq_scr, *, seg, nb):
    lb = seg // nb
    tc = tap_scr.shape[1]

    @pl.when(pl.program_id(0) == 0)
    def _():
        fr = fr_ref[...]

        def hidden(c, carry):
            r0 = pl.multiple_of(c * RC, RC)
            h = jnp.sin(fr[0:1] * (_hp_dot(z_ref[pl.ds(r0, RC), :], w1_ref[...]) + b1_ref[...]))
            h = jnp.sin(fr[1:2] * (_hp_dot(h, w2_ref[...]) + b2_ref[...]))
            h_scr[pl.ds(r0, RC), :] = jnp.sin(fr[2:3] * (_hp_dot(h, w3_ref[...]) + b3_ref[...]))
            return carry
        lax.fori_loop(0, 2 * seg // RC, hidden, 0)

    def taps(w4_ref):
        def body(c, ssq):
            r0 = pl.multiple_of(c * RC, RC)
            t = _hp_dot(h_scr[pl.ds(r0, RC), :], w4_ref[...]) * win_ref[pl.ds(r0, RC), :]
            row = r0 + lax.broadcasted_iota(jnp.int32, t.shape, 0)
            t = jnp.where(row == 0, 0.0, t)
            tap_scr[pl.ds(r0, RC), :] = t
            return ssq + jnp.sum(t * t, axis=0, keepdims=True)
        return body
    ssq = lax.fori_loop(0, seg // RC, taps(w4b_ref), jnp.zeros((1, tc), f32))
    nrm = lax.rsqrt(lax.fori_loop(seg // RC, 2 * seg // RC, taps(w4f_ref), ssq) + 1e-6)

    def scale(c, carry):
        r0 = pl.multiple_of(c * RC, RC)
        tb_scr[pl.ds(r0, RC), :] = (tap_scr[pl.ds(r0, RC), :] * nrm).astype(bf16)
        return carry
    lax.fori_loop(0, 2 * seg // RC, scale, 0)

    row = lax.broadcasted_iota(jnp.int32, (lb, tc), 0)
    sgn = jnp.where((row & 1) == 0, 1.0, -1.0)
    for s in range(2 * nb):
        blk = tb_scr[s * lb:(s + 1) * lb, :]
        p = _bdot(w_ref[0:lb, :], blk)
        q = _bdot(w_ref[lb:2 * lb, :], blk)
        if s > 0:
            t0 = tb_scr[(s - 1) * lb:(s - 1) * lb + 16, :][0:1].astype(f32)
            ka_ref[s - 1] = p + sgn * (pp_scr[...] - t0)
            kb_ref[s - 1] = jnp.where(row == 0, q + pq_scr[...] - t0, q + sgn * pq_scr[...])
        pp_scr[...] = p
        pq_scr[...] = q


HY_TC = 256
HY_NB = 4


def _hyfilter_call(zt, w1, b1, w2, b2, w3, b3, fr, w4, win, wtab, seg, nb):
    tc, lb = HY_TC, seg // nb
    nct = D // tc
    const = lambda shape: pl.BlockSpec(shape, lambda c: (0, 0))
    return pl.pallas_call(
        functools.partial(_hyfilter_kernel, seg=seg, nb=nb),
        grid=(nct,),
        in_specs=[const((2 * seg, 128)), const((128, HY_FFN)), const((1, HY_FFN)),
                  const((HY_FFN, HY_FFN)), const((1, HY_FFN)),
                  const((HY_FFN, HY_FFN)), const((1, HY_FFN)), const((8, HY_FFN)),
                  pl.BlockSpec((HY_FFN, tc), lambda c: (0, c)),
                  pl.BlockSpec((HY_FFN, tc), lambda c: (0, c + nct)),
                  pl.BlockSpec((2 * seg, tc), lambda c: (0, c)),
                  const((2 * lb, lb))],
        out_specs=[pl.BlockSpec((2 * nb - 1, lb, tc), lambda c: (0, 0, c)),
                   pl.BlockSpec((2 * nb - 1, lb, tc), lambda c: (0, 0, c))],
        out_shape=[jax.ShapeDtypeStruct((2 * nb - 1, lb, D), f32)] * 2,
        scratch_shapes=[pltpu.VMEM((2 * seg, HY_FFN), f32),
                        pltpu.VMEM((2 * seg, tc), f32), pltpu.VMEM((2 * seg, tc), bf16),
                        pltpu.VMEM((lb, tc), f32), pltpu.VMEM((lb, tc), f32)],
        compiler_params=_cparams(("arbitrary",), 40),
        name=f"hyena_filter_{seg}",
    )(zt, w1, b1, w2, b2, w3, b3, fr, w4, w4, win, wtab)


HY_PIECE = 64


def _hyconv_segment(u_ref, x0_ref, bias_ref, o_ref, w_ref, wt_ref, ka_ref, kb_ref,
                    ua_scr, ub_scr, ya_scr, yb_scr, row0, nb, lb):
    for i in range(nb):
        ui = u_ref[0, row0 + i * lb:row0 + (i + 1) * lb, :]
        ua_scr[i, 0:lb, :] = _bdot(w_ref[0:lb, :], ui)
        ub_scr[i, 0:lb, :] = _bdot(w_ref[lb:2 * lb, :], ui)

    is0 = lax.broadcasted_iota(jnp.int32, (HY_PIECE, ka_ref.shape[2]), 0) == 0
    for o in range(nb):
        for r in range(0, lb, HY_PIECE):
            rows = slice(r, r + HY_PIECE)
            ya = yb = None
            for i in range(nb):
                a, b = ua_scr[i, rows, :], ub_scr[i, rows, :]
                ka, kb = ka_ref[o - i + nb - 1, rows, :], kb_ref[o - i + nb - 1, rows, :]
                if r == 0:
                    pa = jnp.where(is0, a * ka, a * ka - b * kb)
                    pb = jnp.where(is0, b * kb, a * kb + b * ka)
                else:
                    pa, pb = a * ka - b * kb, a * kb + b * ka
                ya = pa if ya is None else ya + pa
                yb = pb if yb is None else yb + pb
            scale = jnp.where(is0, 0.5 / lb, 1.0 / lb) if r == 0 else 1.0 / lb
            ya_scr[o, rows, :] = (ya * scale).astype(bf16)
            yb_scr[o, rows, :] = (yb * scale).astype(bf16)
        y = _bdot(wt_ref[:, 0:lb], ya_scr[o, 0:lb, :]) + _bdot(wt_ref[:, lb:2 * lb], yb_scr[o, 0:lb, :])
        rows = slice(row0 + o * lb, row0 + (o + 1) * lb)
        y = y + bias_ref[...] * u_ref[0, rows, :].astype(f32)
        o_ref[0, rows, :] = (x0_ref[0, rows, :].astype(f32) * y).astype(bf16)


def _hyconv_kernel(u_ref, x0_ref, bias_ref, w_ref, wt_ref, ka_ref, kb_ref, cw_ref, cwt_ref, cka_ref,
                   ckb_ref, o_ref, ua_scr, ub_scr, ya_scr, yb_scr):
    scr = (ua_scr, ub_scr, ya_scr, yb_scr)
    _hyconv_segment(u_ref, x0_ref, bias_ref, o_ref, w_ref, wt_ref, ka_ref, kb_ref, *scr,
                    row0=0, nb=HY_NB, lb=SEQ // HY_NB)
    _hyconv_segment(u_ref, x0_ref, bias_ref, o_ref, cw_ref, cwt_ref, cka_ref, ckb_ref, *scr,
                    row0=SEQ, nb=1, lb=CTX)


def _hyconv_call(u, x0, bias, lat, ctx):
    tc, lb = HY_TC, SEQ // HY_NB
    full = lambda a: pl.BlockSpec(a.shape, lambda c, b: (0,) * a.ndim)
    spec = lambda a: pl.BlockSpec(a.shape[:2] + (tc,), lambda c, b: (0, 0, c))
    seq = pl.BlockSpec((1, S, tc), lambda c, b: (b, 0, c))
    return pl.pallas_call(
        _hyconv_kernel,
        grid=(D // tc, BATCH),
        in_specs=[seq, seq, pl.BlockSpec((1, tc), lambda c, b: (0, c)),
                  full(lat[0]), full(lat[1]), spec(lat[2]), spec(lat[3]),
                  full(ctx[0]), full(ctx[1]), spec(ctx[2]), spec(ctx[3])],
        out_specs=seq,
        out_shape=jax.ShapeDtypeStruct((BATCH, S, D), bf16),
        scratch_shapes=[pltpu.VMEM((HY_NB, lb, tc), f32), pltpu.VMEM((HY_NB, lb, tc), f32),
                        pltpu.VMEM((HY_NB, lb, tc), bf16), pltpu.VMEM((HY_NB, lb, tc), bf16)],
        compiler_params=_cparams(("parallel", "parallel"), 40),
        name="hyena_conv",
    )(u, x0, bias, *lat, *ctx)


def _rope(xb, perm, cos_t, sin_t):
    n = xb.shape[1]
    return xb.astype(f32) * cos_t + _bdot(xb, perm[0:n, 0:n]) * sin_t


assert WINDOW == ABLK
LOG2E = math.log2(math.e)
A_SPAN = ABLK + 2 * WINDOW
A_KEYS = A_SPAN + CTX
A_COLS = ATT_GROUP * ABLK


def _attn_kernel(sink_ref, q_ref, kv_ref, cos_ref, sin_ref, perm_ref, o_ref, q_scr, k_scr, vt_scr,
                 s_scr, p_scr):
    g = pl.program_id(1)
    qscale = HEAD_P ** -0.5 * LOG2E

    def prep(r0, with_rope):
        qb = q_ref[0, pl.ds(r0, RC), :]
        kvb = kv_ref[0, pl.ds(r0, RC), :]
        kv = kvb.astype(f32)
        if with_rope:
            cos_t, sin_t = cos_ref[pl.ds(r0, RC), :], sin_ref[pl.ds(r0, RC), :]
            q = _rope(qb, perm_ref[...], cos_t, sin_t) * qscale
            k = _rope(kvb, perm_ref[...], cos_t[:, 0:2 * HEAD_P], sin_t[:, 0:2 * HEAD_P])
        else:
            q, k = qb.astype(f32) * qscale, kv
        low = lax.broadcasted_iota(jnp.int32, kv.shape, 1) < HEAD_P
        k_scr[pl.ds(r0, RC), :] = jnp.where(low, k, 0.0).astype(bf16)
        for r in range(ATT_GROUP):
            slab = q[:, (r // 2) * 2 * HEAD_P:(r // 2 + 1) * 2 * HEAD_P]
            if r % 2:
                slab = pltpu.roll(slab, HEAD_P, 1)
            q_scr[r, pl.ds(r0, RC), :] = jnp.where(low, slab, 0.0).astype(bf16)
        v_t = kv.T[HEAD_P:2 * HEAD_P, :].astype(bf16)
        c0 = r0 // ABLK
        for h in range(RC // ABLK):
            vt_scr[c0 + h] = v_t[:, h * ABLK:(h + 1) * ABLK]
    _row_chunks(SEQ, lambda r0: prep(r0, True), unroll=2)
    for r0 in range(SEQ, S, RC):
        prep(r0, False)

    kmq = (lax.broadcasted_iota(jnp.int32, (ABLK, A_COLS), 0)
           - (lax.broadcasted_iota(jnp.int32, (ABLK, A_COLS), 1) & (ABLK - 1)))

    def block(q0, start, masks, slot):
        qstack = jnp.concatenate([q_scr[r, pl.ds(q0, ABLK), :] for r in range(ATT_GROUP)], axis=0)
        chunks = []
        if masks:
            s_loc = _bdot_nt(k_scr[pl.ds(start, len(masks) * ABLK), :], qstack)
            for c, kind in enumerate(masks):
                s_c = s_loc[c * ABLK:(c + 1) * ABLK]
                if kind is not None:
                    s_c = jnp.where(kmq >= 0 if kind == 'ge' else kmq <= 0, s_c, -jnp.inf)
                s_scr[slot, c * ABLK:(c + 1) * ABLK, :] = s_c
                chunks.append((c * ABLK, start // ABLK + c))
        n_loc = len(chunks) * ABLK
        s_scr[slot, n_loc:n_loc + CTX, :] = _bdot_nt(k_scr[SEQ:S, :], qstack)
        chunks += [(n_loc + c * ABLK, SEQ // ABLK + c) for c in range(CTX // ABLK)]

        inv = []
        for r in range(ATT_GROUP):
            cols = slice(r * ABLK, (r + 1) * ABLK)
            sink = sink_ref[g * ATT_GROUP + r] * LOG2E
            m = jnp.full((1, ABLK), sink, f32)
            for row, _ in chunks:
                m = jnp.maximum(m, jnp.max(s_scr[slot, row:row + ABLK, cols], axis=0, keepdims=True))
            den = jnp.exp2(sink - m)
            for row, _ in chunks:
                p = jnp.exp2(s_scr[slot, row:row + ABLK, cols] - m)
                den = den + jnp.sum(p, axis=0, keepdims=True)
                p_scr[slot, row:row + ABLK, cols] = p.astype(bf16)
            inv.append(1.0 / den)

        o_t = None
        for row, vc in chunks:
            part = _bdot(vt_scr[vc], p_scr[slot, row:row + ABLK, :])
            o_t = part if o_t is None else o_t + part
        o_t = jnp.concatenate([o_t[:, r * ABLK:(r + 1) * ABLK] * inv[r] for r in range(ATT_GROUP)], axis=0)
        o_ref[0, pl.ds(q0, ABLK), :] = o_t.T.astype(bf16)

    n_blk = SEQ // ABLK
    block(0, 0, (None, 'le'), 0)
    block(SEQ - ABLK, SEQ - 2 * ABLK, ('ge', None), 1)

    def lat_pair(j, carry):
        for slot in range(2):
            q0 = pl.multiple_of((1 + 2 * j + slot) * ABLK, ABLK)
            block(q0, pl.multiple_of(q0 - WINDOW, ABLK), ('ge', None, 'le'), slot)
        return carry
    lax.fori_loop(0, (n_blk - 2) // 2, lat_pair, 0, unroll=True)

    for slot, q0 in enumerate(range(SEQ, o_ref.shape[1], ABLK)):
        block(q0, 0, (), slot % 2)


def _attn_call(qkv, sinks, cos_t, sin_t, perm, need_ctx):
    gw = ATT_GROUP * HEAD_P
    rows = S if need_ctx else SEQ
    return pl.pallas_call(
        _attn_kernel,
        grid_spec=pltpu.PrefetchScalarGridSpec(
            num_scalar_prefetch=1,
            grid=(BATCH, KV_HEADS),
            in_specs=[pl.BlockSpec((1, S, gw), lambda b, g, s: (b, 0, g)),
                      pl.BlockSpec((1, S, 2 * HEAD_P), lambda b, g, s: (b, 0, QW // (2 * HEAD_P) + g)),
                      pl.BlockSpec((SEQ, gw), lambda b, g, s: (0, 0)),
                      pl.BlockSpec((SEQ, gw), lambda b, g, s: (0, 0)),
                      pl.BlockSpec((gw, gw), lambda b, g, s: (0, 0))],
            out_specs=pl.BlockSpec((1, rows, gw), lambda b, g, s: (b, 0, g)),
            scratch_shapes=[pltpu.VMEM((ATT_GROUP, S, 2 * HEAD_P), bf16),
                            pltpu.VMEM((S, 2 * HEAD_P), bf16),
                            pltpu.VMEM((S // ABLK, HEAD_P, ABLK), bf16),
                            pltpu.VMEM((2, A_KEYS, A_COLS), f32),
                            pltpu.VMEM((2, A_KEYS, A_COLS), bf16)]),
        out_shape=jax.ShapeDtypeStruct((BATCH, rows, QW), bf16),
        compiler_params=_cparams(("parallel", "parallel"), 40),
        name="window_attn",
    )(sinks, qkv, qkv, cos_t, sin_t, perm)


RES_TM = 768
RES_TM_LAT = 1024


def _outproj_kernel(x_ref, mod_ref, *rest, n_in, tm):
    a_refs, w_refs, o_ref = rest[:n_in], rest[n_in:2 * n_in], rest[2 * n_in]
    acc = _bdot(a_refs[0][0], w_refs[0][...])
    for a_ref, w_ref in zip(a_refs[1:], w_refs[1:]):
        acc = acc + _bdot(a_ref[0], w_ref[...])
    for r0 in range(0, tm, RC):
        gate = _mod_row(mod_ref[0], 2, pl.program_id(1) * tm + r0)
        o_ref[0, r0:r0 + RC, :] = x_ref[0, r0:r0 + RC, :] + gate * acc[r0:r0 + RC]


def _outproj_call(xs, mods_l, acts, ws, name):
    n_in = len(acts)
    rows = acts[0].shape[1]
    tm = RES_TM if rows == S else RES_TM_LAT
    in_specs = [pl.BlockSpec((1, tm, D), lambda b, i: (b, i, 0)),
                pl.BlockSpec((1, MOD_ROWS, D), lambda b, i: (b, 0, 0))]
    in_specs += [pl.BlockSpec((1, tm, a.shape[2]), lambda b, i: (b, i, 0)) for a in acts]
    in_specs += [pl.BlockSpec(w.shape, lambda b, i: (0, 0)) for w in ws]
    return pl.pallas_call(
        functools.partial(_outproj_kernel, n_in=n_in, tm=tm),
        grid=(BATCH, rows // tm),
        in_specs=in_specs,
        out_specs=pl.BlockSpec((1, tm, D), lambda b, i: (b, i, 0)),
        out_shape=jax.ShapeDtypeStruct((BATCH, S, D), f32),
        input_output_aliases={0: 0},
        compiler_params=_cparams(("parallel", "parallel"), 40),
        name=name,
    )(xs, mods_l, *acts, *ws)


FFN_TM = 1152
FFN_TM_LAST = 1024
FFN_RC = 128
FFN_TH = 256
FFN_NH = FFN_H // FFN_TH


def _ffn_kernel(x_ref, mod_ref, g_ref, win_ref, wo_ref, *rest, tm, final):
    gf_ref = rest[0] if final else None
    o_ref, h_scr, acc = rest[-3:]
    row0 = pl.program_id(1) * tm
    for r0 in range(0, tm, FFN_RC):
        h_scr[r0:r0 + FFN_RC, :] = _norm_mod(x_ref[0, r0:r0 + FFN_RC, :], g_ref[...], mod_ref[0],
                                             3, 4, row0 + r0).astype(bf16)
    for k in range(FFN_NH):
        gate_cols = slice(k * FFN_TH, (k + 1) * FFN_TH)
        up_cols = slice(FFN_H + k * FFN_TH, FFN_H + (k + 1) * FFN_TH)
        act = (_silu(_bdot(h_scr[...], win_ref[:, gate_cols]))
               * _bdot(h_scr[...], win_ref[:, up_cols])).astype(bf16)
        down = _bdot(act, wo_ref[gate_cols, :])
        if k == 0:
            acc[...] = down
        else:
            acc[...] += down
    for r0 in range(0, tm, FFN_RC):
        x = x_ref[0, r0:r0 + FFN_RC, :] + _mod_row(mod_ref[0], 5, row0 + r0) * acc[r0:r0 + FFN_RC, :]
        if final:
            ms = jnp.mean(x * x, axis=-1, keepdims=True)
            x = x * lax.rsqrt(ms + EPS) * gf_ref[...]
        o_ref[0, r0:r0 + FFN_RC, :] = x


def _ffn_call(xs, mods_l, g, w_in, w_out, final_g=None):
    final = final_g is not None
    tm, rows = (FFN_TM_LAST, SEQ) if final else (FFN_TM, S)
    resident = lambda a: pl.BlockSpec(a.shape, lambda b, i: (0, 0), pipeline_mode=pl.Buffered(1))
    vec = pl.BlockSpec((1, D), lambda b, i: (0, 0))
    return pl.pallas_call(
        functools.partial(_ffn_kernel, tm=tm, final=final),
        grid=(BATCH, rows // tm),
        in_specs=[pl.BlockSpec((1, tm, D), lambda b, i: (b, i, 0)),
                  pl.BlockSpec((1, MOD_ROWS, D), lambda b, i: (b, 0, 0)),
                  vec, resident(w_in), resident(w_out)] + ([vec] if final else []),
        out_specs=pl.BlockSpec((1, tm, D), lambda b, i: (b, i, 0)),
        out_shape=jax.ShapeDtypeStruct((BATCH, rows, D), f32),
        scratch_shapes=[pltpu.VMEM((tm, D), bf16), pltpu.VMEM((tm, D), f32)],
        input_output_aliases={} if final else {0: 0},
        compiler_params=_cparams(("parallel", "parallel"), 52),
        name="swiglu_ffn_final" if final else "swiglu_ffn",
    )(xs, mods_l, g, w_in, w_out, *([final_g] if final else []))


def _dft_table(lb):
    n = 2 * lb
    f = jnp.arange(lb, dtype=jnp.int32)[:, None]
    t = jnp.arange(lb, dtype=jnp.int32)[None, :]
    ang = (2.0 * math.pi / n) * ((f * t) % n).astype(f32)
    nyq = jnp.where((t & 1) == 0, 1.0, -1.0).astype(f32)
    sin_rows = jnp.where(f == 0, nyq, jnp.sin(ang))
    return jnp.concatenate([jnp.cos(ang), sin_rows], axis=0)


def _hy_features(seg):
    t = jnp.abs(jnp.arange(2 * seg) - seg).astype(f32)
    t_unit = t / (seg - 1)
    ang = 2.0 * math.pi * t / seg
    fb = jnp.linspace(1e-4, HY_BANDS - 1, HY_BANDS, dtype=f32)
    z = jnp.concatenate([t_unit[:, None], jnp.cos(ang[:, None] * fb), -jnp.sin(ang[:, None] * fb)], axis=-1)
    z = jnp.pad(z, ((0, 0), (0, 128 - HY_EMB)))
    deltas = jnp.linspace(math.log(1e-2) / 1.5, math.log(1e-2) / 0.3, D, dtype=f32)
    win = jnp.exp(-t_unit[:, None] * jnp.abs(deltas)[None, :])
    return z, win


def _rope_tables():
    nf = HEAD_P // 4
    inv = ROPE_BASE ** (-jnp.arange(nf, dtype=f32) / nf)
    pos = jnp.arange(SEQ)
    row = (pos // GRID_W).astype(f32)
    col = (pos % GRID_W).astype(f32)
    ar = row[:, None] * inv[None, :]
    ac = col[:, None] * inv[None, :]
    ang = jnp.concatenate([ar, ar, ac, ac], axis=-1)
    sign = jnp.tile(jnp.concatenate([-jnp.ones(nf, f32), jnp.ones(nf, f32)]), 2)
    cos_t = jnp.tile(jnp.cos(ang), (1, ATT_GROUP))
    sin_t = jnp.tile(jnp.sin(ang) * sign[None, :], (1, ATT_GROUP))
    lane = jnp.arange(ATT_GROUP * HEAD_P)
    partner = jnp.where((lane & nf) == 0, lane + nf, lane - nf)
    perm = (lane[:, None] == partner[None, :]).astype(bf16)
    return cos_t, sin_t, perm


def _head_expand(lane_off):
    r = jnp.arange(DT_LANES)[:, None]
    c = jnp.arange(D)[None, :]
    return (c // HEAD_P + lane_off == r).astype(bf16)


def _pad_rows(a, rows):
    return jnp.pad(a, ((0, rows - a.shape[0]), (0, 0)))


def _pad_lanes(a, lanes):
    return jnp.pad(a, ((0, 0), (0, lanes - a.shape[1])))


def kernel(x, c, ctx, c_ctx, ada_w, ada_b, norm1_g, norm2_g, hy_in_w, ssd_conv_w, ssd_conv_b, ssd_dt_bias, ssd_a_log, ssd_d, ssd_norm_g, hy_conv_w, hy_conv_b, hy_w1, hy_b1, hy_w2, hy_b2, hy_w3, hy_b3, hy_w4, hy_freq, hy_bias, hy_out_w, attn_qkv_w, attn_sinks, attn_out_w, ffn_w_in, ffn_w_out, final_g):
    xs = jnp.concatenate([x, ctx], axis=1)

    cc = jnp.concatenate([c, c_ctx[None, :], jnp.zeros((MOD_ROWS - BATCH - 1, D), f32)], axis=0)
    modv = _mod_call(cc, ada_w, ada_b)
    lat = modv[:, :BATCH].reshape(DEPTH, BATCH, N_MOD, D)
    cmod = jnp.broadcast_to(modv[:, BATCH].reshape(DEPTH, 1, N_MOD, D), (DEPTH, BATCH, N_MOD, D))
    pad = jnp.zeros((DEPTH, BATCH, CTX_MOD - N_MOD, D), f32)
    mods = jnp.concatenate([lat, pad, cmod, pad], axis=2)

    tabs = {}
    for seg, nb in ((SEQ, HY_NB), (CTX, 1)):
        w = _dft_table(seg // nb)
        tabs[seg] = (w.astype(bf16), w.T.astype(bf16)) + _hy_features(seg)
    e_f, e_b = _head_expand(0), _head_expand(HEADS)
    cos_t, sin_t, perm = _rope_tables()

    for layer in range(DEPTH):
        j = layer // 2
        mods_l = mods[layer]
        g1 = norm1_g[layer][None, :]
        if layer % 2 == 0:
            w_in = hy_in_w[j]
            w_zx = w_in[:, :COL_DT].astype(bf16)
            wdt = _pad_lanes(w_in[:, COL_DT:COL_HY], DT_LANES).astype(bf16)
            dtb = _pad_lanes(ssd_dt_bias[j].reshape(1, 2 * HEADS), DT_LANES)
            cw = jnp.concatenate([jnp.zeros((4, D), f32),
                                  jnp.concatenate([ssd_conv_w[j], ssd_conv_b[j][None, :]], axis=0)], axis=1)
            zx, dt = _inproj_ssd_call(xs, mods_l, g1, w_zx, _pad_rows(cw, 8), wdt, dtb)
            cw_hy = _pad_rows(jnp.concatenate([hy_conv_w[j], hy_conv_b[j][None, :]], axis=0), 8)
            x0, u = _inproj_hy_call(xs, mods_l, g1, w_in[:, COL_HY:].astype(bf16), cw_hy)

            alog = _pad_lanes(ssd_a_log[j].reshape(1, 2 * HEADS), DT_LANES)
            dskip = jnp.repeat(ssd_d[j], HEAD_P)[None, :]
            yn = _ssd_call(zx, dt, alog, dskip, ssd_norm_g[j][None, :], e_f, e_b)

            segs = []
            for seg, nb in ((SEQ, HY_NB), (CTX, 1)):
                wtab, wtab_t, zt, win = tabs[seg]
                ka, kb = _hyfilter_call(zt, _pad_rows(hy_w1[j], 128), hy_b1[j][None, :], hy_w2[j],
                                        hy_b2[j][None, :], hy_w3[j], hy_b3[j][None, :],
                                        _pad_rows(hy_freq[j], 8), hy_w4[j], win, wtab, seg, nb)
                segs.append((wtab, wtab_t, ka, kb))
            gh = _hyconv_call(u, x0, hy_bias[j][None, :], *segs)
            w_out = hy_out_w[j].astype(bf16)
            xs = _outproj_call(xs, mods_l, [yn, gh], [w_out[:D], w_out[D:]], "outproj_even")
        else:
            wq = attn_qkv_w[j]
            wk = wq[:, QW:QW + KVW].reshape(D, KV_HEADS, HEAD_P)
            wv = wq[:, QW + KVW:].reshape(D, KV_HEADS, HEAD_P)
            w_kv = jnp.concatenate([wk, wv], axis=2).reshape(D, 2 * KVW)
            w_qkv = jnp.concatenate([wq[:, :QW], w_kv], axis=1).astype(bf16)
            qkv = _nmm_call(xs, mods_l, g1, w_qkv, 512, "qkv_proj")
            o = _attn_call(qkv, attn_sinks[j], cos_t, sin_t, perm, need_ctx=layer < DEPTH - 1)
            xs = _outproj_call(xs, mods_l, [o], [attn_out_w[j].astype(bf16)], "outproj_attn")
        xs = _ffn_call(xs, mods_l, norm2_g[layer][None, :], ffn_w_in[layer].astype(bf16),
                       ffn_w_out[layer].astype(bf16),
                       final_g=final_g[None, :] if layer == DEPTH - 1 else None)
    return xs
```

```python
import functools
import math

import jax
import jax.numpy as jnp
from jax import lax
from jax.experimental import pallas as pl
from jax.experimental.pallas import tpu as pltpu

f32 = jnp.float32
bf16 = jnp.bfloat16

D = 1024
BATCH = 8
SEQ = 2048
DEPTH = 4
GRID_W = 64
CTX = 256
S = SEQ + CTX
EPS = 1e-6
N_MOD = 6
MOD_ROWS = 16
CTX_MOD = 8

HEADS = 16
HEAD_P = 64
STATE = 128
GROUPS = 2
CHUNK = 128
XBC = D + 2 * GROUPS * STATE
ZX = D + XBC
DT_LANES = 128
HY_BANDS = 16
HY_EMB = 1 + 2 * HY_BANDS
HY_FFN = 64
COL_XBC = D
COL_DT = COL_XBC + XBC
COL_HY = COL_DT + 2 * HEADS

Q_HEADS = 16
KV_HEADS = 4
ATT_GROUP = Q_HEADS // KV_HEADS
QW = Q_HEADS * HEAD_P
KVW = KV_HEADS * HEAD_P
WINDOW = 128
ABLK = 128
ROPE_BASE = 10000.0
FFN_H = -(-8 * D // (3 * 256)) * 256

RC = 256
NMM_RC = 768
VMEM_CAP = 56 * 1024 * 1024


def _cparams(sem, vmem_mb):
    return pltpu.CompilerParams(dimension_semantics=sem,
                                vmem_limit_bytes=min(int(vmem_mb * 1024 * 1024), VMEM_CAP))


def _silu(v):
    return v * jax.nn.sigmoid(v)


def _softplus(v):
    return jnp.maximum(v, 0.0) + jnp.log1p(jnp.exp(-jnp.abs(v)))


def _bdot(a, b):
    return jnp.dot(a, b, preferred_element_type=f32)


def _bdot_nt(a, b):
    return lax.dot_general(a, b, (((1,), (1,)), ((), ())), preferred_element_type=f32)


def _row_chunks(n_rows, body, unroll=1):
    def step(c, carry):
        body(pl.multiple_of(c * RC, RC))
        return carry
    lax.fori_loop(0, n_rows // RC, step, 0, unroll=unroll)


def _norm_mod(x, g, mod, shift_row, scale_row, row0):
    ms = jnp.mean(x * x, axis=-1, keepdims=True)
    scale = _mod_row(mod, scale_row, row0)
    return x * lax.rsqrt(ms + EPS) * (g * (1.0 + scale)) + _mod_row(mod, shift_row, row0)


def _mod_row(mod, k, row0):
    return jnp.where(row0 >= SEQ, mod[CTX_MOD + k:CTX_MOD + k + 1], mod[k:k + 1])


def _norm_mod_to_scratch(x_ref, g_ref, mod_ref, h_scr):
    def body(r0):
        h_scr[pl.ds(r0, RC), :] = _norm_mod(x_ref[0, pl.ds(r0, RC), :], g_ref[...], mod_ref[0],
                                            0, 1, r0).astype(bf16)
    _row_chunks(S, body)


CONV_ROWS = 64


def _conv3_rows(p_scr, r0, cw):
    n, cols = CONV_ROWS, p_scr.shape[1]
    zero = jnp.zeros((8, cols), f32)
    tiles = [p_scr[r0 + k:r0 + k + 8, :] for k in range(0, n, 8)]
    above = zero if r0 in (0, SEQ) else p_scr[r0 - 8:r0, :]
    below = zero if r0 + n in (SEQ, S) else p_scr[r0 + n:r0 + n + 8, :]
    sub = lax.broadcasted_iota(jnp.int32, (8, cols), 0)
    down = [pltpu.roll(t, 1, 0) for t in [above] + tiles]
    up = [pltpu.roll(t, 7, 0) for t in tiles + [below]]
    out = []
    for k, t in enumerate(tiles):
        prev = jnp.where(sub == 0, down[k], down[k + 1])
        nxt = jnp.where(sub == 7, up[k + 1], up[k])
        out.append(prev * cw[0:1] + t * cw[1:2] + nxt * cw[2:3] + cw[3:4])
    return jnp.concatenate(out, axis=0)


def _project_then_conv(mm, piece):
    done = 0
    for r0 in range(0, S, NMM_RC):
        while done + CONV_ROWS < r0:
            piece(done)
            done += CONV_ROWS
        mm(r0)
    while done < S:
        piece(done)
        done += CONV_ROWS


def _mod_kernel(c_ref, w_ref, b_ref, o_ref):
    sc = _silu(c_ref[...]).astype(bf16)
    o_ref[0] = _bdot(sc, w_ref[0].astype(bf16)) + b_ref[0]


def _mod_call(cc, ada_w, ada_b):
    tn = 1024
    return pl.pallas_call(
        _mod_kernel,
        grid=(DEPTH, N_MOD * D // tn),
        in_specs=[pl.BlockSpec((MOD_ROWS, D), lambda l, j: (0, 0)),
                  pl.BlockSpec((1, D, tn), lambda l, j: (l, 0, j)),
                  pl.BlockSpec((1, 1, tn), lambda l, j: (l, 0, j))],
        out_specs=pl.BlockSpec((1, MOD_ROWS, tn), lambda l, j: (l, 0, j)),
        out_shape=jax.ShapeDtypeStruct((DEPTH, MOD_ROWS, N_MOD * D), f32),
        compiler_params=_cparams(("parallel", "parallel"), 24),
        name="adaln_mod",
    )(cc, ada_w, ada_b.reshape(DEPTH, 1, N_MOD * D))


def _nmm_kernel(x_ref, mod_ref, g_ref, w_ref, o_ref, h_scr):
    @pl.when(pl.program_id(1) == 0)
    def _():
        _norm_mod_to_scratch(x_ref, g_ref, mod_ref, h_scr)

    def body(c, carry):
        r0 = pl.multiple_of(c * NMM_RC, NMM_RC)
        o_ref[0, pl.ds(r0, NMM_RC), :] = _bdot(h_scr[pl.ds(r0, NMM_RC), :], w_ref[...]).astype(o_ref.dtype)
        return carry
    lax.fori_loop(0, S // NMM_RC, body, 0)


def _nmm_call(xs, mods_l, g, w, tn, name):
    n = w.shape[1]
    return pl.pallas_call(
        _nmm_kernel,
        grid=(BATCH, n // tn),
        in_specs=[pl.BlockSpec((1, S, D), lambda b, j: (b, 0, 0)),
                  pl.BlockSpec((1, MOD_ROWS, D), lambda b, j: (b, 0, 0)),
                  pl.BlockSpec((1, D), lambda b, j: (0, 0)),
                  pl.BlockSpec((D, tn), lambda b, j: (0, j))],
        out_specs=pl.BlockSpec((1, S, tn), lambda b, j: (b, 0, j)),
        out_shape=jax.ShapeDtypeStruct((BATCH, S, n), bf16),
        scratch_shapes=[pltpu.VMEM((S, D), bf16)],
        compiler_params=_cparams(("parallel", "arbitrary"), 40),
        name=name,
    )(xs, mods_l, g, w)


ZX_TN = 512
NZ_T = D // ZX_TN


def _inproj_ssd_kernel(x_ref, mod_ref, g_ref, w_ref, cw_ref, wdt_ref, dtb_ref, zx_ref, dt_ref,
                       h_scr, p_scr):
    j = pl.program_id(1)

    @pl.when(j == 0)
    def _():
        _norm_mod_to_scratch(x_ref, g_ref, mod_ref, h_scr)

        def dt_body(c, carry):
            r0 = pl.multiple_of(c * NMM_RC, NMM_RC)
            sp = _softplus(_bdot(h_scr[pl.ds(r0, NMM_RC), :], wdt_ref[...]) + dtb_ref[...])
            lane = lax.broadcasted_iota(jnp.int32, sp.shape, 1)
            dt_ref[0, pl.ds(r0, NMM_RC), :] = jnp.where(lane < 2 * HEADS, sp, 0.0)
            return carry
        lax.fori_loop(0, S // NMM_RC, dt_body, 0)

    @pl.when(j < NZ_T)
    def _():
        def body(c, carry):
            r0 = pl.multiple_of(c * NMM_RC, NMM_RC)
            zx_ref[0, pl.ds(r0, NMM_RC), :] = _bdot(h_scr[pl.ds(r0, NMM_RC), :], w_ref[...]).astype(bf16)
            return carry
        lax.fori_loop(0, S // NMM_RC, body, 0)

    @pl.when(j >= NZ_T)
    def _():
        def mm(r0):
            p_scr[r0:r0 + NMM_RC, :] = _bdot(h_scr[r0:r0 + NMM_RC, :], w_ref[...])

        def piece(a):
            zx_ref[0, a:a + CONV_ROWS, :] = _silu(_conv3_rows(p_scr, a, cw_ref[...])).astype(bf16)
        _project_then_conv(mm, piece)


def _inproj_ssd_call(xs, mods_l, g, w_zx, cw, wdt, dtb):
    return pl.pallas_call(
        _inproj_ssd_kernel,
        grid=(BATCH, ZX // ZX_TN),
        in_specs=[pl.BlockSpec((1, S, D), lambda b, j: (b, 0, 0)),
                  pl.BlockSpec((1, MOD_ROWS, D), lambda b, j: (b, 0, 0)),
                  pl.BlockSpec((1, D), lambda b, j: (0, 0)),
                  pl.BlockSpec((D, ZX_TN), lambda b, j: (0, j)),
                  pl.BlockSpec((8, ZX_TN), lambda b, j: (0, j)),
                  pl.BlockSpec((D, DT_LANES), lambda b, j: (0, 0)),
                  pl.BlockSpec((1, DT_LANES), lambda b, j: (0, 0))],
        out_specs=[pl.BlockSpec((1, S, ZX_TN), lambda b, j: (b, 0, j)),
                   pl.BlockSpec((1, S, DT_LANES), lambda b, j: (b, 0, 0))],
        out_shape=[jax.ShapeDtypeStruct((BATCH, S, ZX), bf16),
                   jax.ShapeDtypeStruct((BATCH, S, DT_LANES), f32)],
        scratch_shapes=[pltpu.VMEM((S, D), bf16), pltpu.VMEM((S, ZX_TN), f32)],
        compiler_params=_cparams(("parallel", "arbitrary"), 48),
        name="inproj_ssd",
    )(xs, mods_l, g, w_zx, cw, wdt, dtb)


HY_TN = 256


def _inproj_hy_kernel(x_ref, mod_ref, g_ref, w0_ref, w1_ref, w2_ref, c0_ref, c1_ref, c2_ref,
                      x0_ref, u_ref, h_scr, p0_scr, p1_scr, p2_scr):
    @pl.when(pl.program_id(1) == 0)
    def _():
        _norm_mod_to_scratch(x_ref, g_ref, mod_ref, h_scr)

    def mm_x0(r0):
        p0_scr[r0:r0 + NMM_RC, :] = _bdot(h_scr[r0:r0 + NMM_RC, :], w0_ref[...])

    def piece_x0(a):
        x0_ref[0, a:a + CONV_ROWS, :] = _conv3_rows(p0_scr, a, c0_ref[...]).astype(bf16)
    _project_then_conv(mm_x0, piece_x0)

    def mm_u(r0):
        p1_scr[r0:r0 + NMM_RC, :] = _bdot(h_scr[r0:r0 + NMM_RC, :], w1_ref[...])
        p2_scr[r0:r0 + NMM_RC, :] = _bdot(h_scr[r0:r0 + NMM_RC, :], w2_ref[...])

    def piece_u(a):
        x1 = _conv3_rows(p1_scr, a, c1_ref[...])
        v = _conv3_rows(p2_scr, a, c2_ref[...])
        u_ref[0, a:a + CONV_ROWS, :] = (x1 * v).astype(bf16)
    _project_then_conv(mm_u, piece_u)


def _inproj_hy_call(xs, mods_l, g, w_hy, cw_hy):
    nt = D // HY_TN
    wspec = lambda k: pl.BlockSpec((D, HY_TN), lambda b, j, k=k: (0, j + k * nt))
    cspec = lambda k: pl.BlockSpec((8, HY_TN), lambda b, j, k=k: (0, j + k * nt))
    return pl.pallas_call(
        _inproj_hy_kernel,
        grid=(BATCH, nt),
        in_specs=[pl.BlockSpec((1, S, D), lambda b, j: (b, 0, 0)),
                  pl.BlockSpec((1, MOD_ROWS, D), lambda b, j: (b, 0, 0)),
                  pl.BlockSpec((1, D), lambda b, j: (0, 0)),
                  wspec(0), wspec(1), wspec(2), cspec(0), cspec(1), cspec(2)],
        out_specs=[pl.BlockSpec((1, S, HY_TN), lambda b, j: (b, 0, j)),
                   pl.BlockSpec((1, S, HY_TN), lambda b, j: (b, 0, j))],
        out_shape=[jax.ShapeDtypeStruct((BATCH, S, D), bf16),
                   jax.ShapeDtypeStruct((BATCH, S, D), bf16)],
        scratch_shapes=[pltpu.VMEM((S, D), bf16)] + [pltpu.VMEM((S, HY_TN), f32)] * 3,
        compiler_params=_cparams(("parallel", "arbitrary"), 48),
        name="inproj_hyena",
    )(xs, mods_l, g, w_hy, w_hy, w_hy, cw_hy, cw_hy, cw_hy)


N_CHUNK = S // CHUNK
LAT_CHUNKS = SEQ // CHUNK


def _split3_dot(t, a):
    a1 = a.astype(bf16)
    r1 = a - a1.astype(f32)
    a2 = r1.astype(bf16)
    a3 = (r1 - a2.astype(f32)).astype(bf16)
    return _bdot(t, a1) + _bdot(t, a2) + _bdot(t, a3)


def _ssd_chunk(zx_ref, dt_ref, aneg, e_ref, y_scr, st_scr, r0, backward):
    lane_off = HEADS if backward else 0
    x = zx_ref[0, pl.ds(r0, CHUNK), D:2 * D].astype(f32)
    bm = zx_ref[0, pl.ds(r0, CHUNK), 2 * D:2 * D + GROUPS * STATE]
    cm = zx_ref[0, pl.ds(r0, CHUNK), 2 * D + GROUPS * STATE:ZX]
    dt = dt_ref[0, pl.ds(r0, CHUNK), :]
    a = dt * aneg

    ri = lax.broadcasted_iota(jnp.int32, (CHUNK, CHUNK), 0)
    ci = lax.broadcasted_iota(jnp.int32, (CHUNK, CHUNK), 1)
    causal = (ci >= ri) if backward else (ci <= ri)
    tmat = jnp.where(causal, 1.0, 0.0).astype(bf16)
    acum = _split3_dot(tmat, a)
    acum_t = acum.T
    tot = acum[0:1] if backward else acum[CHUNK - 1:CHUNK]
    dte = jnp.exp2(tot - acum)
    eac = jnp.exp2(acum)
    cdec = jnp.broadcast_to(jnp.exp2(tot), (16, DT_LANES))

    e = e_ref[...]
    dt_x = _bdot(dt.astype(bf16), e)
    dte_x = _bdot(dte.astype(bf16), e)
    eac_x = _bdot(eac.astype(bf16), e)
    cdec_x = _bdot(cdec.astype(bf16), e)[0:1]

    xdt = x * dt_x
    xdt_b = xdt.astype(bf16)
    xs_b = (xdt * dte_x).astype(bf16)
    lane = lax.broadcasted_iota(jnp.int32, (CHUNK, 2 * HEAD_P), 1)
    low = lane < HEAD_P
    hg = HEADS // GROUPS
    gw = hg * HEAD_P
    for g in range(GROUPS):
        bg = bm[:, g * STATE:(g + 1) * STATE]
        cg = cm[:, g * STATE:(g + 1) * STATE]
        cb = _bdot_nt(cg, bg)
        hin = st_scr[:, g * gw:(g + 1) * gw]
        yoff = _bdot(cg, hin.astype(bf16)) * eac_x[:, g * gw:(g + 1) * gw]
        for hp in range(hg // 2):
            h0 = g * hg + 2 * hp
            ms = []
            for h in (h0, h0 + 1):
                l0 = lane_off + h
                seg = acum[:, l0:l0 + 1] - acum_t[l0:l0 + 1, :]
                ms.append(cb * jnp.exp2(jnp.where(causal, seg, -jnp.inf)))
            lhs = jnp.concatenate(ms, axis=1).astype(bf16)
            xp = xdt[:, h0 * HEAD_P:(h0 + 2) * HEAD_P]
            rhs = jnp.concatenate([jnp.where(low, xp, 0.0), jnp.where(low, 0.0, xp)],
                                  axis=0).astype(bf16)
            yd = _bdot(lhs, rhs) + yoff[:, hp * 2 * HEAD_P:(hp + 1) * 2 * HEAD_P]
            cols = slice(h0 * HEAD_P, (h0 + 2) * HEAD_P)
            y_scr[pl.ds(r0, CHUNK), cols] = y_scr[pl.ds(r0, CHUNK), cols] + yd
        bg_t = bg.astype(f32).T.astype(bf16)
        st_scr[:, g * gw:(g + 1) * gw] = (hin * cdec_x[:, g * gw:(g + 1) * gw]
                                         + _bdot(bg_t, xs_b[:, g * gw:(g + 1) * gw]))
    del xdt_b


def _ssd_kernel(zx_ref, dt_ref, alog_ref, dskip_ref, ng_ref, ef_ref, eb_ref, o_ref,
                y_scr, sf_scr, sb_scr):
    aneg = -jnp.exp(alog_ref[...]) * LOG2E
    sf_scr[...] = jnp.zeros_like(sf_scr)
    sb_scr[...] = jnp.zeros_like(sb_scr)

    def init(c, carry):
        r0 = pl.multiple_of(c * CHUNK, CHUNK)
        y_scr[pl.ds(r0, CHUNK), :] = dskip_ref[...] * zx_ref[0, pl.ds(r0, CHUNK), D:2 * D].astype(f32)
        return carry
    lax.fori_loop(0, N_CHUNK, init, 0)

    def step(t, carry):
        cf = jnp.where(t < N_CHUNK - LAT_CHUNKS, LAT_CHUNKS + t, t - (N_CHUNK - LAT_CHUNKS))
        cbk = N_CHUNK - 1 - t
        _ssd_chunk(zx_ref, dt_ref, aneg, ef_ref, y_scr, sf_scr, pl.multiple_of(cf * CHUNK, CHUNK), False)
        _ssd_chunk(zx_ref, dt_ref, aneg, eb_ref, y_scr, sb_scr, pl.multiple_of(cbk * CHUNK, CHUNK), True)
        return carry
    lax.fori_loop(0, N_CHUNK, step, 0, unroll=2)

    def fin(c, carry):
        r0 = pl.multiple_of(c * CHUNK, CHUNK)
        t = y_scr[pl.ds(r0, CHUNK), :] * _silu(zx_ref[0, pl.ds(r0, CHUNK), 0:D].astype(f32))
        ms = jnp.mean(t * t, axis=-1, keepdims=True)
        o_ref[0, pl.ds(r0, CHUNK), :] = (t * lax.rsqrt(ms + EPS) * ng_ref[...]).astype(bf16)
        return carry
    lax.fori_loop(0, N_CHUNK, fin, 0)


def _ssd_call(zx, dt, alog, dskip, ng, e_f, e_b):
    return pl.pallas_call(
        _ssd_kernel,
        grid=(BATCH,),
        in_specs=[pl.BlockSpec((1, S, ZX), lambda b: (b, 0, 0)),
                  pl.BlockSpec((1, S, DT_LANES), lambda b: (b, 0, 0)),
                  pl.BlockSpec((1, DT_LANES), lambda b: (0, 0)),
                  pl.BlockSpec((1, D), lambda b: (0, 0)),
                  pl.BlockSpec((1, D), lambda b: (0, 0)),
                  pl.BlockSpec((DT_LANES, D), lambda b: (0, 0)),
                  pl.BlockSpec((DT_LANES, D), lambda b: (0, 0))],
        out_specs=pl.BlockSpec((1, S, D), lambda b: (b, 0, 0)),
        out_shape=jax.ShapeDtypeStruct((BATCH, S, D), bf16),
        scratch_shapes=[pltpu.VMEM((S, D), f32),
                        pltpu.VMEM((STATE, D), f32),
                        pltpu.VMEM((STATE, D), f32)],
        compiler_params=_cparams(("parallel",), 56),
        name="ssd_scan",
    )(zx, dt, alog, dskip, ng, e_f, e_b)


def _hp_dot(a, b):
    return jnp.dot(a, b, precision=lax.Precision.HIGHEST, preferred_element_type=f32)


def _hyfilter_kernel(z_ref, w1_ref, b1_ref, w2_ref, b2_ref, w3_ref, b3_ref, fr_ref, w4f_ref, w4b_ref,
                     win_ref, w_ref, ka_ref, kb_ref, h_scr, tap_scr, tb_scr, pp_scr, pq_scr, *, seg, nb):
    lb = seg // nb
    tc = tap_scr.shape[1]

    @pl.when(pl.program_id(0) == 0)
    def _():
        fr = fr_ref[...]

        def hidden(c, carry):
            r0 = pl.multiple_of(c * RC, RC)
            h = jnp.sin(fr[0:1] * (_hp_dot(z_ref[pl.ds(r0, RC), :], w1_ref[...]) + b1_ref[...]))
            h = jnp.sin(fr[1:2] * (_hp_dot(h, w2_ref[...]) + b2_ref[...]))
            h_scr[pl.ds(r0, RC), :] = jnp.sin(fr[2:3] * (_hp_dot(h, w3_ref[...]) + b3_ref[...]))
            return carry
        lax.fori_loop(0, 2 * seg // RC, hidden, 0)

    def taps(w4_ref):
        def body(c, ssq):
            r0 = pl.multiple_of(c * RC, RC)
            t = _hp_dot(h_scr[pl.ds(r0, RC), :], w4_ref[...]) * win_ref[pl.ds(r0, RC), :]
            row = r0 + lax.broadcasted_iota(jnp.int32, t.shape, 0)
            t = jnp.where(row == 0, 0.0, t)
            tap_scr[pl.ds(r0, RC), :] = t
            return ssq + jnp.sum(t * t, axis=0, keepdims=True)
        return body
    ssq = lax.fori_loop(0, seg // RC, taps(w4b_ref), jnp.zeros((1, tc), f32))
    nrm = lax.rsqrt(lax.fori_loop(seg // RC, 2 * seg // RC, taps(w4f_ref), ssq) + 1e-6)

    def scale(c, carry):
        r0 = pl.multiple_of(c * RC, RC)
        tb_scr[pl.ds(r0, RC), :] = (tap_scr[pl.ds(r0, RC), :] * nrm).astype(bf16)
        return carry
    lax.fori_loop(0, 2 * seg // RC, scale, 0)

    row = lax.broadcasted_iota(jnp.int32, (lb, tc), 0)
    sgn = jnp.where((row & 1) == 0, 1.0, -1.0)
    for s in range(2 * nb):
        blk = tb_scr[s * lb:(s + 1) * lb, :]
        p = _bdot(w_ref[0:lb, :], blk)
        q = _bdot(w_ref[lb:2 * lb, :], blk)
        if s > 0:
            t0 = tb_scr[(s - 1) * lb:(s - 1) * lb + 16, :][0:1].astype(f32)
            ka_ref[s - 1] = (p + sgn * (pp_scr[...] - t0)).astype(ka_ref.dtype)
            kb_ref[s - 1] = jnp.where(row == 0, q + pq_scr[...] - t0,
                                      q + sgn * pq_scr[...]).astype(kb_ref.dtype)
        pp_scr[...] = p
        pq_scr[...] = q


HY_TC = 256
HY_NB = 4
HY_SPEC_DT = bf16


def _hyfilter_call(zt, w1, b1, w2, b2, w3, b3, fr, w4, win, wtab, seg, nb):
    tc, lb = HY_TC, seg // nb
    nct = D // tc
    const = lambda shape: pl.BlockSpec(shape, lambda c: (0, 0))
    return pl.pallas_call(
        functools.partial(_hyfilter_kernel, seg=seg, nb=nb),
        grid=(nct,),
        in_specs=[const((2 * seg, 128)), const((128, HY_FFN)), const((1, HY_FFN)),
                  const((HY_FFN, HY_FFN)), const((1, HY_FFN)),
                  const((HY_FFN, HY_FFN)), const((1, HY_FFN)), const((8, HY_FFN)),
                  pl.BlockSpec((HY_FFN, tc), lambda c: (0, c)),
                  pl.BlockSpec((HY_FFN, tc), lambda c: (0, c + nct)),
                  pl.BlockSpec((2 * seg, tc), lambda c: (0, c)),
                  const((2 * lb, lb))],
        out_specs=[pl.BlockSpec((2 * nb - 1, lb, tc), lambda c: (0, 0, c)),
                   pl.BlockSpec((2 * nb - 1, lb, tc), lambda c: (0, 0, c))],
        out_shape=[jax.ShapeDtypeStruct((2 * nb - 1, lb, D), HY_SPEC_DT)] * 2,
        scratch_shapes=[pltpu.VMEM((2 * seg, HY_FFN), f32),
                        pltpu.VMEM((2 * seg, tc), f32), pltpu.VMEM((2 * seg, tc), bf16),
                        pltpu.VMEM((lb, tc), f32), pltpu.VMEM((lb, tc), f32)],
        compiler_params=_cparams(("arbitrary",), 40),
        name=f"hyena_filter_{seg}",
    )(zt, w1, b1, w2, b2, w3, b3, fr, w4, w4, win, wtab)


HY_PIECE = 64


def _hyconv_segment(u_ref, x0_ref, bias_ref, o_ref, w_ref, wt_ref, ka_ref, kb_ref,
                    ua_scr, ub_scr, ya_scr, yb_scr, row0, nb, lb):
    for i in range(nb):
        ui = u_ref[0, row0 + i * lb:row0 + (i + 1) * lb, :]
        ua_scr[i, 0:lb, :] = _bdot(w_ref[0:lb, :], ui).astype(HY_SPEC_DT)
        ub_scr[i, 0:lb, :] = _bdot(w_ref[lb:2 * lb, :], ui).astype(HY_SPEC_DT)

    is0 = lax.broadcasted_iota(jnp.int32, (HY_PIECE, ka_ref.shape[2]), 0) == 0
    for o in range(nb):
        for r in range(0, lb, HY_PIECE):
            rows = slice(r, r + HY_PIECE)
            ya = yb = None
            for i in range(nb):
                a, b = ua_scr[i, rows, :], ub_scr[i, rows, :]
                ka, kb = ka_ref[o - i + nb - 1, rows, :], kb_ref[o - i + nb - 1, rows, :]
                if r == 0:
                    pa = jnp.where(is0, a * ka, a * ka - b * kb)
                    pb = jnp.where(is0, b * kb, a * kb + b * ka)
                else:
                    pa, pb = a * ka - b * kb, a * kb + b * ka
                ya = pa if ya is None else ya + pa
                yb = pb if yb is None else yb + pb
            scale = jnp.where(is0, 0.5 / lb, 1.0 / lb) if r == 0 else 1.0 / lb
            ya_scr[o, rows, :] = (ya * scale).astype(bf16)
            yb_scr[o, rows, :] = (yb * scale).astype(bf16)
        y = _bdot(wt_ref[:, 0:lb], ya_scr[o, 0:lb, :]) + _bdot(wt_ref[:, lb:2 * lb], yb_scr[o, 0:lb, :])
        rows = slice(row0 + o * lb, row0 + (o + 1) * lb)
        y = y + bias_ref[...] * u_ref[0, rows, :].astype(f32)
        o_ref[0, rows, :] = (x0_ref[0, rows, :].astype(f32) * y).astype(bf16)


def _hyconv_kernel(u_ref, x0_ref, bias_ref, w_ref, wt_ref, ka_ref, kb_ref, cw_ref, cwt_ref, cka_ref,
                   ckb_ref, o_ref, ua_scr, ub_scr, ya_scr, yb_scr):
    scr = (ua_scr, ub_scr, ya_scr, yb_scr)
    _hyconv_segment(u_ref, x0_ref, bias_ref, o_ref, w_ref, wt_ref, ka_ref, kb_ref, *scr,
                    row0=0, nb=HY_NB, lb=SEQ // HY_NB)
    _hyconv_segment(u_ref, x0_ref, bias_ref, o_ref, cw_ref, cwt_ref, cka_ref, ckb_ref, *scr,
                    row0=SEQ, nb=1, lb=CTX)


def _hyconv_call(u, x0, bias, lat, ctx):
    tc, lb = HY_TC, SEQ // HY_NB
    full = lambda a: pl.BlockSpec(a.shape, lambda c, b: (0,) * a.ndim)
    spec = lambda a: pl.BlockSpec(a.shape[:2] + (tc,), lambda c, b: (0, 0, c))
    seq = pl.BlockSpec((1, S, tc), lambda c, b: (b, 0, c))
    return pl.pallas_call(
        _hyconv_kernel,
        grid=(D // tc, BATCH),
        in_specs=[seq, seq, pl.BlockSpec((1, tc), lambda c, b: (0, c)),
                  full(lat[0]), full(lat[1]), spec(lat[2]), spec(lat[3]),
                  full(ctx[0]), full(ctx[1]), spec(ctx[2]), spec(ctx[3])],
        out_specs=seq,
        out_shape=jax.ShapeDtypeStruct((BATCH, S, D), bf16),
        scratch_shapes=[pltpu.VMEM((HY_NB, lb, tc), HY_SPEC_DT), pltpu.VMEM((HY_NB, lb, tc), HY_SPEC_DT),
                        pltpu.VMEM((HY_NB, lb, tc), bf16), pltpu.VMEM((HY_NB, lb, tc), bf16)],
        compiler_params=_cparams(("parallel", "parallel"), 40),
        name="hyena_conv",
    )(u, x0, bias, *lat, *ctx)


def _rope(xb, perm, cos_t, sin_t):
    n = xb.shape[1]
    return xb.astype(f32) * cos_t + _bdot(xb, perm[0:n, 0:n]) * sin_t


assert WINDOW == ABLK
LOG2E = math.log2(math.e)
A_SPAN = ABLK + 2 * WINDOW
A_KEYS = A_SPAN + CTX
A_COLS = ATT_GROUP * ABLK


def _attn_kernel(sink_ref, q_ref, kv_ref, cos_ref, sin_ref, perm_ref, o_ref, q_scr, k_scr, vt_scr,
                 s_scr, p_scr):
    g = pl.program_id(1)
    qscale = HEAD_P ** -0.5 * LOG2E

    def prep(r0, with_rope):
        qb = q_ref[0, pl.ds(r0, RC), :]
        kvb = kv_ref[0, pl.ds(r0, RC), :]
        kv = kvb.astype(f32)
        if with_rope:
            cos_t, sin_t = cos_ref[pl.ds(r0, RC), :], sin_ref[pl.ds(r0, RC), :]
            q = _rope(qb, perm_ref[...], cos_t, sin_t) * qscale
            k = _rope(kvb, perm_ref[...], cos_t[:, 0:2 * HEAD_P], sin_t[:, 0:2 * HEAD_P])
        else:
            q, k = qb.astype(f32) * qscale, kv
        low = lax.broadcasted_iota(jnp.int32, kv.shape, 1) < HEAD_P
        k_scr[pl.ds(r0, RC), :] = jnp.where(low, k, 0.0).astype(bf16)
        for r in range(ATT_GROUP):
            slab = q[:, (r // 2) * 2 * HEAD_P:(r // 2 + 1) * 2 * HEAD_P]
            if r % 2:
                slab = pltpu.roll(slab, HEAD_P, 1)
            q_scr[r, pl.ds(r0, RC), :] = jnp.where(low, slab, 0.0).astype(bf16)
        v_t = kv.T[HEAD_P:2 * HEAD_P, :].astype(bf16)
        c0 = r0 // ABLK
        for h in range(RC // ABLK):
            vt_scr[c0 + h] = v_t[:, h * ABLK:(h + 1) * ABLK]
    _row_chunks(SEQ, lambda r0: prep(r0, True), unroll=2)
    for r0 in range(SEQ, S, RC):
        prep(r0, False)

    kmq = (lax.broadcasted_iota(jnp.int32, (ABLK, A_COLS), 0)
           - (lax.broadcasted_iota(jnp.int32, (ABLK, A_COLS), 1) & (ABLK - 1)))

    def block(q0, start, masks, slot):
        qstack = jnp.concatenate([q_scr[r, pl.ds(q0, ABLK), :] for r in range(ATT_GROUP)], axis=0)
        chunks = []
        if masks:
            s_loc = _bdot_nt(k_scr[pl.ds(start, len(masks) * ABLK), :], qstack)
            for c, kind in enumerate(masks):
                s_c = s_loc[c * ABLK:(c + 1) * ABLK]
                if kind is not None:
                    s_c = jnp.where(kmq >= 0 if kind == 'ge' else kmq <= 0, s_c, -jnp.inf)
                s_scr[slot, c * ABLK:(c + 1) * ABLK, :] = s_c
                chunks.append((c * ABLK, start // ABLK + c))
        n_loc = len(chunks) * ABLK
        s_scr[slot, n_loc:n_loc + CTX, :] = _bdot_nt(k_scr[SEQ:S, :], qstack)
        chunks += [(n_loc + c * ABLK, SEQ // ABLK + c) for c in range(CTX // ABLK)]

        inv = []
        for r in range(ATT_GROUP):
            cols = slice(r * ABLK, (r + 1) * ABLK)
            sink = sink_ref[g * ATT_GROUP + r] * LOG2E
            m = jnp.full((1, ABLK), sink, f32)
            for row, _ in chunks:
                m = jnp.maximum(m, jnp.max(s_scr[slot, row:row + ABLK, cols], axis=0, keepdims=True))
            den = jnp.exp2(sink - m)
            for row, _ in chunks:
                p = jnp.exp2(s_scr[slot, row:row + ABLK, cols] - m)
                den = den + jnp.sum(p, axis=0, keepdims=True)
                p_scr[slot, row:row + ABLK, cols] = p.astype(bf16)
            inv.append(1.0 / den)

        o_t = None
        for row, vc in chunks:
            part = _bdot(vt_scr[vc], p_scr[slot, row:row + ABLK, :])
            o_t = part if o_t is None else o_t + part
        o_t = jnp.concatenate([o_t[:, r * ABLK:(r + 1) * ABLK] * inv[r] for r in range(ATT_GROUP)], axis=0)
        o_ref[0, pl.ds(q0, ABLK), :] = o_t.T.astype(bf16)

    n_blk = SEQ // ABLK
    block(0, 0, (None, 'le'), 0)
    block(SEQ - ABLK, SEQ - 2 * ABLK, ('ge', None), 1)

    def lat_pair(j, carry):
        for slot in range(2):
            q0 = pl.multiple_of((1 + 2 * j + slot) * ABLK, ABLK)
            block(q0, pl.multiple_of(q0 - WINDOW, ABLK), ('ge', None, 'le'), slot)
        return carry
    lax.fori_loop(0, (n_blk - 2) // 2, lat_pair, 0, unroll=True)

    for slot, q0 in enumerate(range(SEQ, o_ref.shape[1], ABLK)):
        block(q0, 0, (), slot % 2)


def _attn_call(qkv, sinks, cos_t, sin_t, perm, need_ctx):
    gw = ATT_GROUP * HEAD_P
    rows = S if need_ctx else SEQ
    return pl.pallas_call(
        _attn_kernel,
        grid_spec=pltpu.PrefetchScalarGridSpec(
            num_scalar_prefetch=1,
            grid=(BATCH, KV_HEADS),
            in_specs=[pl.BlockSpec((1, S, gw), lambda b, g, s: (b, 0, g)),
                      pl.BlockSpec((1, S, 2 * HEAD_P), lambda b, g, s: (b, 0, QW // (2 * HEAD_P) + g)),
                      pl.BlockSpec((SEQ, gw), lambda b, g, s: (0, 0)),
                      pl.BlockSpec((SEQ, gw), lambda b, g, s: (0, 0)),
                      pl.BlockSpec((gw, gw), lambda b, g, s: (0, 0))],
            out_specs=pl.BlockSpec((1, rows, gw), lambda b, g, s: (b, 0, g)),
            scratch_shapes=[pltpu.VMEM((ATT_GROUP, S, 2 * HEAD_P), bf16),
                            pltpu.VMEM((S, 2 * HEAD_P), bf16),
                            pltpu.VMEM((S // ABLK, HEAD_P, ABLK), bf16),
                            pltpu.VMEM((2, A_KEYS, A_COLS), f32),
                            pltpu.VMEM((2, A_KEYS, A_COLS), bf16)]),
        out_shape=jax.ShapeDtypeStruct((BATCH, rows, QW), bf16),
        compiler_params=_cparams(("parallel", "parallel"), 40),
        name="window_attn",
    )(sinks, qkv, qkv, cos_t, sin_t, perm)


RES_TM = 768
RES_TM_LAT = 1024


def _outproj_kernel(x_ref, mod_ref, *rest, n_in, tm):
    a_refs, w_refs, o_ref = rest[:n_in], rest[n_in:2 * n_in], rest[2 * n_in]
    acc = _bdot(a_refs[0][0], w_refs[0][...])
    for a_ref, w_ref in zip(a_refs[1:], w_refs[1:]):
        acc = acc + _bdot(a_ref[0], w_ref[...])
    for r0 in range(0, tm, RC):
        gate = _mod_row(mod_ref[0], 2, pl.program_id(1) * tm + r0)
        o_ref[0, r0:r0 + RC, :] = x_ref[0, r0:r0 + RC, :] + gate * acc[r0:r0 + RC]


def _outproj_call(xs, mods_l, acts, ws, name):
    n_in = len(acts)
    rows = acts[0].shape[1]
    tm = RES_TM if rows == S else RES_TM_LAT
    in_specs = [pl.BlockSpec((1, tm, D), lambda b, i: (b, i, 0)),
                pl.BlockSpec((1, MOD_ROWS, D), lambda b, i: (b, 0, 0))]
    in_specs += [pl.BlockSpec((1, tm, a.shape[2]), lambda b, i: (b, i, 0)) for a in acts]
    in_specs += [pl.BlockSpec(w.shape, lambda b, i: (0, 0)) for w in ws]
    return pl.pallas_call(
        functools.partial(_outproj_kernel, n_in=n_in, tm=tm),
        grid=(BATCH, rows // tm),
        in_specs=in_specs,
        out_specs=pl.BlockSpec((1, tm, D), lambda b, i: (b, i, 0)),
        out_shape=jax.ShapeDtypeStruct((BATCH, S, D), f32),
        input_output_aliases={0: 0},
        compiler_params=_cparams(("parallel", "parallel"), 40),
        name=name,
    )(xs, mods_l, *acts, *ws)


FFN_TM = 1152
FFN_TM_LAST = 1024
FFN_RC = 128
FFN_TH = 256
FFN_NH = FFN_H // FFN_TH


def _ffn_kernel(x_ref, mod_ref, g_ref, win_ref, wo_ref, *rest, tm, final):
    gf_ref = rest[0] if final else None
    o_ref, h_scr, acc = rest[-3:]
    row0 = pl.program_id(1) * tm
    for r0 in range(0, tm, FFN_RC):
        h_scr[r0:r0 + FFN_RC, :] = _norm_mod(x_ref[0, r0:r0 + FFN_RC, :], g_ref[...], mod_ref[0],
                                             3, 4, row0 + r0).astype(bf16)
    for k in range(FFN_NH):
        gate_cols = slice(k * FFN_TH, (k + 1) * FFN_TH)
        up_cols = slice(FFN_H + k * FFN_TH, FFN_H + (k + 1) * FFN_TH)
        act = (_silu(_bdot(h_scr[...], win_ref[:, gate_cols]))
               * _bdot(h_scr[...], win_ref[:, up_cols])).astype(bf16)
        down = _bdot(act, wo_ref[gate_cols, :])
        if k == 0:
            acc[...] = down
        else:
            acc[...] += down
    for r0 in range(0, tm, FFN_RC):
        x = x_ref[0, r0:r0 + FFN_RC, :] + _mod_row(mod_ref[0], 5, row0 + r0) * acc[r0:r0 + FFN_RC, :]
        if final:
            ms = jnp.mean(x * x, axis=-1, keepdims=True)
            x = x * lax.rsqrt(ms + EPS) * gf_ref[...]
        o_ref[0, r0:r0 + FFN_RC, :] = x


def _ffn_call(xs, mods_l, g, w_in, w_out, final_g=None):
    final = final_g is not None
    tm, rows = (FFN_TM_LAST, SEQ) if final else (FFN_TM, S)
    resident = lambda a: pl.BlockSpec(a.shape, lambda b, i: (0, 0), pipeline_mode=pl.Buffered(1))
    vec = pl.BlockSpec((1, D), lambda b, i: (0, 0))
    return pl.pallas_call(
        functools.partial(_ffn_kernel, tm=tm, final=final),
        grid=(BATCH, rows // tm),
        in_specs=[pl.BlockSpec((1, tm, D), lambda b, i: (b, i, 0)),
                  pl.BlockSpec((1, MOD_ROWS, D), lambda b, i: (b, 0, 0)),
                  vec, resident(w_in), resident(w_out)] + ([vec] if final else []),
        out_specs=pl.BlockSpec((1, tm, D), lambda b, i: (b, i, 0)),
        out_shape=jax.ShapeDtypeStruct((BATCH, rows, D), f32),
        scratch_shapes=[pltpu.VMEM((tm, D), bf16), pltpu.VMEM((tm, D), f32)],
        input_output_aliases={} if final else {0: 0},
        compiler_params=_cparams(("parallel", "parallel"), 52),
        name="swiglu_ffn_final" if final else "swiglu_ffn",
    )(xs, mods_l, g, w_in, w_out, *([final_g] if final else []))


def _dft_table(lb):
    n = 2 * lb
    f = jnp.arange(lb, dtype=jnp.int32)[:, None]
    t = jnp.arange(lb, dtype=jnp.int32)[None, :]
    ang = (2.0 * math.pi / n) * ((f * t) % n).astype(f32)
    nyq = jnp.where((t & 1) == 0, 1.0, -1.0).astype(f32)
    sin_rows = jnp.where(f == 0, nyq, jnp.sin(ang))
    return jnp.concatenate([jnp.cos(ang), sin_rows], axis=0)


def _hy_features(seg):
    t = jnp.abs(jnp.arange(2 * seg) - seg).astype(f32)
    t_unit = t / (seg - 1)
    ang = 2.0 * math.pi * t / seg
    fb = jnp.linspace(1e-4, HY_BANDS - 1, HY_BANDS, dtype=f32)
    z = jnp.concatenate([t_unit[:, None], jnp.cos(ang[:, None] * fb), -jnp.sin(ang[:, None] * fb)], axis=-1)
    z = jnp.pad(z, ((0, 0), (0, 128 - HY_EMB)))
    deltas = jnp.linspace(math.log(1e-2) / 1.5, math.log(1e-2) / 0.3, D, dtype=f32)
    win = jnp.exp(-t_unit[:, None] * jnp.abs(deltas)[None, :])
    return z, win


def _rope_tables():
    nf = HEAD_P // 4
    inv = ROPE_BASE ** (-jnp.arange(nf, dtype=f32) / nf)
    pos = jnp.arange(SEQ)
    row = (pos // GRID_W).astype(f32)
    col = (pos % GRID_W).astype(f32)
    ar = row[:, None] * inv[None, :]
    ac = col[:, None] * inv[None, :]
    ang = jnp.concatenate([ar, ar, ac, ac], axis=-1)
    sign = jnp.tile(jnp.concatenate([-jnp.ones(nf, f32), jnp.ones(nf, f32)]), 2)
    cos_t = jnp.tile(jnp.cos(ang), (1, ATT_GROUP))
    sin_t = jnp.tile(jnp.sin(ang) * sign[None, :], (1, ATT_GROUP))
    lane = jnp.arange(ATT_GROUP * HEAD_P)
    partner = jnp.where((lane & nf) == 0, lane + nf, lane - nf)
    perm = (lane[:, None] == partner[None, :]).astype(bf16)
    return cos_t, sin_t, perm


def _head_expand(lane_off):
    r = jnp.arange(DT_LANES)[:, None]
    c = jnp.arange(D)[None, :]
    return (c // HEAD_P + lane_off == r).astype(bf16)


def _pad_rows(a, rows):
    return jnp.pad(a, ((0, rows - a.shape[0]), (0, 0)))


def _pad_lanes(a, lanes):
    return jnp.pad(a, ((0, 0), (0, lanes - a.shape[1])))


def kernel(x, c, ctx, c_ctx, ada_w, ada_b, norm1_g, norm2_g, hy_in_w, ssd_conv_w, ssd_conv_b, ssd_dt_bias, ssd_a_log, ssd_d, ssd_norm_g, hy_conv_w, hy_conv_b, hy_w1, hy_b1, hy_w2, hy_b2, hy_w3, hy_b3, hy_w4, hy_freq, hy_bias, hy_out_w, attn_qkv_w, attn_sinks, attn_out_w, ffn_w_in, ffn_w_out, final_g):
    xs = jnp.concatenate([x, ctx], axis=1)

    cc = jnp.concatenate([c, c_ctx[None, :], jnp.zeros((MOD_ROWS - BATCH - 1, D), f32)], axis=0)
    modv = _mod_call(cc, ada_w, ada_b)
    lat = modv[:, :BATCH].reshape(DEPTH, BATCH, N_MOD, D)
    cmod = jnp.broadcast_to(modv[:, BATCH].reshape(DEPTH, 1, N_MOD, D), (DEPTH, BATCH, N_MOD, D))
    pad = jnp.zeros((DEPTH, BATCH, CTX_MOD - N_MOD, D), f32)
    mods = jnp.concatenate([lat, pad, cmod, pad], axis=2)

    tabs = {}
    for seg, nb in ((SEQ, HY_NB), (CTX, 1)):
        w = _dft_table(seg // nb)
        tabs[seg] = (w.astype(bf16), w.T.astype(bf16)) + _hy_features(seg)
    e_f, e_b = _head_expand(0), _head_expand(HEADS)
    cos_t, sin_t, perm = _rope_tables()

    for layer in range(DEPTH):
        j = layer // 2
        mods_l = mods[layer]
        g1 = norm1_g[layer][None, :]
        if layer % 2 == 0:
            w_in = hy_in_w[j]
            w_zx = w_in[:, :COL_DT].astype(bf16)
            wdt = _pad_lanes(w_in[:, COL_DT:COL_HY], DT_LANES).astype(bf16)
            dtb = _pad_lanes(ssd_dt_bias[j].reshape(1, 2 * HEADS), DT_LANES)
            cw = jnp.concatenate([jnp.zeros((4, D), f32),
                                  jnp.concatenate([ssd_conv_w[j], ssd_conv_b[j][None, :]], axis=0)], axis=1)
            zx, dt = _inproj_ssd_call(xs, mods_l, g1, w_zx, _pad_rows(cw, 8), wdt, dtb)
            cw_hy = _pad_rows(jnp.concatenate([hy_conv_w[j], hy_conv_b[j][None, :]], axis=0), 8)
            x0, u = _inproj_hy_call(xs, mods_l, g1, w_in[:, COL_HY:].astype(bf16), cw_hy)

            alog = _pad_lanes(ssd_a_log[j].reshape(1, 2 * HEADS), DT_LANES)
            dskip = jnp.repeat(ssd_d[j], HEAD_P)[None, :]
            yn = _ssd_call(zx, dt, alog, dskip, ssd_norm_g[j][None, :], e_f, e_b)

            segs = []
            for seg, nb in ((SEQ, HY_NB), (CTX, 1)):
                wtab, wtab_t, zt, win = tabs[seg]
                ka, kb = _hyfilter_call(zt, _pad_rows(hy_w1[j], 128), hy_b1[j][None, :], hy_w2[j],
                                        hy_b2[j][None, :], hy_w3[j], hy_b3[j][None, :],
                                        _pad_rows(hy_freq[j], 8), hy_w4[j], win, wtab, seg, nb)
                segs.append((wtab, wtab_t, ka, kb))
            gh = _hyconv_call(u, x0, hy_bias[j][None, :], *segs)
            w_out = hy_out_w[j].astype(bf16)
            xs = _outproj_call(xs, mods_l, [yn, gh], [w_out[:D], w_out[D:]], "outproj_even")
        else:
            wq = attn_qkv_w[j]
            wk = wq[:, QW:QW + KVW].reshape(D, KV_HEADS, HEAD_P)
            wv = wq[:, QW + KVW:].reshape(D, KV_HEADS, HEAD_P)
            w_kv = jnp.concatenate([wk, wv], axis=2).reshape(D, 2 * KVW)
            w_qkv = jnp.concatenate([wq[:, :QW], w_kv], axis=1).astype(bf16)
            qkv = _nmm_call(xs, mods_l, g1, w_qkv, 512, "qkv_proj")
            o = _attn_call(qkv, attn_sinks[j], cos_t, sin_t, perm, need_ctx=layer < DEPTH - 1)
            xs = _outproj_call(xs, mods_l, [o], [attn_out_w[j].astype(bf16)], "outproj_attn")
        xs = _ffn_call(xs, mods_l, norm2_g[layer][None, :], ffn_w_in[layer].astype(bf16),
                       ffn_w_out[layer].astype(bf16),
                       final_g=final_g[None, :] if layer == DEPTH - 1 else None)
    return xs
```

```python
import functools
import math

import jax
import jax.numpy as jnp
from jax import lax
from jax.experimental import pallas as pl
from jax.experimental.pallas import tpu as pltpu

f32 = jnp.float32
bf16 = jnp.bfloat16

D = 1024
BATCH = 8
SEQ = 2048
DEPTH = 4
GRID_W = 64
CTX = 256
S = SEQ + CTX
EPS = 1e-6
N_MOD = 6
MOD_ROWS = 16
CTX_MOD = 8

HEADS = 16
HEAD_P = 64
STATE = 128
GROUPS = 2
CHUNK = 128
XBC = D + 2 * GROUPS * STATE
ZX = D + XBC
DT_LANES = 128
HY_BANDS = 16
HY_EMB = 1 + 2 * HY_BANDS
HY_FFN = 64
COL_XBC = D
COL_DT = COL_XBC + XBC
COL_HY = COL_DT + 2 * HEADS

Q_HEADS = 16
KV_HEADS = 4
ATT_GROUP = Q_HEADS // KV_HEADS
QW = Q_HEADS * HEAD_P
KVW = KV_HEADS * HEAD_P
WINDOW = 128
ABLK = 128
ROPE_BASE = 10000.0
FFN_H = -(-8 * D // (3 * 256)) * 256
LOG2E = math.log2(math.e)

RC = 256
NMM_RC = 768
VMEM_CAP = 56 * 1024 * 1024


def _cparams(sem, vmem_mb):
    return pltpu.CompilerParams(dimension_semantics=sem,
                                vmem_limit_bytes=min(int(vmem_mb * 1024 * 1024), VMEM_CAP))


def _silu(v):
    return v * jax.nn.sigmoid(v)


def _softplus(v):
    return jnp.maximum(v, 0.0) + jnp.log1p(jnp.exp(-jnp.abs(v)))


def _bdot(a, b):
    return jnp.dot(a, b, preferred_element_type=f32)


def _bdot_nt(a, b):
    return lax.dot_general(a, b, (((1,), (1,)), ((), ())), preferred_element_type=f32)


def _row_chunks(n_rows, body, unroll=1):
    def step(c, carry):
        body(pl.multiple_of(c * RC, RC))
        return carry
    lax.fori_loop(0, n_rows // RC, step, 0, unroll=unroll)


def _norm_mod(x, g, mod, shift_row, scale_row, row0):
    ms = jnp.mean(x * x, axis=-1, keepdims=True)
    scale = _mod_row(mod, scale_row, row0)
    return x * lax.rsqrt(ms + EPS) * (g * (1.0 + scale)) + _mod_row(mod, shift_row, row0)


def _mod_row(mod, k, row0):
    return jnp.where(row0 >= SEQ, mod[CTX_MOD + k:CTX_MOD + k + 1], mod[k:k + 1])


def _norm_mod_to_scratch(x_ref, g_ref, mod_ref, h_scr):
    def body(r0):
        h_scr[pl.ds(r0, RC), :] = _norm_mod(x_ref[0, pl.ds(r0, RC), :], g_ref[...], mod_ref[0],
                                            0, 1, r0).astype(bf16)
    _row_chunks(S, body)


CONV_ROWS = 64


def _conv3_rows(p_scr, r0, cw):
    n, cols = CONV_ROWS, p_scr.shape[1]
    zero = jnp.zeros((8, cols), f32)
    tiles = [p_scr[r0 + k:r0 + k + 8, :] for k in range(0, n, 8)]
    above = zero if r0 in (0, SEQ) else p_scr[r0 - 8:r0, :]
    below = zero if r0 + n in (SEQ, S) else p_scr[r0 + n:r0 + n + 8, :]
    sub = lax.broadcasted_iota(jnp.int32, (8, cols), 0)
    down = [pltpu.roll(t, 1, 0) for t in [above] + tiles]
    up = [pltpu.roll(t, 7, 0) for t in tiles + [below]]
    out = []
    for k, t in enumerate(tiles):
        prev = jnp.where(sub == 0, down[k], down[k + 1])
        nxt = jnp.where(sub == 7, up[k + 1], up[k])
        out.append(prev * cw[0:1] + t * cw[1:2] + nxt * cw[2:3] + cw[3:4])
    return jnp.concatenate(out, axis=0)


def _project_then_conv(mm, piece):
    done = 0
    for r0 in range(0, S, NMM_RC):
        while done + CONV_ROWS < r0:
            piece(done)
            done += CONV_ROWS
        mm(r0)
    while done < S:
        piece(done)
        done += CONV_ROWS


def _mod_kernel(c_ref, w_ref, b_ref, o_ref):
    sc = _silu(c_ref[...]).astype(bf16)
    o_ref[0] = _bdot(sc, w_ref[0].astype(bf16)) + b_ref[0]


def _mod_call(cc, ada_w, ada_b):
    tn = 1024
    return pl.pallas_call(
        _mod_kernel,
        grid=(DEPTH, N_MOD * D // tn),
        in_specs=[pl.BlockSpec((MOD_ROWS, D), lambda l, j: (0, 0)),
                  pl.BlockSpec((1, D, tn), lambda l, j: (l, 0, j)),
                  pl.BlockSpec((1, 1, tn), lambda l, j: (l, 0, j))],
        out_specs=pl.BlockSpec((1, MOD_ROWS, tn), lambda l, j: (l, 0, j)),
        out_shape=jax.ShapeDtypeStruct((DEPTH, MOD_ROWS, N_MOD * D), f32),
        compiler_params=_cparams(("parallel", "parallel"), 24),
        name="adaln_mod",
    )(cc, ada_w, ada_b.reshape(DEPTH, 1, N_MOD * D))


def _nmm_kernel(x_ref, mod_ref, g_ref, w_ref, o_ref, h_scr):
    @pl.when(pl.program_id(1) == 0)
    def _():
        _norm_mod_to_scratch(x_ref, g_ref, mod_ref, h_scr)

    def body(c, carry):
        r0 = pl.multiple_of(c * NMM_RC, NMM_RC)
        o_ref[0, pl.ds(r0, NMM_RC), :] = _bdot(h_scr[pl.ds(r0, NMM_RC), :], w_ref[...]).astype(o_ref.dtype)
        return carry
    lax.fori_loop(0, S // NMM_RC, body, 0)


def _nmm_call(xs, mods_l, g, w, tn, name):
    n = w.shape[1]
    return pl.pallas_call(
        _nmm_kernel,
        grid=(BATCH, n // tn),
        in_specs=[pl.BlockSpec((1, S, D), lambda b, j: (b, 0, 0)),
                  pl.BlockSpec((1, MOD_ROWS, D), lambda b, j: (b, 0, 0)),
                  pl.BlockSpec((1, D), lambda b, j: (0, 0)),
                  pl.BlockSpec((D, tn), lambda b, j: (0, j))],
        out_specs=pl.BlockSpec((1, S, tn), lambda b, j: (b, 0, j)),
        out_shape=jax.ShapeDtypeStruct((BATCH, S, n), bf16),
        scratch_shapes=[pltpu.VMEM((S, D), bf16)],
        compiler_params=_cparams(("parallel", "arbitrary"), 40),
        name=name,
    )(xs, mods_l, g, w)


ZX_TN = 512
NZ_T = D // ZX_TN
ZX_TILES = ZX // ZX_TN
HY_TN = 256
HY_TILES = D // HY_TN
assert COL_DT == ZX and COL_DT % DT_LANES == 0


def _inproj_even_kernel(x_ref, mod_ref, g_ref, w_ref, cw_ref, wdt_ref, dtb_ref, w0_ref, w1_ref, w2_ref,
                        c0_ref, c1_ref, c2_ref, zx_ref, dt_ref, x0_ref, u_ref,
                        h_scr, wb_scr, p0_scr, p1_scr, p2_scr):
    j = pl.program_id(1)

    @pl.when(j == 0)
    def _():
        _norm_mod_to_scratch(x_ref, g_ref, mod_ref, h_scr)
        wdt = wdt_ref[0].astype(bf16)

        def dt_body(c, carry):
            r0 = pl.multiple_of(c * NMM_RC, NMM_RC)
            sp = _softplus(_bdot(h_scr[pl.ds(r0, NMM_RC), :], wdt) + dtb_ref[...])
            lane = lax.broadcasted_iota(jnp.int32, sp.shape, 1)
            dt_ref[0, pl.ds(r0, NMM_RC), :] = jnp.where(lane < 2 * HEADS, sp, 0.0)
            return carry
        lax.fori_loop(0, S // NMM_RC, dt_body, 0)

    @pl.when(j < ZX_TILES)
    def _():
        wb_scr[...] = w_ref[0].astype(bf16)

    @pl.when(j < NZ_T)
    def _():
        def body(c, carry):
            r0 = pl.multiple_of(c * NMM_RC, NMM_RC)
            zx_ref[0, pl.ds(r0, NMM_RC), :] = _bdot(h_scr[pl.ds(r0, NMM_RC), :], wb_scr[...]).astype(bf16)
            return carry
        lax.fori_loop(0, S // NMM_RC, body, 0)

    @pl.when(jnp.logical_and(j >= NZ_T, j < ZX_TILES))
    def _():
        def mm(r0):
            p0_scr[r0:r0 + NMM_RC, :] = _bdot(h_scr[r0:r0 + NMM_RC, :], wb_scr[...])

        def piece(a):
            zx_ref[0, a:a + CONV_ROWS, :] = _silu(_conv3_rows(p0_scr, a, cw_ref[...])).astype(bf16)
        _project_then_conv(mm, piece)

    @pl.when(j >= ZX_TILES)
    def _():
        px0 = p0_scr.at[:, 0:HY_TN]

        def mm_x0(r0):
            px0[r0:r0 + NMM_RC, :] = _bdot(h_scr[r0:r0 + NMM_RC, :], w0_ref[...])

        def piece_x0(a):
            x0_ref[0, a:a + CONV_ROWS, :] = _conv3_rows(px0, a, c0_ref[...]).astype(bf16)
        _project_then_conv(mm_x0, piece_x0)

        def mm_u(r0):
            p1_scr[r0:r0 + NMM_RC, :] = _bdot(h_scr[r0:r0 + NMM_RC, :], w1_ref[...])
            p2_scr[r0:r0 + NMM_RC, :] = _bdot(h_scr[r0:r0 + NMM_RC, :], w2_ref[...])

        def piece_u(a):
            x1 = _conv3_rows(p1_scr, a, c1_ref[...])
            v = _conv3_rows(p2_scr, a, c2_ref[...])
            u_ref[0, a:a + CONV_ROWS, :] = (x1 * v).astype(bf16)
        _project_then_conv(mm_u, piece_u)


def _inproj_even_call(xs, mods_l, g, w_all, jl, cw, dtb, w_hy, cw_hy):
    zt = lambda j: jnp.minimum(j, ZX_TILES - 1)
    ht = lambda j: jnp.maximum(j - ZX_TILES, 0)
    wspec = lambda k: pl.BlockSpec((D, HY_TN), lambda b, j, k=k: (0, ht(j) + k * HY_TILES))
    cspec = lambda k: pl.BlockSpec((8, HY_TN), lambda b, j, k=k: (0, ht(j) + k * HY_TILES))
    return pl.pallas_call(
        _inproj_even_kernel,
        grid=(BATCH, ZX_TILES + HY_TILES),
        in_specs=[pl.BlockSpec((1, S, D), lambda b, j: (b, 0, 0), pipeline_mode=pl.Buffered(1)),
                  pl.BlockSpec((1, MOD_ROWS, D), lambda b, j: (b, 0, 0)),
                  pl.BlockSpec((1, D), lambda b, j: (0, 0)),
                  pl.BlockSpec((1, D, ZX_TN), lambda b, j: (jl, 0, zt(j))),
                  pl.BlockSpec((8, ZX_TN), lambda b, j: (0, zt(j))),
                  pl.BlockSpec((1, D, DT_LANES), lambda b, j: (jl, 0, COL_DT // DT_LANES)),
                  pl.BlockSpec((1, DT_LANES), lambda b, j: (0, 0)),
                  wspec(0), wspec(1), wspec(2), cspec(0), cspec(1), cspec(2)],
        out_specs=[pl.BlockSpec((1, S, ZX_TN), lambda b, j: (b, 0, zt(j))),
                   pl.BlockSpec((1, S, DT_LANES), lambda b, j: (b, 0, 0)),
                   pl.BlockSpec((1, S, HY_TN), lambda b, j: (b, 0, ht(j))),
                   pl.BlockSpec((1, S, HY_TN), lambda b, j: (b, 0, ht(j)))],
        out_shape=[jax.ShapeDtypeStruct((BATCH, S, ZX), bf16),
                   jax.ShapeDtypeStruct((BATCH, S, DT_LANES), f32),
                   jax.ShapeDtypeStruct((BATCH, S, D), bf16),
                   jax.ShapeDtypeStruct((BATCH, S, D), bf16)],
        scratch_shapes=[pltpu.VMEM((S, D), bf16), pltpu.VMEM((D, ZX_TN), bf16),
                        pltpu.VMEM((S, ZX_TN), f32), pltpu.VMEM((S, HY_TN), f32),
                        pltpu.VMEM((S, HY_TN), f32)],
        compiler_params=_cparams(("parallel", "arbitrary"), 56),
        name="inproj_even",
    )(xs, mods_l, g, w_all, cw, w_all, dtb, w_hy, w_hy, w_hy, cw_hy, cw_hy, cw_hy)


N_CHUNK = S // CHUNK
LAT_CHUNKS = SEQ // CHUNK


def _split3_dot(t, a):
    a1 = a.astype(bf16)
    r1 = a - a1.astype(f32)
    a2 = r1.astype(bf16)
    a3 = (r1 - a2.astype(f32)).astype(bf16)
    return _bdot(t, a1) + _bdot(t, a2) + _bdot(t, a3)


def _ssd_chunk(zx_ref, dt_ref, aneg, e_ref, y_scr, st_scr, r0, backward):
    lane_off = HEADS if backward else 0
    x = zx_ref[0, pl.ds(r0, CHUNK), D:2 * D].astype(f32)
    bm = zx_ref[0, pl.ds(r0, CHUNK), 2 * D:2 * D + GROUPS * STATE]
    cm = zx_ref[0, pl.ds(r0, CHUNK), 2 * D + GROUPS * STATE:ZX]
    dt = dt_ref[0, pl.ds(r0, CHUNK), :]
    a = dt * aneg

    ri = lax.broadcasted_iota(jnp.int32, (CHUNK, CHUNK), 0)
    ci = lax.broadcasted_iota(jnp.int32, (CHUNK, CHUNK), 1)
    causal = (ci >= ri) if backward else (ci <= ri)
    tmat = jnp.where(causal, 1.0, 0.0).astype(bf16)
    acum = _split3_dot(tmat, a)
    acum_t = acum.T
    tot = acum[0:1] if backward else acum[CHUNK - 1:CHUNK]
    dte = jnp.exp2(tot - acum)
    eac = jnp.exp2(acum)
    cdec = jnp.broadcast_to(jnp.exp2(tot), (16, DT_LANES))

    e = e_ref[...]
    dt_x = _bdot(dt.astype(bf16), e)
    dte_x = _bdot(dte.astype(bf16), e)
    eac_x = _bdot(eac.astype(bf16), e)
    cdec_x = _bdot(cdec.astype(bf16), e)[0:1]

    xdt = x * dt_x
    xs_b = (xdt * dte_x).astype(bf16)
    lane = lax.broadcasted_iota(jnp.int32, (CHUNK, 2 * HEAD_P), 1)
    low = lane < HEAD_P
    hg = HEADS // GROUPS
    gw = hg * HEAD_P
    for g in range(GROUPS):
        bg = bm[:, g * STATE:(g + 1) * STATE]
        cg = cm[:, g * STATE:(g + 1) * STATE]
        cb = _bdot_nt(cg, bg)
        hin = st_scr[:, g * gw:(g + 1) * gw]
        yoff = _bdot(cg, hin.astype(bf16)) * eac_x[:, g * gw:(g + 1) * gw]
        for hp in range(hg // 2):
            h0 = g * hg + 2 * hp
            ms = []
            for h in (h0, h0 + 1):
                l0 = lane_off + h
                seg = acum[:, l0:l0 + 1] - acum_t[l0:l0 + 1, :]
                ms.append(cb * jnp.exp2(jnp.where(causal, seg, -jnp.inf)))
            lhs = jnp.concatenate(ms, axis=1).astype(bf16)
            xp = xdt[:, h0 * HEAD_P:(h0 + 2) * HEAD_P]
            rhs = jnp.concatenate([jnp.where(low, xp, 0.0), jnp.where(low, 0.0, xp)],
                                  axis=0).astype(bf16)
            yd = _bdot(lhs, rhs) + yoff[:, hp * 2 * HEAD_P:(hp + 1) * 2 * HEAD_P]
            cols = slice(h0 * HEAD_P, (h0 + 2) * HEAD_P)
            y_scr[pl.ds(r0, CHUNK), cols] = y_scr[pl.ds(r0, CHUNK), cols] + yd
        bg_t = bg.astype(f32).T.astype(bf16)
        st_scr[:, g * gw:(g + 1) * gw] = (hin * cdec_x[:, g * gw:(g + 1) * gw]
                                         + _bdot(bg_t, xs_b[:, g * gw:(g + 1) * gw]))


def _ssd_kernel(zx_ref, dt_ref, alog_ref, dskip_ref, ng_ref, ef_ref, eb_ref, o_ref,
                y_scr, sf_scr, sb_scr):
    aneg = -jnp.exp(alog_ref[...]) * LOG2E
    sf_scr[...] = jnp.zeros_like(sf_scr)
    sb_scr[...] = jnp.zeros_like(sb_scr)

    def init(c, carry):
        r0 = pl.multiple_of(c * CHUNK, CHUNK)
        y_scr[pl.ds(r0, CHUNK), :] = dskip_ref[...] * zx_ref[0, pl.ds(r0, CHUNK), D:2 * D].astype(f32)
        return carry
    lax.fori_loop(0, N_CHUNK, init, 0)

    def step(t, carry):
        cf = jnp.where(t < N_CHUNK - LAT_CHUNKS, LAT_CHUNKS + t, t - (N_CHUNK - LAT_CHUNKS))
        cbk = N_CHUNK - 1 - t
        _ssd_chunk(zx_ref, dt_ref, aneg, ef_ref, y_scr, sf_scr, pl.multiple_of(cf * CHUNK, CHUNK), False)
        _ssd_chunk(zx_ref, dt_ref, aneg, eb_ref, y_scr, sb_scr, pl.multiple_of(cbk * CHUNK, CHUNK), True)
        return carry
    lax.fori_loop(0, N_CHUNK, step, 0, unroll=2)

    def fin(c, carry):
        r0 = pl.multiple_of(c * CHUNK, CHUNK)
        t = y_scr[pl.ds(r0, CHUNK), :] * _silu(zx_ref[0, pl.ds(r0, CHUNK), 0:D].astype(f32))
        ms = jnp.mean(t * t, axis=-1, keepdims=True)
        o_ref[0, pl.ds(r0, CHUNK), :] = (t * lax.rsqrt(ms + EPS) * ng_ref[...]).astype(bf16)
        return carry
    lax.fori_loop(0, N_CHUNK, fin, 0)


def _ssd_call(zx, dt, alog, dskip, ng, e_f, e_b):
    return pl.pallas_call(
        _ssd_kernel,
        grid=(BATCH,),
        in_specs=[pl.BlockSpec((1, S, ZX), lambda b: (b, 0, 0)),
                  pl.BlockSpec((1, S, DT_LANES), lambda b: (b, 0, 0)),
                  pl.BlockSpec((1, DT_LANES), lambda b: (0, 0)),
                  pl.BlockSpec((1, D), lambda b: (0, 0)),
                  pl.BlockSpec((1, D), lambda b: (0, 0)),
                  pl.BlockSpec((DT_LANES, D), lambda b: (0, 0)),
                  pl.BlockSpec((DT_LANES, D), lambda b: (0, 0))],
        out_specs=pl.BlockSpec((1, S, D), lambda b: (b, 0, 0)),
        out_shape=jax.ShapeDtypeStruct((BATCH, S, D), bf16),
        scratch_shapes=[pltpu.VMEM((S, D), f32),
                        pltpu.VMEM((STATE, D), f32),
                        pltpu.VMEM((STATE, D), f32)],
        compiler_params=_cparams(("parallel",), 56),
        name="ssd_scan",
    )(zx, dt, alog, dskip, ng, e_f, e_b)


def _hp_dot(a, b):
    return jnp.dot(a, b, precision=lax.Precision.HIGHEST, preferred_element_type=f32)


def _hyfilter_kernel(z_ref, w1_ref, b1_ref, w2_ref, b2_ref, w3_ref, b3_ref, fr_ref, w4f_ref, w4b_ref,
                     win_ref, w_ref, ka_ref, kb_ref, h_scr, tap_scr, tb_scr, pp_scr, pq_scr, *, seg, nb):
    lb = seg // nb
    tc = tap_scr.shape[1]

    @pl.when(pl.program_id(0) == 0)
    def _():
        fr = fr_ref[...]

        def hidden(c, carry):
            r0 = pl.multiple_of(c * RC, RC)
            h = jnp.sin(fr[0:1] * (_hp_dot(z_ref[pl.ds(r0, RC), :], w1_ref[...]) + b1_ref[...]))
            h = jnp.sin(fr[1:2] * (_hp_dot(h, w2_ref[...]) + b2_ref[...]))
            h_scr[pl.ds(r0, RC), :] = jnp.sin(fr[2:3] * (_hp_dot(h, w3_ref[...]) + b3_ref[...]))
            return carry
        lax.fori_loop(0, 2 * seg // RC, hidden, 0)

    def taps(w4_ref):
        def body(c, ssq):
            r0 = pl.multiple_of(c * RC, RC)
            t = _hp_dot(h_scr[pl.ds(r0, RC), :], w4_ref[...]) * win_ref[pl.ds(r0, RC), :]
            row = r0 + lax.broadcasted_iota(jnp.int32, t.shape, 0)
            t = jnp.where(row == 0, 0.0, t)
            tap_scr[pl.ds(r0, RC), :] = t
            return ssq + jnp.sum(t * t, axis=0, keepdims=True)
        return body
    ssq = lax.fori_loop(0, seg // RC, taps(w4b_ref), jnp.zeros((1, tc), f32))
    nrm = lax.rsqrt(lax.fori_loop(seg // RC, 2 * seg // RC, taps(w4f_ref), ssq) + 1e-6)

    def scale(c, carry):
        r0 = pl.multiple_of(c * RC, RC)
        tb_scr[pl.ds(r0, RC), :] = (tap_scr[pl.ds(r0, RC), :] * nrm).astype(bf16)
        return carry
    lax.fori_loop(0, 2 * seg // RC, scale, 0)

    row = lax.broadcasted_iota(jnp.int32, (lb, tc), 0)
    sgn = jnp.where((row & 1) == 0, 1.0, -1.0)
    for s in range(2 * nb):
        blk = tb_scr[s * lb:(s + 1) * lb, :]
        p = _bdot(w_ref[0:lb, :], blk)
        q = _bdot(w_ref[lb:2 * lb, :], blk)
        if s > 0:
            t0 = tb_scr[(s - 1) * lb:(s - 1) * lb + 16, :][0:1].astype(f32)
            ka_ref[s - 1] = (p + sgn * (pp_scr[...] - t0)).astype(ka_ref.dtype)
            kb_ref[s - 1] = jnp.where(row == 0, q + pq_scr[...] - t0,
                                      q + sgn * pq_scr[...]).astype(kb_ref.dtype)
        pp_scr[...] = p
        pq_scr[...] = q


HY_TC = 256
HY_NB = 4
HY_SPEC_DT = bf16


def _hyfilter_call(zt, w1, b1, w2, b2, w3, b3, fr, w4, win, wtab, seg, nb):
    tc, lb = HY_TC, seg // nb
    nct = D // tc
    const = lambda shape: pl.BlockSpec(shape, lambda c: (0, 0))
    return pl.pallas_call(
        functools.partial(_hyfilter_kernel, seg=seg, nb=nb),
        grid=(nct,),
        in_specs=[const((2 * seg, 128)), const((128, HY_FFN)), const((1, HY_FFN)),
                  const((HY_FFN, HY_FFN)), const((1, HY_FFN)),
                  const((HY_FFN, HY_FFN)), const((1, HY_FFN)), const((8, HY_FFN)),
                  pl.BlockSpec((HY_FFN, tc), lambda c: (0, c)),
                  pl.BlockSpec((HY_FFN, tc), lambda c: (0, c + nct)),
                  pl.BlockSpec((2 * seg, tc), lambda c: (0, c)),
                  const((2 * lb, lb))],
        out_specs=[pl.BlockSpec((2 * nb - 1, lb, tc), lambda c: (0, 0, c)),
                   pl.BlockSpec((2 * nb - 1, lb, tc), lambda c: (0, 0, c))],
        out_shape=[jax.ShapeDtypeStruct((2 * nb - 1, lb, D), HY_SPEC_DT)] * 2,
        scratch_shapes=[pltpu.VMEM((2 * seg, HY_FFN), f32),
                        pltpu.VMEM((2 * seg, tc), f32), pltpu.VMEM((2 * seg, tc), bf16),
                        pltpu.VMEM((lb, tc), f32), pltpu.VMEM((lb, tc), f32)],
        compiler_params=_cparams(("arbitrary",), 40),
        name=f"hyena_filter_{seg}",
    )(zt, w1, b1, w2, b2, w3, b3, fr, w4, w4, win, wtab)


HY_PIECE = 64


def _hyconv_segment(u_ref, x0_ref, bias_ref, o_ref, w_ref, wt_ref, ka_ref, kb_ref,
                    ua_scr, ub_scr, ya_scr, yb_scr, row0, nb, lb):
    for i in range(nb):
        ui = u_ref[0, row0 + i * lb:row0 + (i + 1) * lb, :]
        ua_scr[i, 0:lb, :] = _bdot(w_ref[0:lb, :], ui).astype(HY_SPEC_DT)
        ub_scr[i, 0:lb, :] = _bdot(w_ref[lb:2 * lb, :], ui).astype(HY_SPEC_DT)

    is0 = lax.broadcasted_iota(jnp.int32, (HY_PIECE, ka_ref.shape[2]), 0) == 0
    for o in range(nb):
        for r in range(0, lb, HY_PIECE):
            rows = slice(r, r + HY_PIECE)
            ya = yb = None
            for i in range(nb):
                a, b = ua_scr[i, rows, :], ub_scr[i, rows, :]
                ka, kb = ka_ref[o - i + nb - 1, rows, :], kb_ref[o - i + nb - 1, rows, :]
                if r == 0:
                    pa = jnp.where(is0, a * ka, a * ka - b * kb)
                    pb = jnp.where(is0, b * kb, a * kb + b * ka)
                else:
                    pa, pb = a * ka - b * kb, a * kb + b * ka
                ya = pa if ya is None else ya + pa
                yb = pb if yb is None else yb + pb
            scale = jnp.where(is0, 0.5 / lb, 1.0 / lb) if r == 0 else 1.0 / lb
            ya_scr[o, rows, :] = (ya * scale).astype(bf16)
            yb_scr[o, rows, :] = (yb * scale).astype(bf16)
        y = _bdot(wt_ref[:, 0:lb], ya_scr[o, 0:lb, :]) + _bdot(wt_ref[:, lb:2 * lb], yb_scr[o, 0:lb, :])
        rows = slice(row0 + o * lb, row0 + (o + 1) * lb)
        y = y + bias_ref[...] * u_ref[0, rows, :].astype(f32)
        o_ref[0, rows, :] = (x0_ref[0, rows, :].astype(f32) * y).astype(bf16)


def _hyconv_kernel(u_ref, x0_ref, bias_ref, w_ref, wt_ref, ka_ref, kb_ref, cw_ref, cwt_ref, cka_ref,
                   ckb_ref, o_ref, ua_scr, ub_scr, ya_scr, yb_scr):
    scr = (ua_scr, ub_scr, ya_scr, yb_scr)
    _hyconv_segment(u_ref, x0_ref, bias_ref, o_ref, w_ref, wt_ref, ka_ref, kb_ref, *scr,
                    row0=0, nb=HY_NB, lb=SEQ // HY_NB)
    _hyconv_segment(u_ref, x0_ref, bias_ref, o_ref, cw_ref, cwt_ref, cka_ref, ckb_ref, *scr,
                    row0=SEQ, nb=1, lb=CTX)


def _hyconv_call(u, x0, bias, lat, ctx):
    tc, lb = HY_TC, SEQ // HY_NB
    full = lambda a: pl.BlockSpec(a.shape, lambda c, b: (0,) * a.ndim)
    spec = lambda a: pl.BlockSpec(a.shape[:2] + (tc,), lambda c, b: (0, 0, c))
    seq = pl.BlockSpec((1, S, tc), lambda c, b: (b, 0, c))
    return pl.pallas_call(
        _hyconv_kernel,
        grid=(D // tc, BATCH),
        in_specs=[seq, seq, pl.BlockSpec((1, tc), lambda c, b: (0, c)),
                  full(lat[0]), full(lat[1]), spec(lat[2]), spec(lat[3]),
                  full(ctx[0]), full(ctx[1]), spec(ctx[2]), spec(ctx[3])],
        out_specs=seq,
        out_shape=jax.ShapeDtypeStruct((BATCH, S, D), bf16),
        scratch_shapes=[pltpu.VMEM((HY_NB, lb, tc), HY_SPEC_DT), pltpu.VMEM((HY_NB, lb, tc), HY_SPEC_DT),
                        pltpu.VMEM((HY_NB, lb, tc), bf16), pltpu.VMEM((HY_NB, lb, tc), bf16)],
        compiler_params=_cparams(("parallel", "parallel"), 40),
        name="hyena_conv",
    )(u, x0, bias, *lat, *ctx)


def _rope(xb, perm, cos_t, sin_t):
    n = xb.shape[1]
    return xb.astype(f32) * cos_t + _bdot(xb, perm[0:n, 0:n]) * sin_t


assert WINDOW == ABLK
A_SPAN = ABLK + 2 * WINDOW
A_KEYS = A_SPAN + CTX
A_COLS = ATT_GROUP * ABLK


def _attn_kernel(sink_ref, q_ref, kv_ref, cos_ref, sin_ref, perm_ref, o_ref, q_scr, k_scr, vt_scr,
                 s_scr, p_scr):
    g = pl.program_id(1)
    qscale = HEAD_P ** -0.5 * LOG2E

    def prep(r0, with_rope):
        qb = q_ref[0, pl.ds(r0, RC), :]
        kvb = kv_ref[0, pl.ds(r0, RC), :]
        kv = kvb.astype(f32)
        if with_rope:
            cos_t, sin_t = cos_ref[pl.ds(r0, RC), :], sin_ref[pl.ds(r0, RC), :]
            q = _rope(qb, perm_ref[...], cos_t, sin_t) * qscale
            k = _rope(kvb, perm_ref[...], cos_t[:, 0:2 * HEAD_P], sin_t[:, 0:2 * HEAD_P])
        else:
            q, k = qb.astype(f32) * qscale, kv
        low = lax.broadcasted_iota(jnp.int32, kv.shape, 1) < HEAD_P
        k_scr[pl.ds(r0, RC), :] = jnp.where(low, k, 0.0).astype(bf16)
        for r in range(ATT_GROUP):
            slab = q[:, (r // 2) * 2 * HEAD_P:(r // 2 + 1) * 2 * HEAD_P]
            if r % 2:
                slab = pltpu.roll(slab, HEAD_P, 1)
            q_scr[r, pl.ds(r0, RC), :] = jnp.where(low, slab, 0.0).astype(bf16)
        v_t = kv.T[HEAD_P:2 * HEAD_P, :].astype(bf16)
        c0 = r0 // ABLK
        for h in range(RC // ABLK):
            vt_scr[c0 + h] = v_t[:, h * ABLK:(h + 1) * ABLK]
    _row_chunks(SEQ, lambda r0: prep(r0, True), unroll=2)
    for r0 in range(SEQ, S, RC):
        prep(r0, False)

    kmq = (lax.broadcasted_iota(jnp.int32, (ABLK, A_COLS), 0)
           - (lax.broadcasted_iota(jnp.int32, (ABLK, A_COLS), 1) & (ABLK - 1)))

    def block(q0, start, masks, slot):
        qstack = jnp.concatenate([q_scr[r, pl.ds(q0, ABLK), :] for r in range(ATT_GROUP)], axis=0)
        chunks = []
        if masks:
            s_loc = _bdot_nt(k_scr[pl.ds(start, len(masks) * ABLK), :], qstack)
            for c, kind in enumerate(masks):
                s_c = s_loc[c * ABLK:(c + 1) * ABLK]
                if kind is not None:
                    s_c = jnp.where(kmq >= 0 if kind == 'ge' else kmq <= 0, s_c, -jnp.inf)
                s_scr[slot, c * ABLK:(c + 1) * ABLK, :] = s_c
                chunks.append((c * ABLK, start // ABLK + c))
        n_loc = len(chunks) * ABLK
        s_scr[slot, n_loc:n_loc + CTX, :] = _bdot_nt(k_scr[SEQ:S, :], qstack)
        chunks += [(n_loc + c * ABLK, SEQ // ABLK + c) for c in range(CTX // ABLK)]

        inv = []
        for r in range(ATT_GROUP):
            cols = slice(r * ABLK, (r + 1) * ABLK)
            sink = sink_ref[g * ATT_GROUP + r] * LOG2E
            m = jnp.full((1, ABLK), sink, f32)
            for row, _ in chunks:
                m = jnp.maximum(m, jnp.max(s_scr[slot, row:row + ABLK, cols], axis=0, keepdims=True))
            den = jnp.exp2(sink - m)
            for row, _ in chunks:
                p = jnp.exp2(s_scr[slot, row:row + ABLK, cols] - m)
                den = den + jnp.sum(p, axis=0, keepdims=True)
                p_scr[slot, row:row + ABLK, cols] = p.astype(bf16)
            inv.append(1.0 / den)

        o_t = None
        for row, vc in chunks:
            part = _bdot(vt_scr[vc], p_scr[slot, row:row + ABLK, :])
            o_t = part if o_t is None else o_t + part
        o_t = jnp.concatenate([o_t[:, r * ABLK:(r + 1) * ABLK] * inv[r] for r in range(ATT_GROUP)], axis=0)
        o_ref[0, pl.ds(q0, ABLK), :] = o_t.T.astype(bf16)

    n_blk = SEQ // ABLK
    block(0, 0, (None, 'le'), 0)
    block(SEQ - ABLK, SEQ - 2 * ABLK, ('ge', None), 1)

    def lat_pair(j, carry):
        for slot in range(2):
            q0 = pl.multiple_of((1 + 2 * j + slot) * ABLK, ABLK)
            block(q0, pl.multiple_of(q0 - WINDOW, ABLK), ('ge', None, 'le'), slot)
        return carry
    lax.fori_loop(0, (n_blk - 2) // 2, lat_pair, 0, unroll=True)

    for slot, q0 in enumerate(range(SEQ, o_ref.shape[1], ABLK)):
        block(q0, 0, (), slot % 2)


def _attn_call(qkv, sinks, cos_t, sin_t, perm, need_ctx):
    gw = ATT_GROUP * HEAD_P
    rows = S if need_ctx else SEQ
    return pl.pallas_call(
        _attn_kernel,
        grid_spec=pltpu.PrefetchScalarGridSpec(
            num_scalar_prefetch=1,
            grid=(BATCH, KV_HEADS),
            in_specs=[pl.BlockSpec((1, S, gw), lambda b, g, s: (b, 0, g)),
                      pl.BlockSpec((1, S, 2 * HEAD_P), lambda b, g, s: (b, 0, QW // (2 * HEAD_P) + g)),
                      pl.BlockSpec((SEQ, gw), lambda b, g, s: (0, 0)),
                      pl.BlockSpec((SEQ, gw), lambda b, g, s: (0, 0)),
                      pl.BlockSpec((gw, gw), lambda b, g, s: (0, 0))],
            out_specs=pl.BlockSpec((1, rows, gw), lambda b, g, s: (b, 0, g)),
            scratch_shapes=[pltpu.VMEM((ATT_GROUP, S, 2 * HEAD_P), bf16),
                            pltpu.VMEM((S, 2 * HEAD_P), bf16),
                            pltpu.VMEM((S // ABLK, HEAD_P, ABLK), bf16),
                            pltpu.VMEM((2, A_KEYS, A_COLS), f32),
                            pltpu.VMEM((2, A_KEYS, A_COLS), bf16)]),
        out_shape=jax.ShapeDtypeStruct((BATCH, rows, QW), bf16),
        compiler_params=_cparams(("parallel", "parallel"), 40),
        name="window_attn",
    )(sinks, qkv, qkv, cos_t, sin_t, perm)


RES_TM = 768
RES_TM_LAT = 1024


def _outproj_kernel(x_ref, mod_ref, *rest, n_in, tm):
    a_refs, w_refs, o_ref = rest[:n_in], rest[n_in:2 * n_in], rest[2 * n_in]
    acc = _bdot(a_refs[0][0], w_refs[0][...])
    for a_ref, w_ref in zip(a_refs[1:], w_refs[1:]):
        acc = acc + _bdot(a_ref[0], w_ref[...])
    for r0 in range(0, tm, RC):
        gate = _mod_row(mod_ref[0], 2, pl.program_id(1) * tm + r0)
        o_ref[0, r0:r0 + RC, :] = x_ref[0, r0:r0 + RC, :] + gate * acc[r0:r0 + RC]


def _outproj_call(xs, mods_l, acts, ws, name):
    n_in = len(acts)
    rows = acts[0].shape[1]
    tm = RES_TM if rows == S else RES_TM_LAT
    in_specs = [pl.BlockSpec((1, tm, D), lambda b, i: (b, i, 0)),
                pl.BlockSpec((1, MOD_ROWS, D), lambda b, i: (b, 0, 0))]
    in_specs += [pl.BlockSpec((1, tm, a.shape[2]), lambda b, i: (b, i, 0)) for a in acts]
    in_specs += [pl.BlockSpec(w.shape, lambda b, i: (0, 0)) for w in ws]
    return pl.pallas_call(
        functools.partial(_outproj_kernel, n_in=n_in, tm=tm),
        grid=(BATCH, rows // tm),
        in_specs=in_specs,
        out_specs=pl.BlockSpec((1, tm, D), lambda b, i: (b, i, 0)),
        out_shape=jax.ShapeDtypeStruct((BATCH, S, D), f32),
        input_output_aliases={0: 0},
        compiler_params=_cparams(("parallel", "parallel"), 40),
        name=name,
    )(xs, mods_l, *acts, *ws)


FFN_TM = 1152
FFN_TM_LAST = 1024
FFN_RC = 128
FFN_TH = 256
FFN_NH = FFN_H // FFN_TH


def _ffn_kernel(x_ref, mod_ref, g_ref, win_ref, wo_ref, *rest, tm, final):
    gf_ref = rest[0] if final else None
    o_ref, h_scr, acc = rest[-3:]
    row0 = pl.program_id(1) * tm
    for r0 in range(0, tm, FFN_RC):
        h_scr[r0:r0 + FFN_RC, :] = _norm_mod(x_ref[0, r0:r0 + FFN_RC, :], g_ref[...], mod_ref[0],
                                             3, 4, row0 + r0).astype(bf16)
    for k in range(FFN_NH):
        gate_cols = slice(k * FFN_TH, (k + 1) * FFN_TH)
        up_cols = slice(FFN_H + k * FFN_TH, FFN_H + (k + 1) * FFN_TH)
        act = (_silu(_bdot(h_scr[...], win_ref[:, gate_cols]))
               * _bdot(h_scr[...], win_ref[:, up_cols])).astype(bf16)
        down = _bdot(act, wo_ref[gate_cols, :])
        if k == 0:
            acc[...] = down
        else:
            acc[...] += down
    for r0 in range(0, tm, FFN_RC):
        x = x_ref[0, r0:r0 + FFN_RC, :] + _mod_row(mod_ref[0], 5, row0 + r0) * acc[r0:r0 + FFN_RC, :]
        if final:
            ms = jnp.mean(x * x, axis=-1, keepdims=True)
            x = x * lax.rsqrt(ms + EPS) * gf_ref[...]
        o_ref[0, r0:r0 + FFN_RC, :] = x


def _ffn_call(xs, mods_l, g, w_in, w_out, final_g=None):
    final = final_g is not None
    tm, rows = (FFN_TM_LAST, SEQ) if final else (FFN_TM, S)
    resident = lambda a: pl.BlockSpec(a.shape, lambda b, i: (0, 0), pipeline_mode=pl.Buffered(1))
    vec = pl.BlockSpec((1, D), lambda b, i: (0, 0))
    return pl.pallas_call(
        functools.partial(_ffn_kernel, tm=tm, final=final),
        grid=(BATCH, rows // tm),
        in_specs=[pl.BlockSpec((1, tm, D), lambda b, i: (b, i, 0)),
                  pl.BlockSpec((1, MOD_ROWS, D), lambda b, i: (b, 0, 0)),
                  vec, resident(w_in), resident(w_out)] + ([vec] if final else []),
        out_specs=pl.BlockSpec((1, tm, D), lambda b, i: (b, i, 0)),
        out_shape=jax.ShapeDtypeStruct((BATCH, rows, D), f32),
        scratch_shapes=[pltpu.VMEM((tm, D), bf16), pltpu.VMEM((tm, D), f32)],
        input_output_aliases={} if final else {0: 0},
        compiler_params=_cparams(("parallel", "parallel"), 52),
        name="swiglu_ffn_final" if final else "swiglu_ffn",
    )(xs, mods_l, g, w_in, w_out, *([final_g] if final else []))


def _dft_table(lb):
    n = 2 * lb
    f = jnp.arange(lb, dtype=jnp.int32)[:, None]
    t = jnp.arange(lb, dtype=jnp.int32)[None, :]
    ang = (2.0 * math.pi / n) * ((f * t) % n).astype(f32)
    nyq = jnp.where((t & 1) == 0, 1.0, -1.0).astype(f32)
    sin_rows = jnp.where(f == 0, nyq, jnp.sin(ang))
    return jnp.concatenate([jnp.cos(ang), sin_rows], axis=0)


def _hy_features(seg):
    t = jnp.abs(jnp.arange(2 * seg) - seg).astype(f32)
    t_unit = t / (seg - 1)
    ang = 2.0 * math.pi * t / seg
    fb = jnp.linspace(1e-4, HY_BANDS - 1, HY_BANDS, dtype=f32)
    z = jnp.concatenate([t_unit[:, None], jnp.cos(ang[:, None] * fb), -jnp.sin(ang[:, None] * fb)], axis=-1)
    z = jnp.pad(z, ((0, 0), (0, 128 - HY_EMB)))
    deltas = jnp.linspace(math.log(1e-2) / 1.5, math.log(1e-2) / 0.3, D, dtype=f32)
    win = jnp.exp(-t_unit[:, None] * jnp.abs(deltas)[None, :])
    return z, win


def _rope_tables():
    nf = HEAD_P // 4
    inv = ROPE_BASE ** (-jnp.arange(nf, dtype=f32) / nf)
    pos = jnp.arange(SEQ)
    row = (pos // GRID_W).astype(f32)
    col = (pos % GRID_W).astype(f32)
    ar = row[:, None] * inv[None, :]
    ac = col[:, None] * inv[None, :]
    ang = jnp.concatenate([ar, ar, ac, ac], axis=-1)
    sign = jnp.tile(jnp.concatenate([-jnp.ones(nf, f32), jnp.ones(nf, f32)]), 2)
    cos_t = jnp.tile(jnp.cos(ang), (1, ATT_GROUP))
    sin_t = jnp.tile(jnp.sin(ang) * sign[None, :], (1, ATT_GROUP))
    lane = jnp.arange(ATT_GROUP * HEAD_P)
    partner = jnp.where((lane & nf) == 0, lane + nf, lane - nf)
    perm = (lane[:, None] == partner[None, :]).astype(bf16)
    return cos_t, sin_t, perm


def _head_expand(lane_off):
    r = jnp.arange(DT_LANES)[:, None]
    c = jnp.arange(D)[None, :]
    return (c // HEAD_P + lane_off == r).astype(bf16)


def _pad_rows(a, rows):
    return jnp.pad(a, ((0, rows - a.shape[0]), (0, 0)))


def _pad_lanes(a, lanes):
    return jnp.pad(a, ((0, 0), (0, lanes - a.shape[1])))


def kernel(x, c, ctx, c_ctx, ada_w, ada_b, norm1_g, norm2_g, hy_in_w, ssd_conv_w, ssd_conv_b, ssd_dt_bias, ssd_a_log, ssd_d, ssd_norm_g, hy_conv_w, hy_conv_b, hy_w1, hy_b1, hy_w2, hy_b2, hy_w3, hy_b3, hy_w4, hy_freq, hy_bias, hy_out_w, attn_qkv_w, attn_sinks, attn_out_w, ffn_w_in, ffn_w_out, final_g):
    xs = jnp.concatenate([x, ctx], axis=1)

    cc = jnp.concatenate([c, c_ctx[None, :], jnp.zeros((MOD_ROWS - BATCH - 1, D), f32)], axis=0)
    modv = _mod_call(cc, ada_w, ada_b)
    lat = modv[:, :BATCH].reshape(DEPTH, BATCH, N_MOD, D)
    cmod = jnp.broadcast_to(modv[:, BATCH].reshape(DEPTH, 1, N_MOD, D), (DEPTH, BATCH, N_MOD, D))
    pad = jnp.zeros((DEPTH, BATCH, CTX_MOD - N_MOD, D), f32)
    mods = jnp.concatenate([lat, pad, cmod, pad], axis=2)

    tabs = {}
    for seg, nb in ((SEQ, HY_NB), (CTX, 1)):
        w = _dft_table(seg // nb)
        tabs[seg] = (w.astype(bf16), w.T.astype(bf16)) + _hy_features(seg)
    e_f, e_b = _head_expand(0), _head_expand(HEADS)
    cos_t, sin_t, perm = _rope_tables()

    for layer in range(DEPTH):
        j = layer // 2
        mods_l = mods[layer]
        g1 = norm1_g[layer][None, :]
        if layer % 2 == 0:
            dtb = _pad_lanes(ssd_dt_bias[j].reshape(1, 2 * HEADS), DT_LANES)
            cw = jnp.concatenate([jnp.zeros((4, D), f32),
                                  jnp.concatenate([ssd_conv_w[j], ssd_conv_b[j][None, :]], axis=0)], axis=1)
            cw_hy = _pad_rows(jnp.concatenate([hy_conv_w[j], hy_conv_b[j][None, :]], axis=0), 8)
            zx, dt, x0, u = _inproj_even_call(xs, mods_l, g1, hy_in_w, j, _pad_rows(cw, 8), dtb,
                                              hy_in_w[j][:, COL_HY:].astype(bf16), cw_hy)

            alog = _pad_lanes(ssd_a_log[j].reshape(1, 2 * HEADS), DT_LANES)
            dskip = jnp.repeat(ssd_d[j], HEAD_P)[None, :]
            yn = _ssd_call(zx, dt, alog, dskip, ssd_norm_g[j][None, :], e_f, e_b)

            segs = []
            for seg, nb in ((SEQ, HY_NB), (CTX, 1)):
                wtab, wtab_t, zt, win = tabs[seg]
                ka, kb = _hyfilter_call(zt, _pad_rows(hy_w1[j], 128), hy_b1[j][None, :], hy_w2[j],
                                        hy_b2[j][None, :], hy_w3[j], hy_b3[j][None, :],
                                        _pad_rows(hy_freq[j], 8), hy_w4[j], win, wtab, seg, nb)
                segs.append((wtab, wtab_t, ka, kb))
            gh = _hyconv_call(u, x0, hy_bias[j][None, :], *segs)
            w_out = hy_out_w[j].astype(bf16)
            xs = _outproj_call(xs, mods_l, [yn, gh], [w_out[:D], w_out[D:]], "outproj_even")
        else:
            wq = attn_qkv_w[j]
            wk = wq[:, QW:QW + KVW].reshape(D, KV_HEADS, HEAD_P)
            wv = wq[:, QW + KVW:].reshape(D, KV_HEADS, HEAD_P)
            w_kv = jnp.concatenate([wk, wv], axis=2).reshape(D, 2 * KVW)
            w_qkv = jnp.concatenate([wq[:, :QW], w_kv], axis=1).astype(bf16)
            qkv = _nmm_call(xs, mods_l, g1, w_qkv, 512, "qkv_proj")
            o = _attn_call(qkv, attn_sinks[j], cos_t, sin_t, perm, need_ctx=layer < DEPTH - 1)
            xs = _outproj_call(xs, mods_l, [o], [attn_out_w[j].astype(bf16)], "outproj_attn")
        xs = _ffn_call(xs, mods_l, norm2_g[layer][None, :], ffn_w_in[layer].astype(bf16),
                       ffn_w_out[layer].astype(bf16),
                       final_g=final_g[None, :] if layer == DEPTH - 1 else None)
    return xs
```

```python
import functools
import math

import jax
import jax.numpy as jnp
from jax import lax
from jax.experimental import pallas as pl
from jax.experimental.pallas import tpu as pltpu

f32 = jnp.float32
bf16 = jnp.bfloat16

D = 1024
BATCH = 8
SEQ = 2048
DEPTH = 4
GRID_W = 64
CTX = 256
S = SEQ + CTX
EPS = 1e-6
N_MOD = 6
MOD_ROWS = 16
CTX_MOD = 8

HEADS = 16
HEAD_P = 64
STATE = 128
GROUPS = 2
CHUNK = 128
XBC = D + 2 * GROUPS * STATE
ZX = D + XBC
DT_LANES = 128
HY_BANDS = 16
HY_EMB = 1 + 2 * HY_BANDS
HY_FFN = 64
COL_XBC = D
COL_DT = COL_XBC + XBC
COL_HY = COL_DT + 2 * HEADS

Q_HEADS = 16
KV_HEADS = 4
ATT_GROUP = Q_HEADS // KV_HEADS
QW = Q_HEADS * HEAD_P
KVW = KV_HEADS * HEAD_P
WINDOW = 128
ABLK = 128
ROPE_BASE = 10000.0
FFN_H = -(-8 * D // (3 * 256)) * 256
LOG2E = math.log2(math.e)

RC = 256
NMM_RC = 768
VMEM_CAP = 56 * 1024 * 1024


def _cparams(sem, vmem_mb):
    return pltpu.CompilerParams(dimension_semantics=sem,
                                vmem_limit_bytes=min(int(vmem_mb * 1024 * 1024), VMEM_CAP))


def _silu(v):
    return v * jax.nn.sigmoid(v)


def _softplus(v):
    return jnp.maximum(v, 0.0) + jnp.log1p(jnp.exp(-jnp.abs(v)))


def _bdot(a, b):
    return jnp.dot(a, b, preferred_element_type=f32)


def _bdot_nt(a, b):
    return lax.dot_general(a, b, (((1,), (1,)), ((), ())), preferred_element_type=f32)


def _row_chunks(n_rows, body, unroll=1):
    def step(c, carry):
        body(pl.multiple_of(c * RC, RC))
        return carry
    lax.fori_loop(0, n_rows // RC, step, 0, unroll=unroll)


def _norm_mod(x, g, mod, shift_row, scale_row, row0):
    ms = jnp.mean(x * x, axis=-1, keepdims=True)
    scale = _mod_row(mod, scale_row, row0)
    return x * lax.rsqrt(ms + EPS) * (g * (1.0 + scale)) + _mod_row(mod, shift_row, row0)


def _mod_row(mod, k, row0):
    return jnp.where(row0 >= SEQ, mod[CTX_MOD + k:CTX_MOD + k + 1], mod[k:k + 1])


def _norm_mod_to_scratch(x_ref, g_ref, mod_ref, h_scr):
    def body(r0):
        h_scr[pl.ds(r0, RC), :] = _norm_mod(x_ref[0, pl.ds(r0, RC), :], g_ref[...], mod_ref[0],
                                            0, 1, r0).astype(bf16)
    _row_chunks(S, body)


CONV_ROWS = 64


def _conv3_rows(p_scr, r0, cw):
    n, cols = CONV_ROWS, p_scr.shape[1]
    zero = jnp.zeros((8, cols), f32)
    tiles = [p_scr[r0 + k:r0 + k + 8, :] for k in range(0, n, 8)]
    above = zero if r0 in (0, SEQ) else p_scr[r0 - 8:r0, :]
    below = zero if r0 + n in (SEQ, S) else p_scr[r0 + n:r0 + n + 8, :]
    sub = lax.broadcasted_iota(jnp.int32, (8, cols), 0)
    down = [pltpu.roll(t, 1, 0) for t in [above] + tiles]
    up = [pltpu.roll(t, 7, 0) for t in tiles + [below]]
    out = []
    for k, t in enumerate(tiles):
        prev = jnp.where(sub == 0, down[k], down[k + 1])
        nxt = jnp.where(sub == 7, up[k + 1], up[k])
        out.append(prev * cw[0:1] + t * cw[1:2] + nxt * cw[2:3] + cw[3:4])
    return jnp.concatenate(out, axis=0)


def _project_then_conv(mm, piece):
    done = 0
    for r0 in range(0, S, NMM_RC):
        while done + CONV_ROWS < r0:
            piece(done)
            done += CONV_ROWS
        mm(r0)
    while done < S:
        piece(done)
        done += CONV_ROWS


def _mod_kernel(c_ref, w_ref, b_ref, o_ref):
    sc = _silu(c_ref[...]).astype(bf16)
    o_ref[0] = _bdot(sc, w_ref[0].astype(bf16)) + b_ref[0]


def _mod_call(cc, ada_w, ada_b):
    tn = 1024
    return pl.pallas_call(
        _mod_kernel,
        grid=(DEPTH, N_MOD * D // tn),
        in_specs=[pl.BlockSpec((MOD_ROWS, D), lambda l, j: (0, 0)),
                  pl.BlockSpec((1, D, tn), lambda l, j: (l, 0, j)),
                  pl.BlockSpec((1, 1, tn), lambda l, j: (l, 0, j))],
        out_specs=pl.BlockSpec((1, MOD_ROWS, tn), lambda l, j: (l, 0, j)),
        out_shape=jax.ShapeDtypeStruct((DEPTH, MOD_ROWS, N_MOD * D), f32),
        compiler_params=_cparams(("parallel", "parallel"), 24),
        name="adaln_mod",
    )(cc, ada_w, ada_b.reshape(DEPTH, 1, N_MOD * D))


def _nmm_kernel(x_ref, mod_ref, g_ref, w_ref, o_ref, h_scr):
    @pl.when(pl.program_id(1) == 0)
    def _():
        _norm_mod_to_scratch(x_ref, g_ref, mod_ref, h_scr)

    def body(c, carry):
        r0 = pl.multiple_of(c * NMM_RC, NMM_RC)
        o_ref[0, pl.ds(r0, NMM_RC), :] = _bdot(h_scr[pl.ds(r0, NMM_RC), :], w_ref[...]).astype(o_ref.dtype)
        return carry
    lax.fori_loop(0, S // NMM_RC, body, 0)


def _nmm_call(xs, mods_l, g, w, tn, name):
    n = w.shape[1]
    return pl.pallas_call(
        _nmm_kernel,
        grid=(BATCH, n // tn),
        in_specs=[pl.BlockSpec((1, S, D), lambda b, j: (b, 0, 0)),
                  pl.BlockSpec((1, MOD_ROWS, D), lambda b, j: (b, 0, 0)),
                  pl.BlockSpec((1, D), lambda b, j: (0, 0)),
                  pl.BlockSpec((D, tn), lambda b, j: (0, j))],
        out_specs=pl.BlockSpec((1, S, tn), lambda b, j: (b, 0, j)),
        out_shape=jax.ShapeDtypeStruct((BATCH, S, n), bf16),
        scratch_shapes=[pltpu.VMEM((S, D), bf16)],
        compiler_params=_cparams(("parallel", "arbitrary"), 40),
        name=name,
    )(xs, mods_l, g, w)


ZX_TN = 512
NZ_T = D // ZX_TN
ZX_TILES = ZX // ZX_TN
HY_TN = 256
HY_TILES = D // HY_TN
assert COL_DT == ZX and COL_DT % DT_LANES == 0


def _inproj_even_kernel(x_ref, mod_ref, g_ref, w_ref, cw_ref, wdt_ref, dtb_ref, w0_ref, w1_ref, w2_ref,
                        c0_ref, c1_ref, c2_ref, zx_ref, dt_ref, x0_ref, u_ref,
                        h_scr, wb_scr, p0_scr, p1_scr, p2_scr):
    j = pl.program_id(1)

    @pl.when(j == 0)
    def _():
        _norm_mod_to_scratch(x_ref, g_ref, mod_ref, h_scr)
        wdt = wdt_ref[0].astype(bf16)

        def dt_body(c, carry):
            r0 = pl.multiple_of(c * NMM_RC, NMM_RC)
            sp = _softplus(_bdot(h_scr[pl.ds(r0, NMM_RC), :], wdt) + dtb_ref[...])
            lane = lax.broadcasted_iota(jnp.int32, sp.shape, 1)
            dt_ref[0, pl.ds(r0, NMM_RC), :] = jnp.where(lane < 2 * HEADS, sp, 0.0)
            return carry
        lax.fori_loop(0, S // NMM_RC, dt_body, 0)

    @pl.when(j < ZX_TILES)
    def _():
        wb_scr[...] = w_ref[0].astype(bf16)

    @pl.when(j < NZ_T)
    def _():
        def body(c, carry):
            r0 = pl.multiple_of(c * NMM_RC, NMM_RC)
            zx_ref[0, pl.ds(r0, NMM_RC), :] = _bdot(h_scr[pl.ds(r0, NMM_RC), :], wb_scr[...]).astype(bf16)
            return carry
        lax.fori_loop(0, S // NMM_RC, body, 0)

    @pl.when(jnp.logical_and(j >= NZ_T, j < ZX_TILES))
    def _():
        def mm(r0):
            p0_scr[r0:r0 + NMM_RC, :] = _bdot(h_scr[r0:r0 + NMM_RC, :], wb_scr[...])

        def piece(a):
            zx_ref[0, a:a + CONV_ROWS, :] = _silu(_conv3_rows(p0_scr, a, cw_ref[...])).astype(bf16)
        _project_then_conv(mm, piece)

    @pl.when(j >= ZX_TILES)
    def _():
        px0 = p0_scr.at[:, 0:HY_TN]

        def mm_x0(r0):
            px0[r0:r0 + NMM_RC, :] = _bdot(h_scr[r0:r0 + NMM_RC, :], w0_ref[...])

        def piece_x0(a):
            x0_ref[0, a:a + CONV_ROWS, :] = _conv3_rows(px0, a, c0_ref[...]).astype(bf16)
        _project_then_conv(mm_x0, piece_x0)

        def mm_u(r0):
            p1_scr[r0:r0 + NMM_RC, :] = _bdot(h_scr[r0:r0 + NMM_RC, :], w1_ref[...])
            p2_scr[r0:r0 + NMM_RC, :] = _bdot(h_scr[r0:r0 + NMM_RC, :], w2_ref[...])

        def piece_u(a):
            x1 = _conv3_rows(p1_scr, a, c1_ref[...])
            v = _conv3_rows(p2_scr, a, c2_ref[...])
            u_ref[0, a:a + CONV_ROWS, :] = (x1 * v).astype(bf16)
        _project_then_conv(mm_u, piece_u)


def _inproj_even_call(xs, mods_l, g, w_all, jl, cw, dtb, w_hy, cw_hy):
    zt = lambda j: jnp.minimum(j, ZX_TILES - 1)
    ht = lambda j: jnp.maximum(j - ZX_TILES, 0)
    wspec = lambda k: pl.BlockSpec((D, HY_TN), lambda b, j, k=k: (0, ht(j) + k * HY_TILES))
    cspec = lambda k: pl.BlockSpec((8, HY_TN), lambda b, j, k=k: (0, ht(j) + k * HY_TILES))
    return pl.pallas_call(
        _inproj_even_kernel,
        grid=(BATCH, ZX_TILES + HY_TILES),
        in_specs=[pl.BlockSpec((1, S, D), lambda b, j: (b, 0, 0)),
                  pl.BlockSpec((1, MOD_ROWS, D), lambda b, j: (b, 0, 0)),
                  pl.BlockSpec((1, D), lambda b, j: (0, 0)),
                  pl.BlockSpec((1, D, ZX_TN), lambda b, j: (jl, 0, zt(j))),
                  pl.BlockSpec((8, ZX_TN), lambda b, j: (0, zt(j))),
                  pl.BlockSpec((1, D, DT_LANES), lambda b, j: (jl, 0, COL_DT // DT_LANES)),
                  pl.BlockSpec((1, DT_LANES), lambda b, j: (0, 0)),
                  wspec(0), wspec(1), wspec(2), cspec(0), cspec(1), cspec(2)],
        out_specs=[pl.BlockSpec((1, S, ZX_TN), lambda b, j: (b, 0, zt(j))),
                   pl.BlockSpec((1, S, DT_LANES), lambda b, j: (b, 0, 0)),
                   pl.BlockSpec((1, S, HY_TN), lambda b, j: (b, 0, ht(j))),
                   pl.BlockSpec((1, S, HY_TN), lambda b, j: (b, 0, ht(j)))],
        out_shape=[jax.ShapeDtypeStruct((BATCH, S, ZX), bf16),
                   jax.ShapeDtypeStruct((BATCH, S, DT_LANES), f32),
                   jax.ShapeDtypeStruct((BATCH, S, D), bf16),
                   jax.ShapeDtypeStruct((BATCH, S, D), bf16)],
        scratch_shapes=[pltpu.VMEM((S, D), bf16), pltpu.VMEM((D, ZX_TN), bf16),
                        pltpu.VMEM((S, ZX_TN), f32), pltpu.VMEM((S, HY_TN), f32),
                        pltpu.VMEM((S, HY_TN), f32)],
        compiler_params=_cparams(("parallel", "arbitrary"), 56),
        name="inproj_even",
    )(xs, mods_l, g, w_all, cw, w_all, dtb, w_hy, w_hy, w_hy, cw_hy, cw_hy, cw_hy)


N_CHUNK = S // CHUNK
LAT_CHUNKS = SEQ // CHUNK


def _split3_dot(t, a):
    a1 = a.astype(bf16)
    r1 = a - a1.astype(f32)
    a2 = r1.astype(bf16)
    a3 = (r1 - a2.astype(f32)).astype(bf16)
    return _bdot(t, a1) + _bdot(t, a2) + _bdot(t, a3)


def _ssd_chunk(zx_ref, dt_ref, aneg, e_ref, y_scr, st_scr, r0, backward):
    lane_off = HEADS if backward else 0
    x = zx_ref[0, pl.ds(r0, CHUNK), D:2 * D].astype(f32)
    bm = zx_ref[0, pl.ds(r0, CHUNK), 2 * D:2 * D + GROUPS * STATE]
    cm = zx_ref[0, pl.ds(r0, CHUNK), 2 * D + GROUPS * STATE:ZX]
    dt = dt_ref[0, pl.ds(r0, CHUNK), :]
    a = dt * aneg

    ri = lax.broadcasted_iota(jnp.int32, (CHUNK, CHUNK), 0)
    ci = lax.broadcasted_iota(jnp.int32, (CHUNK, CHUNK), 1)
    causal = (ci >= ri) if backward else (ci <= ri)
    tmat = jnp.where(causal, 1.0, 0.0).astype(bf16)
    acum = _split3_dot(tmat, a)
    acum_t = acum.T
    tot = acum[0:1] if backward else acum[CHUNK - 1:CHUNK]
    dte = jnp.exp2(tot - acum)
    eac = jnp.exp2(acum)
    cdec = jnp.broadcast_to(jnp.exp2(tot), (16, DT_LANES))

    e = e_ref[...]
    dt_x = _bdot(dt.astype(bf16), e)
    dte_x = _bdot(dte.astype(bf16), e)
    eac_x = _bdot(eac.astype(bf16), e)
    cdec_x = _bdot(cdec.astype(bf16), e)[0:1]

    xdt = x * dt_x
    xs_b = (xdt * dte_x).astype(bf16)
    lane = lax.broadcasted_iota(jnp.int32, (CHUNK, 2 * HEAD_P), 1)
    low = lane < HEAD_P
    hg = HEADS // GROUPS
    gw = hg * HEAD_P
    for g in range(GROUPS):
        bg = bm[:, g * STATE:(g + 1) * STATE]
        cg = cm[:, g * STATE:(g + 1) * STATE]
        cb = _bdot_nt(cg, bg)
        hin = st_scr[:, g * gw:(g + 1) * gw]
        yoff = _bdot(cg, hin.astype(bf16)) * eac_x[:, g * gw:(g + 1) * gw]
        for hp in range(hg // 2):
            h0 = g * hg + 2 * hp
            ms = []
            for h in (h0, h0 + 1):
                l0 = lane_off + h
                seg = acum[:, l0:l0 + 1] - acum_t[l0:l0 + 1, :]
                ms.append(cb * jnp.exp2(jnp.where(causal, seg, -jnp.inf)))
            lhs = jnp.concatenate(ms, axis=1).astype(bf16)
            xp = xdt[:, h0 * HEAD_P:(h0 + 2) * HEAD_P]
            rhs = jnp.concatenate([jnp.where(low, xp, 0.0), jnp.where(low, 0.0, xp)],
                                  axis=0).astype(bf16)
            yd = _bdot(lhs, rhs) + yoff[:, hp * 2 * HEAD_P:(hp + 1) * 2 * HEAD_P]
            cols = slice(h0 * HEAD_P, (h0 + 2) * HEAD_P)
            y_scr[pl.ds(r0, CHUNK), cols] = y_scr[pl.ds(r0, CHUNK), cols] + yd
        bg_t = bg.astype(f32).T.astype(bf16)
        st_scr[:, g * gw:(g + 1) * gw] = (hin * cdec_x[:, g * gw:(g + 1) * gw]
                                         + _bdot(bg_t, xs_b[:, g * gw:(g + 1) * gw]))


def _ssd_kernel(zx_ref, dt_ref, alog_ref, dskip_ref, ng_ref, ef_ref, eb_ref, o_ref,
                y_scr, sf_scr, sb_scr):
    aneg = -jnp.exp(alog_ref[...]) * LOG2E
    sf_scr[...] = jnp.zeros_like(sf_scr)
    sb_scr[...] = jnp.zeros_like(sb_scr)

    def init(c, carry):
        r0 = pl.multiple_of(c * CHUNK, CHUNK)
        y_scr[pl.ds(r0, CHUNK), :] = dskip_ref[...] * zx_ref[0, pl.ds(r0, CHUNK), D:2 * D].astype(f32)
        return carry
    lax.fori_loop(0, N_CHUNK, init, 0)

    def step(t, carry):
        cf = jnp.where(t < N_CHUNK - LAT_CHUNKS, LAT_CHUNKS + t, t - (N_CHUNK - LAT_CHUNKS))
        cbk = N_CHUNK - 1 - t
        _ssd_chunk(zx_ref, dt_ref, aneg, ef_ref, y_scr, sf_scr, pl.multiple_of(cf * CHUNK, CHUNK), False)
        _ssd_chunk(zx_ref, dt_ref, aneg, eb_ref, y_scr, sb_scr, pl.multiple_of(cbk * CHUNK, CHUNK), True)
        return carry
    lax.fori_loop(0, N_CHUNK, step, 0, unroll=2)

    def fin(c, carry):
        r0 = pl.multiple_of(c * CHUNK, CHUNK)
        t = y_scr[pl.ds(r0, CHUNK), :] * _silu(zx_ref[0, pl.ds(r0, CHUNK), 0:D].astype(f32))
        ms = jnp.mean(t * t, axis=-1, keepdims=True)
        o_ref[0, pl.ds(r0, CHUNK), :] = (t * lax.rsqrt(ms + EPS) * ng_ref[...]).astype(bf16)
        return carry
    lax.fori_loop(0, N_CHUNK, fin, 0)


def _ssd_call(zx, dt, alog, dskip, ng, e_f, e_b):
    return pl.pallas_call(
        _ssd_kernel,
        grid=(BATCH,),
        in_specs=[pl.BlockSpec((1, S, ZX), lambda b: (b, 0, 0)),
                  pl.BlockSpec((1, S, DT_LANES), lambda b: (b, 0, 0)),
                  pl.BlockSpec((1, DT_LANES), lambda b: (0, 0)),
                  pl.BlockSpec((1, D), lambda b: (0, 0)),
                  pl.BlockSpec((1, D), lambda b: (0, 0)),
                  pl.BlockSpec((DT_LANES, D), lambda b: (0, 0)),
                  pl.BlockSpec((DT_LANES, D), lambda b: (0, 0))],
        out_specs=pl.BlockSpec((1, S, D), lambda b: (b, 0, 0)),
        out_shape=jax.ShapeDtypeStruct((BATCH, S, D), bf16),
        scratch_shapes=[pltpu.VMEM((S, D), f32),
                        pltpu.VMEM((STATE, D), f32),
                        pltpu.VMEM((STATE, D), f32)],
        compiler_params=_cparams(("parallel",), 56),
        name="ssd_scan",
    )(zx, dt, alog, dskip, ng, e_f, e_b)


def _hp_dot(a, b):
    return jnp.dot(a, b, precision=lax.Precision.HIGHEST, preferred_element_type=f32)


def _hyfilter_kernel(z_ref, w1_ref, b1_ref, w2_ref, b2_ref, w3_ref, b3_ref, fr_ref, w4f_ref, w4b_ref,
                     win_ref, w_ref, ka_ref, kb_ref, h_scr, tap_scr, tb_scr, pp_scr, pq_scr, *, seg, nb):
    lb = seg // nb
    tc = tap_scr.shape[1]

    @pl.when(pl.program_id(0) == 0)
    def _():
        fr = fr_ref[...]

        def hidden(c, carry):
            r0 = pl.multiple_of(c * RC, RC)
            h = jnp.sin(fr[0:1] * (_hp_dot(z_ref[pl.ds(r0, RC), :], w1_ref[...]) + b1_ref[...]))
            h = jnp.sin(fr[1:2] * (_hp_dot(h, w2_ref[...]) + b2_ref[...]))
            h_scr[pl.ds(r0, RC), :] = jnp.sin(fr[2:3] * (_hp_dot(h, w3_ref[...]) + b3_ref[...]))
            return carry
        lax.fori_loop(0, 2 * seg // RC, hidden, 0)

    def taps(w4_ref):
        def body(c, ssq):
            r0 = pl.multiple_of(c * RC, RC)
            t = _hp_dot(h_scr[pl.ds(r0, RC), :], w4_ref[...]) * win_ref[pl.ds(r0, RC), :]
            row = r0 + lax.broadcasted_iota(jnp.int32, t.shape, 0)
            t = jnp.where(row == 0, 0.0, t)
            tap_scr[pl.ds(r0, RC), :] = t
            return ssq + jnp.sum(t * t, axis=0, keepdims=True)
        return body
    ssq = lax.fori_loop(0, seg // RC, taps(w4b_ref), jnp.zeros((1, tc), f32))
    nrm = lax.rsqrt(lax.fori_loop(seg // RC, 2 * seg // RC, taps(w4f_ref), ssq) + 1e-6)

    def scale(c, carry):
        r0 = pl.multiple_of(c * RC, RC)
        tb_scr[pl.ds(r0, RC), :] = (tap_scr[pl.ds(r0, RC), :] * nrm).astype(bf16)
        return carry
    lax.fori_loop(0, 2 * seg // RC, scale, 0)

    row = lax.broadcasted_iota(jnp.int32, (lb, tc), 0)
    sgn = jnp.where((row & 1) == 0, 1.0, -1.0)
    for s in range(2 * nb):
        blk = tb_scr[s * lb:(s + 1) * lb, :]
        p = _bdot(w_ref[0:lb, :], blk)
        q = _bdot(w_ref[lb:2 * lb, :], blk)
        if s > 0:
            t0 = tb_scr[(s - 1) * lb:(s - 1) * lb + 16, :][0:1].astype(f32)
            ka_ref[s - 1] = (p + sgn * (pp_scr[...] - t0)).astype(ka_ref.dtype)
            kb_ref[s - 1] = jnp.where(row == 0, q + pq_scr[...] - t0,
                                      q + sgn * pq_scr[...]).astype(kb_ref.dtype)
        pp_scr[...] = p
        pq_scr[...] = q


HY_TC = 256
HY_NB = 4
HY_SPEC_DT = bf16


def _hyfilter_call(zt, w1, b1, w2, b2, w3, b3, fr, w4, win, wtab, seg, nb):
    tc, lb = HY_TC, seg // nb
    nct = D // tc
    const = lambda shape: pl.BlockSpec(shape, lambda c: (0, 0))
    return pl.pallas_call(
        functools.partial(_hyfilter_kernel, seg=seg, nb=nb),
        grid=(nct,),
        in_specs=[const((2 * seg, 128)), const((128, HY_FFN)), const((1, HY_FFN)),
                  const((HY_FFN, HY_FFN)), const((1, HY_FFN)),
                  const((HY_FFN, HY_FFN)), const((1, HY_FFN)), const((8, HY_FFN)),
                  pl.BlockSpec((HY_FFN, tc), lambda c: (0, c)),
                  pl.BlockSpec((HY_FFN, tc), lambda c: (0, c + nct)),
                  pl.BlockSpec((2 * seg, tc), lambda c: (0, c)),
                  const((2 * lb, lb))],
        out_specs=[pl.BlockSpec((2 * nb - 1, lb, tc), lambda c: (0, 0, c)),
                   pl.BlockSpec((2 * nb - 1, lb, tc), lambda c: (0, 0, c))],
        out_shape=[jax.ShapeDtypeStruct((2 * nb - 1, lb, D), HY_SPEC_DT)] * 2,
        scratch_shapes=[pltpu.VMEM((2 * seg, HY_FFN), f32),
                        pltpu.VMEM((2 * seg, tc), f32), pltpu.VMEM((2 * seg, tc), bf16),
                        pltpu.VMEM((lb, tc), f32), pltpu.VMEM((lb, tc), f32)],
        compiler_params=_cparams(("arbitrary",), 40),
        name=f"hyena_filter_{seg}",
    )(zt, w1, b1, w2, b2, w3, b3, fr, w4, w4, win, wtab)


HY_PIECE = 64


def _hyconv_segment(u_ref, x0_ref, bias_ref, o_ref, w_ref, wt_ref, ka_ref, kb_ref,
                    ua_scr, ub_scr, ya_scr, yb_scr, row0, nb, lb):
    for i in range(nb):
        ui = u_ref[0, row0 + i * lb:row0 + (i + 1) * lb, :]
        ua_scr[i, 0:lb, :] = _bdot(w_ref[0:lb, :], ui).astype(HY_SPEC_DT)
        ub_scr[i, 0:lb, :] = _bdot(w_ref[lb:2 * lb, :], ui).astype(HY_SPEC_DT)

    is0 = lax.broadcasted_iota(jnp.int32, (HY_PIECE, ka_ref.shape[2]), 0) == 0
    for o in range(nb):
        for r in range(0, lb, HY_PIECE):
            rows = slice(r, r + HY_PIECE)
            ya = yb = None
            for i in range(nb):
                a, b = ua_scr[i, rows, :], ub_scr[i, rows, :]
                ka, kb = ka_ref[o - i + nb - 1, rows, :], kb_ref[o - i + nb - 1, rows, :]
                if r == 0:
                    pa = jnp.where(is0, a * ka, a * ka - b * kb)
                    pb = jnp.where(is0, b * kb, a * kb + b * ka)
                else:
                    pa, pb = a * ka - b * kb, a * kb + b * ka
                ya = pa if ya is None else ya + pa
                yb = pb if yb is None else yb + pb
            scale = jnp.where(is0, 0.5 / lb, 1.0 / lb) if r == 0 else 1.0 / lb
            ya_scr[o, rows, :] = (ya * scale).astype(bf16)
            yb_scr[o, rows, :] = (yb * scale).astype(bf16)
        y = _bdot(wt_ref[:, 0:lb], ya_scr[o, 0:lb, :]) + _bdot(wt_ref[:, lb:2 * lb], yb_scr[o, 0:lb, :])
        rows = slice(row0 + o * lb, row0 + (o + 1) * lb)
        y = y + bias_ref[...] * u_ref[0, rows, :].astype(f32)
        o_ref[0, rows, :] = (x0_ref[0, rows, :].astype(f32) * y).astype(bf16)


def _hyconv_kernel(u_ref, x0_ref, bias_ref, w_ref, wt_ref, ka_ref, kb_ref, cw_ref, cwt_ref, cka_ref,
                   ckb_ref, o_ref, ua_scr, ub_scr, ya_scr, yb_scr):
    scr = (ua_scr, ub_scr, ya_scr, yb_scr)
    _hyconv_segment(u_ref, x0_ref, bias_ref, o_ref, w_ref, wt_ref, ka_ref, kb_ref, *scr,
                    row0=0, nb=HY_NB, lb=SEQ // HY_NB)
    _hyconv_segment(u_ref, x0_ref, bias_ref, o_ref, cw_ref, cwt_ref, cka_ref, ckb_ref, *scr,
                    row0=SEQ, nb=1, lb=CTX)


def _hyconv_call(u, x0, bias, lat, ctx):
    tc, lb = HY_TC, SEQ // HY_NB
    full = lambda a: pl.BlockSpec(a.shape, lambda c, b: (0,) * a.ndim)
    spec = lambda a: pl.BlockSpec(a.shape[:2] + (tc,), lambda c, b: (0, 0, c))
    seq = pl.BlockSpec((1, S, tc), lambda c, b: (b, 0, c))
    return pl.pallas_call(
        _hyconv_kernel,
        grid=(D // tc, BATCH),
        in_specs=[seq, seq, pl.BlockSpec((1, tc), lambda c, b: (0, c)),
                  full(lat[0]), full(lat[1]), spec(lat[2]), spec(lat[3]),
                  full(ctx[0]), full(ctx[1]), spec(ctx[2]), spec(ctx[3])],
        out_specs=seq,
        out_shape=jax.ShapeDtypeStruct((BATCH, S, D), bf16),
        scratch_shapes=[pltpu.VMEM((HY_NB, lb, tc), HY_SPEC_DT), pltpu.VMEM((HY_NB, lb, tc), HY_SPEC_DT),
                        pltpu.VMEM((HY_NB, lb, tc), bf16), pltpu.VMEM((HY_NB, lb, tc), bf16)],
        compiler_params=_cparams(("parallel", "parallel"), 40),
        name="hyena_conv",
    )(u, x0, bias, *lat, *ctx)


def _rope(xb, perm, cos_t, sin_t):
    n = xb.shape[1]
    return xb.astype(f32) * cos_t + _bdot(xb, perm[0:n, 0:n]) * sin_t


assert WINDOW == ABLK
A_SPAN = ABLK + 2 * WINDOW
A_KEYS = A_SPAN + CTX
A_COLS = ATT_GROUP * ABLK


def _attn_kernel(sink_ref, q_ref, kv_ref, cos_ref, sin_ref, perm_ref, o_ref, q_scr, k_scr, vt_scr,
                 s_scr, p_scr):
    g = pl.program_id(1)
    qscale = HEAD_P ** -0.5 * LOG2E

    def prep(r0, with_rope):
        qb = q_ref[0, pl.ds(r0, RC), :]
        kvb = kv_ref[0, pl.ds(r0, RC), :]
        kv = kvb.astype(f32)
        if with_rope:
            cos_t, sin_t = cos_ref[pl.ds(r0, RC), :], sin_ref[pl.ds(r0, RC), :]
            q = _rope(qb, perm_ref[...], cos_t, sin_t) * qscale
            k = _rope(kvb, perm_ref[...], cos_t[:, 0:2 * HEAD_P], sin_t[:, 0:2 * HEAD_P])
        else:
            q, k = qb.astype(f32) * qscale, kv
        low = lax.broadcasted_iota(jnp.int32, kv.shape, 1) < HEAD_P
        k_scr[pl.ds(r0, RC), :] = jnp.where(low, k, 0.0).astype(bf16)
        for r in range(ATT_GROUP):
            slab = q[:, (r // 2) * 2 * HEAD_P:(r // 2 + 1) * 2 * HEAD_P]
            if r % 2:
                slab = pltpu.roll(slab, HEAD_P, 1)
            q_scr[r, pl.ds(r0, RC), :] = jnp.where(low, slab, 0.0).astype(bf16)
        v_t = kv.T[HEAD_P:2 * HEAD_P, :].astype(bf16)
        c0 = r0 // ABLK
        for h in range(RC // ABLK):
            vt_scr[c0 + h] = v_t[:, h * ABLK:(h + 1) * ABLK]
    _row_chunks(SEQ, lambda r0: prep(r0, True), unroll=2)
    for r0 in range(SEQ, S, RC):
        prep(r0, False)

    kmq = (lax.broadcasted_iota(jnp.int32, (ABLK, A_COLS), 0)
           - (lax.broadcasted_iota(jnp.int32, (ABLK, A_COLS), 1) & (ABLK - 1)))

    def block(q0, start, masks, slot):
        qstack = jnp.concatenate([q_scr[r, pl.ds(q0, ABLK), :] for r in range(ATT_GROUP)], axis=0)
        chunks = []
        if masks:
            s_loc = _bdot_nt(k_scr[pl.ds(start, len(masks) * ABLK), :], qstack)
            for c, kind in enumerate(masks):
                s_c = s_loc[c * ABLK:(c + 1) * ABLK]
                if kind is not None:
                    s_c = jnp.where(kmq >= 0 if kind == 'ge' else kmq <= 0, s_c, -jnp.inf)
                s_scr[slot, c * ABLK:(c + 1) * ABLK, :] = s_c
                chunks.append((c * ABLK, start // ABLK + c))
        n_loc = len(chunks) * ABLK
        s_scr[slot, n_loc:n_loc + CTX, :] = _bdot_nt(k_scr[SEQ:S, :], qstack)
        chunks += [(n_loc + c * ABLK, SEQ // ABLK + c) for c in range(CTX // ABLK)]

        inv = []
        for r in range(ATT_GROUP):
            cols = slice(r * ABLK, (r + 1) * ABLK)
            sink = sink_ref[g * ATT_GROUP + r] * LOG2E
            m = jnp.full((1, ABLK), sink, f32)
            for row, _ in chunks:
                m = jnp.maximum(m, jnp.max(s_scr[slot, row:row + ABLK, cols], axis=0, keepdims=True))
            den = jnp.exp2(sink - m)
            for row, _ in chunks:
                p = jnp.exp2(s_scr[slot, row:row + ABLK, cols] - m)
                den = den + jnp.sum(p, axis=0, keepdims=True)
                p_scr[slot, row:row + ABLK, cols] = p.astype(bf16)
            inv.append(1.0 / den)

        o_t = None
        for row, vc in chunks:
            part = _bdot(vt_scr[vc], p_scr[slot, row:row + ABLK, :])
            o_t = part if o_t is None else o_t + part
        o_t = jnp.concatenate([o_t[:, r * ABLK:(r + 1) * ABLK] * inv[r] for r in range(ATT_GROUP)], axis=0)
        o_ref[0, pl.ds(q0, ABLK), :] = o_t.T.astype(bf16)

    n_blk = SEQ // ABLK
    block(0, 0, (None, 'le'), 0)
    block(SEQ - ABLK, SEQ - 2 * ABLK, ('ge', None), 1)

    def lat_pair(j, carry):
        for slot in range(2):
            q0 = pl.multiple_of((1 + 2 * j + slot) * ABLK, ABLK)
            block(q0, pl.multiple_of(q0 - WINDOW, ABLK), ('ge', None, 'le'), slot)
        return carry
    lax.fori_loop(0, (n_blk - 2) // 2, lat_pair, 0, unroll=True)

    for slot, q0 in enumerate(range(SEQ, o_ref.shape[1], ABLK)):
        block(q0, 0, (), slot % 2)


def _attn_call(qkv, sinks, cos_t, sin_t, perm, need_ctx):
    gw = ATT_GROUP * HEAD_P
    rows = S if need_ctx else SEQ
    return pl.pallas_call(
        _attn_kernel,
        grid_spec=pltpu.PrefetchScalarGridSpec(
            num_scalar_prefetch=1,
            grid=(BATCH, KV_HEADS),
            in_specs=[pl.BlockSpec((1, S, gw), lambda b, g, s: (b, 0, g)),
                      pl.BlockSpec((1, S, 2 * HEAD_P), lambda b, g, s: (b, 0, QW // (2 * HEAD_P) + g)),
                      pl.BlockSpec((SEQ, gw), lambda b, g, s: (0, 0)),
                      pl.BlockSpec((SEQ, gw), lambda b, g, s: (0, 0)),
                      pl.BlockSpec((gw, gw), lambda b, g, s: (0, 0))],
            out_specs=pl.BlockSpec((1, rows, gw), lambda b, g, s: (b, 0, g)),
            scratch_shapes=[pltpu.VMEM((ATT_GROUP, S, 2 * HEAD_P), bf16),
                            pltpu.VMEM((S, 2 * HEAD_P), bf16),
                            pltpu.VMEM((S // ABLK, HEAD_P, ABLK), bf16),
                            pltpu.VMEM((2, A_KEYS, A_COLS), f32),
                            pltpu.VMEM((2, A_KEYS, A_COLS), bf16)]),
        out_shape=jax.ShapeDtypeStruct((BATCH, rows, QW), bf16),
        compiler_params=_cparams(("parallel", "parallel"), 40),
        name="window_attn",
    )(sinks, qkv, qkv, cos_t, sin_t, perm)


RES_TM = 768
RES_TM_LAT = 1024


def _outproj_kernel(x_ref, mod_ref, *rest, n_in, tm):
    a_refs, w_refs, o_ref = rest[:n_in], rest[n_in:2 * n_in], rest[2 * n_in]
    acc = _bdot(a_refs[0][0], w_refs[0][...])
    for a_ref, w_ref in zip(a_refs[1:], w_refs[1:]):
        acc = acc + _bdot(a_ref[0], w_ref[...])
    for r0 in range(0, tm, RC):
        gate = _mod_row(mod_ref[0], 2, pl.program_id(1) * tm + r0)
        o_ref[0, r0:r0 + RC, :] = x_ref[0, r0:r0 + RC, :] + gate * acc[r0:r0 + RC]


def _outproj_call(xs, mods_l, acts, ws, name):
    n_in = len(acts)
    rows = acts[0].shape[1]
    tm = RES_TM if rows == S else RES_TM_LAT
    in_specs = [pl.BlockSpec((1, tm, D), lambda b, i: (b, i, 0)),
                pl.BlockSpec((1, MOD_ROWS, D), lambda b, i: (b, 0, 0))]
    in_specs += [pl.BlockSpec((1, tm, a.shape[2]), lambda b, i: (b, i, 0)) for a in acts]
    in_specs += [pl.BlockSpec(w.shape, lambda b, i: (0, 0)) for w in ws]
    return pl.pallas_call(
        functools.partial(_outproj_kernel, n_in=n_in, tm=tm),
        grid=(BATCH, rows // tm),
        in_specs=in_specs,
        out_specs=pl.BlockSpec((1, tm, D), lambda b, i: (b, i, 0)),
        out_shape=jax.ShapeDtypeStruct((BATCH, S, D), f32),
        input_output_aliases={0: 0},
        compiler_params=_cparams(("parallel", "parallel"), 40),
        name=name,
    )(xs, mods_l, *acts, *ws)


FFN_TM = 1152
FFN_TM_LAST = 1024
FFN_RC = 128
FFN_TH = 256
FFN_NH = FFN_H // FFN_TH


def _ffn_kernel(x_ref, mod_ref, g_ref, win_ref, wo_ref, *rest, tm, final):
    gf_ref = rest[0] if final else None
    o_ref, h_scr, acc = rest[-3:]
    row0 = pl.program_id(1) * tm
    for r0 in range(0, tm, FFN_RC):
        h_scr[r0:r0 + FFN_RC, :] = _norm_mod(x_ref[0, r0:r0 + FFN_RC, :], g_ref[...], mod_ref[0],
                                             3, 4, row0 + r0).astype(bf16)
    for k in range(FFN_NH):
        gate_cols = slice(k * FFN_TH, (k + 1) * FFN_TH)
        up_cols = slice(FFN_H + k * FFN_TH, FFN_H + (k + 1) * FFN_TH)
        act = (_silu(_bdot(h_scr[...], win_ref[:, gate_cols]))
               * _bdot(h_scr[...], win_ref[:, up_cols])).astype(bf16)
        down = _bdot(act, wo_ref[gate_cols, :])
        if k == 0:
            acc[...] = down
        else:
            acc[...] += down
    for r0 in range(0, tm, FFN_RC):
        x = x_ref[0, r0:r0 + FFN_RC, :] + _mod_row(mod_ref[0], 5, row0 + r0) * acc[r0:r0 + FFN_RC, :]
        if final:
            ms = jnp.mean(x * x, axis=-1, keepdims=True)
            x = x * lax.rsqrt(ms + EPS) * gf_ref[...]
        o_ref[0, r0:r0 + FFN_RC, :] = x


def _ffn_call(xs, mods_l, g, w_in, w_out, final_g=None):
    final = final_g is not None
    tm, rows = (FFN_TM_LAST, SEQ) if final else (FFN_TM, S)
    resident = lambda a: pl.BlockSpec(a.shape, lambda b, i: (0, 0), pipeline_mode=pl.Buffered(1))
    vec = pl.BlockSpec((1, D), lambda b, i: (0, 0))
    return pl.pallas_call(
        functools.partial(_ffn_kernel, tm=tm, final=final),
        grid=(BATCH, rows // tm),
        in_specs=[pl.BlockSpec((1, tm, D), lambda b, i: (b, i, 0)),
                  pl.BlockSpec((1, MOD_ROWS, D), lambda b, i: (b, 0, 0)),
                  vec, resident(w_in), resident(w_out)] + ([vec] if final else []),
        out_specs=pl.BlockSpec((1, tm, D), lambda b, i: (b, i, 0)),
        out_shape=jax.ShapeDtypeStruct((BATCH, rows, D), f32),
        scratch_shapes=[pltpu.VMEM((tm, D), bf16), pltpu.VMEM((tm, D), f32)],
        input_output_aliases={} if final else {0: 0},
        compiler_params=_cparams(("parallel", "parallel"), 52),
        name="swiglu_ffn_final" if final else "swiglu_ffn",
    )(xs, mods_l, g, w_in, w_out, *([final_g] if final else []))


def _dft_table(lb):
    n = 2 * lb
    f = jnp.arange(lb, dtype=jnp.int32)[:, None]
    t = jnp.arange(lb, dtype=jnp.int32)[None, :]
    ang = (2.0 * math.pi / n) * ((f * t) % n).astype(f32)
    nyq = jnp.where((t & 1) == 0, 1.0, -1.0).astype(f32)
    sin_rows = jnp.where(f == 0, nyq, jnp.sin(ang))
    return jnp.concatenate([jnp.cos(ang), sin_rows], axis=0)


def _hy_features(seg):
    t = jnp.abs(jnp.arange(2 * seg) - seg).astype(f32)
    t_unit = t / (seg - 1)
    ang = 2.0 * math.pi * t / seg
    fb = jnp.linspace(1e-4, HY_BANDS - 1, HY_BANDS, dtype=f32)
    z = jnp.concatenate([t_unit[:, None], jnp.cos(ang[:, None] * fb), -jnp.sin(ang[:, None] * fb)], axis=-1)
    z = jnp.pad(z, ((0, 0), (0, 128 - HY_EMB)))
    deltas = jnp.linspace(math.log(1e-2) / 1.5, math.log(1e-2) / 0.3, D, dtype=f32)
    win = jnp.exp(-t_unit[:, None] * jnp.abs(deltas)[None, :])
    return z, win


def _rope_tables():
    nf = HEAD_P // 4
    inv = ROPE_BASE ** (-jnp.arange(nf, dtype=f32) / nf)
    pos = jnp.arange(SEQ)
    row = (pos // GRID_W).astype(f32)
    col = (pos % GRID_W).astype(f32)
    ar = row[:, None] * inv[None, :]
    ac = col[:, None] * inv[None, :]
    ang = jnp.concatenate([ar, ar, ac, ac], axis=-1)
    sign = jnp.tile(jnp.concatenate([-jnp.ones(nf, f32), jnp.ones(nf, f32)]), 2)
    cos_t = jnp.tile(jnp.cos(ang), (1, ATT_GROUP))
    sin_t = jnp.tile(jnp.sin(ang) * sign[None, :], (1, ATT_GROUP))
    lane = jnp.arange(ATT_GROUP * HEAD_P)
    partner = jnp.where((lane & nf) == 0, lane + nf, lane - nf)
    perm = (lane[:, None] == partner[None, :]).astype(bf16)
    return cos_t, sin_t, perm


def _head_expand(lane_off):
    r = jnp.arange(DT_LANES)[:, None]
    c = jnp.arange(D)[None, :]
    return (c // HEAD_P + lane_off == r).astype(bf16)


def _pad_rows(a, rows):
    return jnp.pad(a, ((0, rows - a.shape[0]), (0, 0)))


def _pad_lanes(a, lanes):
    return jnp.pad(a, ((0, 0), (0, lanes - a.shape[1])))


def kernel(x, c, ctx, c_ctx, ada_w, ada_b, norm1_g, norm2_g, hy_in_w, ssd_conv_w, ssd_conv_b, ssd_dt_bias, ssd_a_log, ssd_d, ssd_norm_g, hy_conv_w, hy_conv_b, hy_w1, hy_b1, hy_w2, hy_b2, hy_w3, hy_b3, hy_w4, hy_freq, hy_bias, hy_out_w, attn_qkv_w, attn_sinks, attn_out_w, ffn_w_in, ffn_w_out, final_g):
    xs = jnp.concatenate([x, ctx], axis=1)

    cc = jnp.concatenate([c, c_ctx[None, :], jnp.zeros((MOD_ROWS - BATCH - 1, D), f32)], axis=0)
    modv = _mod_call(cc, ada_w, ada_b)
    lat = modv[:, :BATCH].reshape(DEPTH, BATCH, N_MOD, D)
    cmod = jnp.broadcast_to(modv[:, BATCH].reshape(DEPTH, 1, N_MOD, D), (DEPTH, BATCH, N_MOD, D))
    pad = jnp.zeros((DEPTH, BATCH, CTX_MOD - N_MOD, D), f32)
    mods = jnp.concatenate([lat, pad, cmod, pad], axis=2)

    tabs = {}
    for seg, nb in ((SEQ, HY_NB), (CTX, 1)):
        w = _dft_table(seg // nb)
        tabs[seg] = (w.astype(bf16), w.T.astype(bf16)) + _hy_features(seg)
    e_f, e_b = _head_expand(0), _head_expand(HEADS)
    cos_t, sin_t, perm = _rope_tables()

    for layer in range(DEPTH):
        j = layer // 2
        mods_l = mods[layer]
        g1 = norm1_g[layer][None, :]
        if layer % 2 == 0:
            dtb = _pad_lanes(ssd_dt_bias[j].reshape(1, 2 * HEADS), DT_LANES)
            cw = jnp.concatenate([jnp.zeros((4, D), f32),
                                  jnp.concatenate([ssd_conv_w[j], ssd_conv_b[j][None, :]], axis=0)], axis=1)
            cw_hy = _pad_rows(jnp.concatenate([hy_conv_w[j], hy_conv_b[j][None, :]], axis=0), 8)
            zx, dt, x0, u = _inproj_even_call(xs, mods_l, g1, hy_in_w, j, _pad_rows(cw, 8), dtb,
                                              hy_in_w[j][:, COL_HY:].astype(bf16), cw_hy)

            alog = _pad_lanes(ssd_a_log[j].reshape(1, 2 * HEADS), DT_LANES)
            dskip = jnp.repeat(ssd_d[j], HEAD_P)[None, :]
            yn = _ssd_call(zx, dt, alog, dskip, ssd_norm_g[j][None, :], e_f, e_b)

            segs = []
            for seg, nb in ((SEQ, HY_NB), (CTX, 1)):
                wtab, wtab_t, zt, win = tabs[seg]
                ka, kb = _hyfilter_call(zt, _pad_rows(hy_w1[j], 128), hy_b1[j][None, :], hy_w2[j],
                                        hy_b2[j][None, :], hy_w3[j], hy_b3[j][None, :],
                                        _pad_rows(hy_freq[j], 8), hy_w4[j], win, wtab, seg, nb)
                segs.append((wtab, wtab_t, ka, kb))
            gh = _hyconv_call(u, x0, hy_bias[j][None, :], *segs)
            w_out = hy_out_w[j].astype(bf16)
            xs = _outproj_call(xs, mods_l, [yn, gh], [w_out[:D], w_out[D:]], "outproj_even")
        else:
            wq = attn_qkv_w[j]
            wk = wq[:, QW:QW + KVW].reshape(D, KV_HEADS, HEAD_P)
            wv = wq[:, QW + KVW:].reshape(D, KV_HEADS, HEAD_P)
            w_kv = jnp.concatenate([wk, wv], axis=2).reshape(D, 2 * KVW)
            w_qkv = jnp.concatenate([wq[:, :QW], w_kv], axis=1).astype(bf16)
            qkv = _nmm_call(xs, mods_l, g1, w_qkv, 512, "qkv_proj")
            o = _attn_call(qkv, attn_sinks[j], cos_t, sin_t, perm, need_ctx=layer < DEPTH - 1)
            xs = _outproj_call(xs, mods_l, [o], [attn_out_w[j].astype(bf16)], "outproj_attn")
        xs = _ffn_call(xs, mods_l, norm2_g[layer][None, :], ffn_w_in[layer].astype(bf16),
                       ffn_w_out[layer].astype(bf16),
                       final_g=final_g[None, :] if layer == DEPTH - 1 else None)
    return xs
```

```python
import functools
import math

import jax
import jax.numpy as jnp
from jax import lax
from jax.experimental import pallas as pl
from jax.experimental.pallas import tpu as pltpu

f32 = jnp.float32
bf16 = jnp.bfloat16

D = 1024
BATCH = 8
SEQ = 2048
DEPTH = 4
GRID_W = 64
CTX = 256
S = SEQ + CTX
EPS = 1e-6
N_MOD = 6
MOD_ROWS = 16
CTX_MOD = 8

HEADS = 16
HEAD_P = 64
STATE = 128
GROUPS = 2
CHUNK = 128
XBC = D + 2 * GROUPS * STATE
ZX = D + XBC
DT_LANES = 128
HY_BANDS = 16
HY_EMB = 1 + 2 * HY_BANDS
HY_FFN = 64
COL_XBC = D
COL_DT = COL_XBC + XBC
COL_HY = COL_DT + 2 * HEADS

Q_HEADS = 16
KV_HEADS = 4
ATT_GROUP = Q_HEADS // KV_HEADS
QW = Q_HEADS * HEAD_P
KVW = KV_HEADS * HEAD_P
WINDOW = 128
ABLK = 128
ROPE_BASE = 10000.0
FFN_H = -(-8 * D // (3 * 256)) * 256
LOG2E = math.log2(math.e)

RC = 256
NMM_RC = 768
VMEM_CAP = 56 * 1024 * 1024


def _cparams(sem, vmem_mb):
    return pltpu.CompilerParams(dimension_semantics=sem,
                                vmem_limit_bytes=min(int(vmem_mb * 1024 * 1024), VMEM_CAP))


def _silu(v):
    return v * jax.nn.sigmoid(v)


def _softplus(v):
    return jnp.maximum(v, 0.0) + jnp.log1p(jnp.exp(-jnp.abs(v)))


def _bdot(a, b):
    return jnp.dot(a, b, preferred_element_type=f32)


def _bdot_nt(a, b):
    return lax.dot_general(a, b, (((1,), (1,)), ((), ())), preferred_element_type=f32)


def _row_chunks(n_rows, body, unroll=1):
    def step(c, carry):
        body(pl.multiple_of(c * RC, RC))
        return carry
    lax.fori_loop(0, n_rows // RC, step, 0, unroll=unroll)


def _norm_mod(x, g, mod, shift_row, scale_row, row0):
    ms = jnp.mean(x * x, axis=-1, keepdims=True)
    scale = _mod_row(mod, scale_row, row0)
    return x * lax.rsqrt(ms + EPS) * (g * (1.0 + scale)) + _mod_row(mod, shift_row, row0)


def _mod_row(mod, k, row0):
    return jnp.where(row0 >= SEQ, mod[CTX_MOD + k:CTX_MOD + k + 1], mod[k:k + 1])


def _norm_mod_to_scratch(x_ref, g_ref, mod_ref, h_scr):
    def body(r0):
        h_scr[pl.ds(r0, RC), :] = _norm_mod(x_ref[0, pl.ds(r0, RC), :], g_ref[...], mod_ref[0],
                                            0, 1, r0).astype(bf16)
    _row_chunks(S, body)


CONV_ROWS = 64


def _conv3_rows(p_scr, r0, cw):
    n, cols = CONV_ROWS, p_scr.shape[1]
    zero = jnp.zeros((8, cols), f32)
    tiles = [p_scr[r0 + k:r0 + k + 8, :] for k in range(0, n, 8)]
    above = zero if r0 in (0, SEQ) else p_scr[r0 - 8:r0, :]
    below = zero if r0 + n in (SEQ, S) else p_scr[r0 + n:r0 + n + 8, :]
    sub = lax.broadcasted_iota(jnp.int32, (8, cols), 0)
    down = [pltpu.roll(t, 1, 0) for t in [above] + tiles]
    up = [pltpu.roll(t, 7, 0) for t in tiles + [below]]
    out = []
    for k, t in enumerate(tiles):
        prev = jnp.where(sub == 0, down[k], down[k + 1])
        nxt = jnp.where(sub == 7, up[k + 1], up[k])
        out.append(prev * cw[0:1] + t * cw[1:2] + nxt * cw[2:3] + cw[3:4])
    return jnp.concatenate(out, axis=0)


def _project_then_conv(mm, piece):
    done = 0
    for r0 in range(0, S, NMM_RC):
        while done + CONV_ROWS < r0:
            piece(done)
            done += CONV_ROWS
        mm(r0)
    while done < S:
        piece(done)
        done += CONV_ROWS


def _mod_kernel(c_ref, w_ref, b_ref, o_ref):
    sc = _silu(c_ref[...]).astype(bf16)
    o_ref[0] = _bdot(sc, w_ref[0].astype(bf16)) + b_ref[0]


def _mod_call(cc, ada_w, ada_b):
    tn = 1024
    return pl.pallas_call(
        _mod_kernel,
        grid=(DEPTH, N_MOD * D // tn),
        in_specs=[pl.BlockSpec((MOD_ROWS, D), lambda l, j: (0, 0)),
                  pl.BlockSpec((1, D, tn), lambda l, j: (l, 0, j)),
                  pl.BlockSpec((1, 1, tn), lambda l, j: (l, 0, j))],
        out_specs=pl.BlockSpec((1, MOD_ROWS, tn), lambda l, j: (l, 0, j)),
        out_shape=jax.ShapeDtypeStruct((DEPTH, MOD_ROWS, N_MOD * D), f32),
        compiler_params=_cparams(("parallel", "parallel"), 24),
        name="adaln_mod",
    )(cc, ada_w, ada_b.reshape(DEPTH, 1, N_MOD * D))


def _nmm_kernel(x_ref, mod_ref, g_ref, w_ref, o_ref, h_scr):
    @pl.when(pl.program_id(1) == 0)
    def _():
        _norm_mod_to_scratch(x_ref, g_ref, mod_ref, h_scr)

    def body(c, carry):
        r0 = pl.multiple_of(c * NMM_RC, NMM_RC)
        o_ref[0, pl.ds(r0, NMM_RC), :] = _bdot(h_scr[pl.ds(r0, NMM_RC), :], w_ref[...]).astype(o_ref.dtype)
        return carry
    lax.fori_loop(0, S // NMM_RC, body, 0)


def _nmm_call(xs, mods_l, g, w, tn, name):
    n = w.shape[1]
    return pl.pallas_call(
        _nmm_kernel,
        grid=(BATCH, n // tn),
        in_specs=[pl.BlockSpec((1, S, D), lambda b, j: (b, 0, 0)),
                  pl.BlockSpec((1, MOD_ROWS, D), lambda b, j: (b, 0, 0)),
                  pl.BlockSpec((1, D), lambda b, j: (0, 0)),
                  pl.BlockSpec((D, tn), lambda b, j: (0, j))],
        out_specs=pl.BlockSpec((1, S, tn), lambda b, j: (b, 0, j)),
        out_shape=jax.ShapeDtypeStruct((BATCH, S, n), bf16),
        scratch_shapes=[pltpu.VMEM((S, D), bf16)],
        compiler_params=_cparams(("parallel", "arbitrary"), 40),
        name=name,
    )(xs, mods_l, g, w)


ZX_TN = 512
NZ_T = D // ZX_TN
ZX_TILES = ZX // ZX_TN
HY_TN = 256
HY_TILES = D // HY_TN
assert COL_DT == ZX and COL_DT % DT_LANES == 0


def _inproj_even_kernel(x_ref, mod_ref, g_ref, w_ref, cw_ref, wdt_ref, dtb_ref, w0_ref, w1_ref, w2_ref,
                        c0_ref, c1_ref, c2_ref, zx_ref, dt_ref, x0_ref, u_ref,
                        h_scr, wb_scr, p0_scr, p1_scr, p2_scr):
    j = pl.program_id(1)

    @pl.when(j == 0)
    def _():
        _norm_mod_to_scratch(x_ref, g_ref, mod_ref, h_scr)
        wdt = wdt_ref[0].astype(bf16)

        def dt_body(c, carry):
            r0 = pl.multiple_of(c * NMM_RC, NMM_RC)
            sp = _softplus(_bdot(h_scr[pl.ds(r0, NMM_RC), :], wdt) + dtb_ref[...])
            lane = lax.broadcasted_iota(jnp.int32, sp.shape, 1)
            dt_ref[0, pl.ds(r0, NMM_RC), :] = jnp.where(lane < 2 * HEADS, sp, 0.0)
            return carry
        lax.fori_loop(0, S // NMM_RC, dt_body, 0)

    @pl.when(j < ZX_TILES)
    def _():
        wb_scr[...] = w_ref[0].astype(bf16)

    @pl.when(j < NZ_T)
    def _():
        def body(c, carry):
            r0 = pl.multiple_of(c * NMM_RC, NMM_RC)
            zx_ref[0, pl.ds(r0, NMM_RC), :] = _bdot(h_scr[pl.ds(r0, NMM_RC), :], wb_scr[...]).astype(bf16)
            return carry
        lax.fori_loop(0, S // NMM_RC, body, 0)

    @pl.when(jnp.logical_and(j >= NZ_T, j < ZX_TILES))
    def _():
        def mm(r0):
            p0_scr[r0:r0 + NMM_RC, :] = _bdot(h_scr[r0:r0 + NMM_RC, :], wb_scr[...])

        def piece(a):
            zx_ref[0, a:a + CONV_ROWS, :] = _silu(_conv3_rows(p0_scr, a, cw_ref[...])).astype(bf16)
        _project_then_conv(mm, piece)

    @pl.when(j >= ZX_TILES)
    def _():
        px0 = p0_scr.at[:, 0:HY_TN]

        def mm_x0(r0):
            px0[r0:r0 + NMM_RC, :] = _bdot(h_scr[r0:r0 + NMM_RC, :], w0_ref[...])

        def piece_x0(a):
            x0_ref[0, a:a + CONV_ROWS, :] = _conv3_rows(px0, a, c0_ref[...]).astype(bf16)
        _project_then_conv(mm_x0, piece_x0)

        def mm_u(r0):
            p1_scr[r0:r0 + NMM_RC, :] = _bdot(h_scr[r0:r0 + NMM_RC, :], w1_ref[...])
            p2_scr[r0:r0 + NMM_RC, :] = _bdot(h_scr[r0:r0 + NMM_RC, :], w2_ref[...])

        def piece_u(a):
            x1 = _conv3_rows(p1_scr, a, c1_ref[...])
            v = _conv3_rows(p2_scr, a, c2_ref[...])
            u_ref[0, a:a + CONV_ROWS, :] = (x1 * v).astype(bf16)
        _project_then_conv(mm_u, piece_u)


def _inproj_even_call(xs, mods_l, g, w_all, jl, cw, dtb, w_hy, cw_hy):
    zt = lambda j: jnp.minimum(j, ZX_TILES - 1)
    ht = lambda j: jnp.maximum(j - ZX_TILES, 0)
    wspec = lambda k: pl.BlockSpec((D, HY_TN), lambda b, j, k=k: (0, ht(j) + k * HY_TILES))
    cspec = lambda k: pl.BlockSpec((8, HY_TN), lambda b, j, k=k: (0, ht(j) + k * HY_TILES))
    return pl.pallas_call(
        _inproj_even_kernel,
        grid=(BATCH, ZX_TILES + HY_TILES),
        in_specs=[pl.BlockSpec((1, S, D), lambda b, j: (b, 0, 0)),
                  pl.BlockSpec((1, MOD_ROWS, D), lambda b, j: (b, 0, 0)),
                  pl.BlockSpec((1, D), lambda b, j: (0, 0)),
                  pl.BlockSpec((1, D, ZX_TN), lambda b, j: (jl, 0, zt(j))),
                  pl.BlockSpec((8, ZX_TN), lambda b, j: (0, zt(j))),
                  pl.BlockSpec((1, D, DT_LANES), lambda b, j: (jl, 0, COL_DT // DT_LANES)),
                  pl.BlockSpec((1, DT_LANES), lambda b, j: (0, 0)),
                  wspec(0), wspec(1), wspec(2), cspec(0), cspec(1), cspec(2)],
        out_specs=[pl.BlockSpec((1, S, ZX_TN), lambda b, j: (b, 0, zt(j))),
                   pl.BlockSpec((1, S, DT_LANES), lambda b, j: (b, 0, 0)),
                   pl.BlockSpec((1, S, HY_TN), lambda b, j: (b, 0, ht(j))),
                   pl.BlockSpec((1, S, HY_TN), lambda b, j: (b, 0, ht(j)))],
        out_shape=[jax.ShapeDtypeStruct((BATCH, S, ZX), bf16),
                   jax.ShapeDtypeStruct((BATCH, S, DT_LANES), f32),
                   jax.ShapeDtypeStruct((BATCH, S, D), bf16),
                   jax.ShapeDtypeStruct((BATCH, S, D), bf16)],
        scratch_shapes=[pltpu.VMEM((S, D), bf16), pltpu.VMEM((D, ZX_TN), bf16),
                        pltpu.VMEM((S, ZX_TN), f32), pltpu.VMEM((S, HY_TN), f32),
                        pltpu.VMEM((S, HY_TN), f32)],
        compiler_params=_cparams(("parallel", "arbitrary"), 56),
        name="inproj_even",
    )(xs, mods_l, g, w_all, cw, w_all, dtb, w_hy, w_hy, w_hy, cw_hy, cw_hy, cw_hy)


N_CHUNK = S // CHUNK
LAT_CHUNKS = SEQ // CHUNK


def _split3_dot(t, a):
    a1 = a.astype(bf16)
    r1 = a - a1.astype(f32)
    a2 = r1.astype(bf16)
    a3 = (r1 - a2.astype(f32)).astype(bf16)
    return _bdot(t, a1) + _bdot(t, a2) + _bdot(t, a3)


def _ssd_chunk(zx_ref, dt_ref, aneg, e_ref, y_scr, st_scr, r0, backward):
    lane_off = HEADS if backward else 0
    x = zx_ref[0, pl.ds(r0, CHUNK), D:2 * D].astype(f32)
    bm = zx_ref[0, pl.ds(r0, CHUNK), 2 * D:2 * D + GROUPS * STATE]
    cm = zx_ref[0, pl.ds(r0, CHUNK), 2 * D + GROUPS * STATE:ZX]
    dt = dt_ref[0, pl.ds(r0, CHUNK), :]
    a = dt * aneg

    ri = lax.broadcasted_iota(jnp.int32, (CHUNK, CHUNK), 0)
    ci = lax.broadcasted_iota(jnp.int32, (CHUNK, CHUNK), 1)
    causal = (ci >= ri) if backward else (ci <= ri)
    tmat = jnp.where(causal, 1.0, 0.0).astype(bf16)
    acum = _split3_dot(tmat, a)
    acum_t = acum.T
    tot = acum[0:1] if backward else acum[CHUNK - 1:CHUNK]
    dte = jnp.exp2(tot - acum)
    eac = jnp.exp2(acum)
    cdec = jnp.broadcast_to(jnp.exp2(tot), (16, DT_LANES))

    e = e_ref[...]
    dt_x = _bdot(dt.astype(bf16), e)
    dte_x = _bdot(dte.astype(bf16), e)
    eac_x = _bdot(eac.astype(bf16), e)
    cdec_x = _bdot(cdec.astype(bf16), e)[0:1]

    xdt = x * dt_x
    xs_b = (xdt * dte_x).astype(bf16)
    lane = lax.broadcasted_iota(jnp.int32, (CHUNK, 2 * HEAD_P), 1)
    low = lane < HEAD_P
    hg = HEADS // GROUPS
    gw = hg * HEAD_P
    for g in range(GROUPS):
        bg = bm[:, g * STATE:(g + 1) * STATE]
        cg = cm[:, g * STATE:(g + 1) * STATE]
        cb = _bdot_nt(cg, bg)
        hin = st_scr[:, g * gw:(g + 1) * gw]
        yoff = _bdot(cg, hin.astype(bf16)) * eac_x[:, g * gw:(g + 1) * gw]
        for hp in range(hg // 2):
            h0 = g * hg + 2 * hp
            ms = []
            for h in (h0, h0 + 1):
                l0 = lane_off + h
                seg = acum[:, l0:l0 + 1] - acum_t[l0:l0 + 1, :]
                ms.append(cb * jnp.exp2(jnp.where(causal, seg, -jnp.inf)))
            lhs = jnp.concatenate(ms, axis=1).astype(bf16)
            xp = xdt[:, h0 * HEAD_P:(h0 + 2) * HEAD_P]
            rhs = jnp.concatenate([jnp.where(low, xp, 0.0), jnp.where(low, 0.0, xp)],
                                  axis=0).astype(bf16)
            yd = _bdot(lhs, rhs) + yoff[:, hp * 2 * HEAD_P:(hp + 1) * 2 * HEAD_P]
            cols = slice(h0 * HEAD_P, (h0 + 2) * HEAD_P)
            y_scr[pl.ds(r0, CHUNK), cols] = y_scr[pl.ds(r0, CHUNK), cols] + yd
        bg_t = bg.astype(f32).T.astype(bf16)
        st_scr[:, g * gw:(g + 1) * gw] = (hin * cdec_x[:, g * gw:(g + 1) * gw]
                                         + _bdot(bg_t, xs_b[:, g * gw:(g + 1) * gw]))


def _ssd_kernel(zx_ref, dt_ref, alog_ref, dskip_ref, ng_ref, ef_ref, eb_ref, o_ref,
                y_scr, sf_scr, sb_scr):
    aneg = -jnp.exp(alog_ref[...]) * LOG2E
    sf_scr[...] = jnp.zeros_like(sf_scr)
    sb_scr[...] = jnp.zeros_like(sb_scr)

    def init(c, carry):
        r0 = pl.multiple_of(c * CHUNK, CHUNK)
        y_scr[pl.ds(r0, CHUNK), :] = dskip_ref[...] * zx_ref[0, pl.ds(r0, CHUNK), D:2 * D].astype(f32)
        return carry
    lax.fori_loop(0, N_CHUNK, init, 0)

    def step(t, carry):
        cf = jnp.where(t < N_CHUNK - LAT_CHUNKS, LAT_CHUNKS + t, t - (N_CHUNK - LAT_CHUNKS))
        cbk = N_CHUNK - 1 - t
        _ssd_chunk(zx_ref, dt_ref, aneg, ef_ref, y_scr, sf_scr, pl.multiple_of(cf * CHUNK, CHUNK), False)
        _ssd_chunk(zx_ref, dt_ref, aneg, eb_ref, y_scr, sb_scr, pl.multiple_of(cbk * CHUNK, CHUNK), True)
        return carry
    lax.fori_loop(0, N_CHUNK, step, 0, unroll=3)

    def fin(c, carry):
        r0 = pl.multiple_of(c * CHUNK, CHUNK)
        t = y_scr[pl.ds(r0, CHUNK), :] * _silu(zx_ref[0, pl.ds(r0, CHUNK), 0:D].astype(f32))
        ms = jnp.mean(t * t, axis=-1, keepdims=True)
        o_ref[0, pl.ds(r0, CHUNK), :] = (t * lax.rsqrt(ms + EPS) * ng_ref[...]).astype(bf16)
        return carry
    lax.fori_loop(0, N_CHUNK, fin, 0)


def _ssd_call(zx, dt, alog, dskip, ng, e_f, e_b):
    return pl.pallas_call(
        _ssd_kernel,
        grid=(BATCH,),
        in_specs=[pl.BlockSpec((1, S, ZX), lambda b: (b, 0, 0)),
                  pl.BlockSpec((1, S, DT_LANES), lambda b: (b, 0, 0)),
                  pl.BlockSpec((1, DT_LANES), lambda b: (0, 0)),
                  pl.BlockSpec((1, D), lambda b: (0, 0)),
                  pl.BlockSpec((1, D), lambda b: (0, 0)),
                  pl.BlockSpec((DT_LANES, D), lambda b: (0, 0)),
                  pl.BlockSpec((DT_LANES, D), lambda b: (0, 0))],
        out_specs=pl.BlockSpec((1, S, D), lambda b: (b, 0, 0)),
        out_shape=jax.ShapeDtypeStruct((BATCH, S, D), bf16),
        scratch_shapes=[pltpu.VMEM((S, D), f32),
                        pltpu.VMEM((STATE, D), f32),
                        pltpu.VMEM((STATE, D), f32)],
        compiler_params=_cparams(("parallel",), 56),
        name="ssd_scan",
    )(zx, dt, alog, dskip, ng, e_f, e_b)


def _hp_dot(a, b):
    return jnp.dot(a, b, precision=lax.Precision.HIGHEST, preferred_element_type=f32)


def _hyfilter_kernel(z_ref, w1_ref, b1_ref, w2_ref, b2_ref, w3_ref, b3_ref, fr_ref, w4f_ref, w4b_ref,
                     win_ref, w_ref, ka_ref, kb_ref, h_scr, tap_scr, tb_scr, pp_scr, pq_scr, *, seg, nb):
    lb = seg // nb
    tc = tap_scr.shape[1]

    @pl.when(pl.program_id(0) == 0)
    def _():
        fr = fr_ref[...]

        def hidden(c, carry):
            r0 = pl.multiple_of(c * RC, RC)
            h = jnp.sin(fr[0:1] * (_hp_dot(z_ref[pl.ds(r0, RC), :], w1_ref[...]) + b1_ref[...]))
            h = jnp.sin(fr[1:2] * (_hp_dot(h, w2_ref[...]) + b2_ref[...]))
            h_scr[pl.ds(r0, RC), :] = jnp.sin(fr[2:3] * (_hp_dot(h, w3_ref[...]) + b3_ref[...]))
            return carry
        lax.fori_loop(0, 2 * seg // RC, hidden, 0)

    def taps(w4_ref):
        def body(c, ssq):
            r0 = pl.multiple_of(c * RC, RC)
            t = _hp_dot(h_scr[pl.ds(r0, RC), :], w4_ref[...]) * win_ref[pl.ds(r0, RC), :]
            row = r0 + lax.broadcasted_iota(jnp.int32, t.shape, 0)
            t = jnp.where(row == 0, 0.0, t)
            tap_scr[pl.ds(r0, RC), :] = t
            return ssq + jnp.sum(t * t, axis=0, keepdims=True)
        return body
    ssq = lax.fori_loop(0, seg // RC, taps(w4b_ref), jnp.zeros((1, tc), f32))
    nrm = lax.rsqrt(lax.fori_loop(seg // RC, 2 * seg // RC, taps(w4f_ref), ssq) + 1e-6)

    def scale(c, carry):
        r0 = pl.multiple_of(c * RC, RC)
        tb_scr[pl.ds(r0, RC), :] = (tap_scr[pl.ds(r0, RC), :] * nrm).astype(bf16)
        return carry
    lax.fori_loop(0, 2 * seg // RC, scale, 0)

    row = lax.broadcasted_iota(jnp.int32, (lb, tc), 0)
    sgn = jnp.where((row & 1) == 0, 1.0, -1.0)
    for s in range(2 * nb):
        blk = tb_scr[s * lb:(s + 1) * lb, :]
        p = _bdot(w_ref[0:lb, :], blk)
        q = _bdot(w_ref[lb:2 * lb, :], blk)
        if s > 0:
            t0 = tb_scr[(s - 1) * lb:(s - 1) * lb + 16, :][0:1].astype(f32)
            ka_ref[s - 1] = (p + sgn * (pp_scr[...] - t0)).astype(ka_ref.dtype)
            kb_ref[s - 1] = jnp.where(row == 0, q + pq_scr[...] - t0,
                                      q + sgn * pq_scr[...]).astype(kb_ref.dtype)
        pp_scr[...] = p
        pq_scr[...] = q


HY_TC = 256
HY_NB = 4
HY_SPEC_DT = bf16


def _hyfilter_call(zt, w1, b1, w2, b2, w3, b3, fr, w4, win, wtab, seg, nb):
    tc, lb = HY_TC, seg // nb
    nct = D // tc
    const = lambda shape: pl.BlockSpec(shape, lambda c: (0, 0))
    return pl.pallas_call(
        functools.partial(_hyfilter_kernel, seg=seg, nb=nb),
        grid=(nct,),
        in_specs=[const((2 * seg, 128)), const((128, HY_FFN)), const((1, HY_FFN)),
                  const((HY_FFN, HY_FFN)), const((1, HY_FFN)),
                  const((HY_FFN, HY_FFN)), const((1, HY_FFN)), const((8, HY_FFN)),
                  pl.BlockSpec((HY_FFN, tc), lambda c: (0, c)),
                  pl.BlockSpec((HY_FFN, tc), lambda c: (0, c + nct)),
                  pl.BlockSpec((2 * seg, tc), lambda c: (0, c)),
                  const((2 * lb, lb))],
        out_specs=[pl.BlockSpec((2 * nb - 1, lb, tc), lambda c: (0, 0, c)),
                   pl.BlockSpec((2 * nb - 1, lb, tc), lambda c: (0, 0, c))],
        out_shape=[jax.ShapeDtypeStruct((2 * nb - 1, lb, D), HY_SPEC_DT)] * 2,
        scratch_shapes=[pltpu.VMEM((2 * seg, HY_FFN), f32),
                        pltpu.VMEM((2 * seg, tc), f32), pltpu.VMEM((2 * seg, tc), bf16),
                        pltpu.VMEM((lb, tc), f32), pltpu.VMEM((lb, tc), f32)],
        compiler_params=_cparams(("arbitrary",), 40),
        name=f"hyena_filter_{seg}",
    )(zt, w1, b1, w2, b2, w3, b3, fr, w4, w4, win, wtab)


HY_PIECE = 64


def _hyconv_segment(u_ref, x0_ref, bias_ref, o_ref, w_ref, wt_ref, ka_ref, kb_ref,
                    ua_scr, ub_scr, ya_scr, yb_scr, row0, nb, lb):
    for i in range(nb):
        ui = u_ref[0, row0 + i * lb:row0 + (i + 1) * lb, :]
        ua_scr[i, 0:lb, :] = _bdot(w_ref[0:lb, :], ui).astype(HY_SPEC_DT)
        ub_scr[i, 0:lb, :] = _bdot(w_ref[lb:2 * lb, :], ui).astype(HY_SPEC_DT)

    is0 = lax.broadcasted_iota(jnp.int32, (HY_PIECE, ka_ref.shape[2]), 0) == 0
    for o in range(nb):
        for r in range(0, lb, HY_PIECE):
            rows = slice(r, r + HY_PIECE)
            ya = yb = None
            for i in range(nb):
                a, b = ua_scr[i, rows, :], ub_scr[i, rows, :]
                ka, kb = ka_ref[o - i + nb - 1, rows, :], kb_ref[o - i + nb - 1, rows, :]
                if r == 0:
                    pa = jnp.where(is0, a * ka, a * ka - b * kb)
                    pb = jnp.where(is0, b * kb, a * kb + b * ka)
                else:
                    pa, pb = a * ka - b * kb, a * kb + b * ka
                ya = pa if ya is None else ya + pa
                yb = pb if yb is None else yb + pb
            scale = jnp.where(is0, 0.5 / lb, 1.0 / lb) if r == 0 else 1.0 / lb
            ya_scr[o, rows, :] = (ya * scale).astype(bf16)
            yb_scr[o, rows, :] = (yb * scale).astype(bf16)
        y = _bdot(wt_ref[:, 0:lb], ya_scr[o, 0:lb, :]) + _bdot(wt_ref[:, lb:2 * lb], yb_scr[o, 0:lb, :])
        rows = slice(row0 + o * lb, row0 + (o + 1) * lb)
        y = y + bias_ref[...] * u_ref[0, rows, :].astype(f32)
        o_ref[0, rows, :] = (x0_ref[0, rows, :].astype(f32) * y).astype(bf16)


def _hyconv_kernel(u_ref, x0_ref, bias_ref, w_ref, wt_ref, ka_ref, kb_ref, cw_ref, cwt_ref, cka_ref,
                   ckb_ref, o_ref, ua_scr, ub_scr, ya_scr, yb_scr):
    scr = (ua_scr, ub_scr, ya_scr, yb_scr)
    _hyconv_segment(u_ref, x0_ref, bias_ref, o_ref, w_ref, wt_ref, ka_ref, kb_ref, *scr,
                    row0=0, nb=HY_NB, lb=SEQ // HY_NB)
    _hyconv_segment(u_ref, x0_ref, bias_ref, o_ref, cw_ref, cwt_ref, cka_ref, ckb_ref, *scr,
                    row0=SEQ, nb=1, lb=CTX)


def _hyconv_call(u, x0, bias, lat, ctx):
    tc, lb = HY_TC, SEQ // HY_NB
    full = lambda a: pl.BlockSpec(a.shape, lambda c, b: (0,) * a.ndim)
    spec = lambda a: pl.BlockSpec(a.shape[:2] + (tc,), lambda c, b: (0, 0, c))
    seq = pl.BlockSpec((1, S, tc), lambda c, b: (b, 0, c))
    return pl.pallas_call(
        _hyconv_kernel,
        grid=(D // tc, BATCH),
        in_specs=[seq, seq, pl.BlockSpec((1, tc), lambda c, b: (0, c)),
                  full(lat[0]), full(lat[1]), spec(lat[2]), spec(lat[3]),
                  full(ctx[0]), full(ctx[1]), spec(ctx[2]), spec(ctx[3])],
        out_specs=seq,
        out_shape=jax.ShapeDtypeStruct((BATCH, S, D), bf16),
        scratch_shapes=[pltpu.VMEM((HY_NB, lb, tc), HY_SPEC_DT), pltpu.VMEM((HY_NB, lb, tc), HY_SPEC_DT),
                        pltpu.VMEM((HY_NB, lb, tc), bf16), pltpu.VMEM((HY_NB, lb, tc), bf16)],
        compiler_params=_cparams(("parallel", "parallel"), 40),
        name="hyena_conv",
    )(u, x0, bias, *lat, *ctx)


def _rope(xb, perm, cos_t, sin_t):
    n = xb.shape[1]
    return xb.astype(f32) * cos_t + _bdot(xb, perm[0:n, 0:n]) * sin_t


assert WINDOW == ABLK
A_SPAN = ABLK + 2 * WINDOW
A_KEYS = A_SPAN + CTX
A_COLS = ATT_GROUP * ABLK


def _attn_kernel(sink_ref, q_ref, kv_ref, cos_ref, sin_ref, perm_ref, o_ref, q_scr, k_scr, vt_scr,
                 s_scr, p_scr):
    g = pl.program_id(1)
    qscale = HEAD_P ** -0.5 * LOG2E

    def prep(r0, with_rope):
        qb = q_ref[0, pl.ds(r0, RC), :]
        kvb = kv_ref[0, pl.ds(r0, RC), :]
        kv = kvb.astype(f32)
        if with_rope:
            cos_t, sin_t = cos_ref[pl.ds(r0, RC), :], sin_ref[pl.ds(r0, RC), :]
            q = _rope(qb, perm_ref[...], cos_t, sin_t) * qscale
            k = _rope(kvb, perm_ref[...], cos_t[:, 0:2 * HEAD_P], sin_t[:, 0:2 * HEAD_P])
        else:
            q, k = qb.astype(f32) * qscale, kv
        low = lax.broadcasted_iota(jnp.int32, kv.shape, 1) < HEAD_P
        k_scr[pl.ds(r0, RC), :] = jnp.where(low, k, 0.0).astype(bf16)
        for r in range(ATT_GROUP):
            slab = q[:, (r // 2) * 2 * HEAD_P:(r // 2 + 1) * 2 * HEAD_P]
            if r % 2:
                slab = pltpu.roll(slab, HEAD_P, 1)
            q_scr[r, pl.ds(r0, RC), :] = jnp.where(low, slab, 0.0).astype(bf16)
        v_t = kv.T[HEAD_P:2 * HEAD_P, :].astype(bf16)
        c0 = r0 // ABLK
        for h in range(RC // ABLK):
            vt_scr[c0 + h] = v_t[:, h * ABLK:(h + 1) * ABLK]
    _row_chunks(SEQ, lambda r0: prep(r0, True), unroll=4)
    for r0 in range(SEQ, S, RC):
        prep(r0, False)

    kmq = (lax.broadcasted_iota(jnp.int32, (ABLK, A_COLS), 0)
           - (lax.broadcasted_iota(jnp.int32, (ABLK, A_COLS), 1) & (ABLK - 1)))

    def block(q0, start, masks, slot):
        qstack = jnp.concatenate([q_scr[r, pl.ds(q0, ABLK), :] for r in range(ATT_GROUP)], axis=0)
        chunks = []
        if masks:
            s_loc = _bdot_nt(k_scr[pl.ds(start, len(masks) * ABLK), :], qstack)
            for c, kind in enumerate(masks):
                s_c = s_loc[c * ABLK:(c + 1) * ABLK]
                if kind is not None:
                    s_c = jnp.where(kmq >= 0 if kind == 'ge' else kmq <= 0, s_c, -jnp.inf)
                s_scr[slot, c * ABLK:(c + 1) * ABLK, :] = s_c
                chunks.append((c * ABLK, start // ABLK + c))
        n_loc = len(chunks) * ABLK
        s_scr[slot, n_loc:n_loc + CTX, :] = _bdot_nt(k_scr[SEQ:S, :], qstack)
        chunks += [(n_loc + c * ABLK, SEQ // ABLK + c) for c in range(CTX // ABLK)]

        inv = []
        for r in range(ATT_GROUP):
            cols = slice(r * ABLK, (r + 1) * ABLK)
            sink = sink_ref[g * ATT_GROUP + r] * LOG2E
            m = jnp.full((1, ABLK), sink, f32)
            for row, _ in chunks:
                m = jnp.maximum(m, jnp.max(s_scr[slot, row:row + ABLK, cols], axis=0, keepdims=True))
            den = jnp.exp2(sink - m)
            for row, _ in chunks:
                p = jnp.exp2(s_scr[slot, row:row + ABLK, cols] - m)
                den = den + jnp.sum(p, axis=0, keepdims=True)
                p_scr[slot, row:row + ABLK, cols] = p.astype(bf16)
            inv.append(1.0 / den)

        o_t = None
        for row, vc in chunks:
            part = _bdot(vt_scr[vc], p_scr[slot, row:row + ABLK, :])
            o_t = part if o_t is None else o_t + part
        o_t = jnp.concatenate([o_t[:, r * ABLK:(r + 1) * ABLK] * inv[r] for r in range(ATT_GROUP)], axis=0)
        o_ref[0, pl.ds(q0, ABLK), :] = o_t.T.astype(bf16)

    n_blk = SEQ // ABLK
    block(0, 0, (None, 'le'), 0)
    block(SEQ - ABLK, SEQ - 2 * ABLK, ('ge', None), 1)

    def lat_pair(j, carry):
        for slot in range(2):
            q0 = pl.multiple_of((1 + 2 * j + slot) * ABLK, ABLK)
            block(q0, pl.multiple_of(q0 - WINDOW, ABLK), ('ge', None, 'le'), slot)
        return carry
    lax.fori_loop(0, (n_blk - 2) // 2, lat_pair, 0, unroll=True)

    for slot, q0 in enumerate(range(SEQ, o_ref.shape[1], ABLK)):
        block(q0, 0, (), slot % 2)


def _attn_call(qkv, sinks, cos_t, sin_t, perm, need_ctx):
    gw = ATT_GROUP * HEAD_P
    rows = S if need_ctx else SEQ
    return pl.pallas_call(
        _attn_kernel,
        grid_spec=pltpu.PrefetchScalarGridSpec(
            num_scalar_prefetch=1,
            grid=(BATCH, KV_HEADS),
            in_specs=[pl.BlockSpec((1, S, gw), lambda b, g, s: (b, 0, g)),
                      pl.BlockSpec((1, S, 2 * HEAD_P), lambda b, g, s: (b, 0, QW // (2 * HEAD_P) + g)),
                      pl.BlockSpec((SEQ, gw), lambda b, g, s: (0, 0)),
                      pl.BlockSpec((SEQ, gw), lambda b, g, s: (0, 0)),
                      pl.BlockSpec((gw, gw), lambda b, g, s: (0, 0))],
            out_specs=pl.BlockSpec((1, rows, gw), lambda b, g, s: (b, 0, g)),
            scratch_shapes=[pltpu.VMEM((ATT_GROUP, S, 2 * HEAD_P), bf16),
                            pltpu.VMEM((S, 2 * HEAD_P), bf16),
                            pltpu.VMEM((S // ABLK, HEAD_P, ABLK), bf16),
                            pltpu.VMEM((2, A_KEYS, A_COLS), f32),
                            pltpu.VMEM((2, A_KEYS, A_COLS), bf16)]),
        out_shape=jax.ShapeDtypeStruct((BATCH, rows, QW), bf16),
        compiler_params=_cparams(("parallel", "parallel"), 40),
        name="window_attn",
    )(sinks, qkv, qkv, cos_t, sin_t, perm)


RES_TM = 768
RES_TM_LAT = 1024


def _outproj_kernel(x_ref, mod_ref, *rest, n_in, tm):
    a_refs, w_refs, o_ref = rest[:n_in], rest[n_in:2 * n_in], rest[2 * n_in]
    acc = _bdot(a_refs[0][0], w_refs[0][...])
    for a_ref, w_ref in zip(a_refs[1:], w_refs[1:]):
        acc = acc + _bdot(a_ref[0], w_ref[...])
    for r0 in range(0, tm, RC):
        gate = _mod_row(mod_ref[0], 2, pl.program_id(1) * tm + r0)
        o_ref[0, r0:r0 + RC, :] = x_ref[0, r0:r0 + RC, :] + gate * acc[r0:r0 + RC]


def _outproj_call(xs, mods_l, acts, ws, name):
    n_in = len(acts)
    rows = acts[0].shape[1]
    tm = RES_TM if rows == S else RES_TM_LAT
    in_specs = [pl.BlockSpec((1, tm, D), lambda b, i: (b, i, 0)),
                pl.BlockSpec((1, MOD_ROWS, D), lambda b, i: (b, 0, 0))]
    in_specs += [pl.BlockSpec((1, tm, a.shape[2]), lambda b, i: (b, i, 0)) for a in acts]
    in_specs += [pl.BlockSpec(w.shape, lambda b, i: (0, 0)) for w in ws]
    return pl.pallas_call(
        functools.partial(_outproj_kernel, n_in=n_in, tm=tm),
        grid=(BATCH, rows // tm),
        in_specs=in_specs,
        out_specs=pl.BlockSpec((1, tm, D), lambda b, i: (b, i, 0)),
        out_shape=jax.ShapeDtypeStruct((BATCH, S, D), f32),
        input_output_aliases={0: 0},
        compiler_params=_cparams(("parallel", "parallel"), 40),
        name=name,
    )(xs, mods_l, *acts, *ws)


FFN_TM = 1152
FFN_TM_LAST = 1024
FFN_RC = 128
FFN_TH = 256
FFN_NH = FFN_H // FFN_TH


def _ffn_kernel(x_ref, mod_ref, g_ref, win_ref, wo_ref, *rest, tm, final):
    gf_ref = rest[0] if final else None
    o_ref, h_scr, acc = rest[-3:]
    row0 = pl.program_id(1) * tm
    for r0 in range(0, tm, FFN_RC):
        h_scr[r0:r0 + FFN_RC, :] = _norm_mod(x_ref[0, r0:r0 + FFN_RC, :], g_ref[...], mod_ref[0],
                                             3, 4, row0 + r0).astype(bf16)
    for k in range(FFN_NH):
        gate_cols = slice(k * FFN_TH, (k + 1) * FFN_TH)
        up_cols = slice(FFN_H + k * FFN_TH, FFN_H + (k + 1) * FFN_TH)
        act = (_silu(_bdot(h_scr[...], win_ref[:, gate_cols]))
               * _bdot(h_scr[...], win_ref[:, up_cols])).astype(bf16)
        down = _bdot(act, wo_ref[gate_cols, :])
        if k == 0:
            acc[...] = down
        else:
            acc[...] += down
    for r0 in range(0, tm, FFN_RC):
        x = x_ref[0, r0:r0 + FFN_RC, :] + _mod_row(mod_ref[0], 5, row0 + r0) * acc[r0:r0 + FFN_RC, :]
        if final:
            ms = jnp.mean(x * x, axis=-1, keepdims=True)
            x = x * lax.rsqrt(ms + EPS) * gf_ref[...]
        o_ref[0, r0:r0 + FFN_RC, :] = x


def _ffn_call(xs, mods_l, g, w_in, w_out, final_g=None):
    final = final_g is not None
    tm, rows = (FFN_TM_LAST, SEQ) if final else (FFN_TM, S)
    resident = lambda a: pl.BlockSpec(a.shape, lambda b, i: (0, 0), pipeline_mode=pl.Buffered(1))
    vec = pl.BlockSpec((1, D), lambda b, i: (0, 0))
    return pl.pallas_call(
        functools.partial(_ffn_kernel, tm=tm, final=final),
        grid=(BATCH, rows // tm),
        in_specs=[pl.BlockSpec((1, tm, D), lambda b, i: (b, i, 0)),
                  pl.BlockSpec((1, MOD_ROWS, D), lambda b, i: (b, 0, 0)),
                  vec, resident(w_in), resident(w_out)] + ([vec] if final else []),
        out_specs=pl.BlockSpec((1, tm, D), lambda b, i: (b, i, 0)),
        out_shape=jax.ShapeDtypeStruct((BATCH, rows, D), f32),
        scratch_shapes=[pltpu.VMEM((tm, D), bf16), pltpu.VMEM((tm, D), f32)],
        input_output_aliases={} if final else {0: 0},
        compiler_params=_cparams(("parallel", "parallel"), 52),
        name="swiglu_ffn_final" if final else "swiglu_ffn",
    )(xs, mods_l, g, w_in, w_out, *([final_g] if final else []))


def _dft_table(lb):
    n = 2 * lb
    f = jnp.arange(lb, dtype=jnp.int32)[:, None]
    t = jnp.arange(lb, dtype=jnp.int32)[None, :]
    ang = (2.0 * math.pi / n) * ((f * t) % n).astype(f32)
    nyq = jnp.where((t & 1) == 0, 1.0, -1.0).astype(f32)
    sin_rows = jnp.where(f == 0, nyq, jnp.sin(ang))
    return jnp.concatenate([jnp.cos(ang), sin_rows], axis=0)


def _hy_features(seg):
    t = jnp.abs(jnp.arange(2 * seg) - seg).astype(f32)
    t_unit = t / (seg - 1)
    ang = 2.0 * math.pi * t / seg
    fb = jnp.linspace(1e-4, HY_BANDS - 1, HY_BANDS, dtype=f32)
    z = jnp.concatenate([t_unit[:, None], jnp.cos(ang[:, None] * fb), -jnp.sin(ang[:, None] * fb)], axis=-1)
    z = jnp.pad(z, ((0, 0), (0, 128 - HY_EMB)))
    deltas = jnp.linspace(math.log(1e-2) / 1.5, math.log(1e-2) / 0.3, D, dtype=f32)
    win = jnp.exp(-t_unit[:, None] * jnp.abs(deltas)[None, :])
    return z, win


def _rope_tables():
    nf = HEAD_P // 4
    inv = ROPE_BASE ** (-jnp.arange(nf, dtype=f32) / nf)
    pos = jnp.arange(SEQ)
    row = (pos // GRID_W).astype(f32)
    col = (pos % GRID_W).astype(f32)
    ar = row[:, None] * inv[None, :]
    ac = col[:, None] * inv[None, :]
    ang = jnp.concatenate([ar, ar, ac, ac], axis=-1)
    sign = jnp.tile(jnp.concatenate([-jnp.ones(nf, f32), jnp.ones(nf, f32)]), 2)
    cos_t = jnp.tile(jnp.cos(ang), (1, ATT_GROUP))
    sin_t = jnp.tile(jnp.sin(ang) * sign[None, :], (1, ATT_GROUP))
    lane = jnp.arange(ATT_GROUP * HEAD_P)
    partner = jnp.where((lane & nf) == 0, lane + nf, lane - nf)
    perm = (lane[:, None] == partner[None, :]).astype(bf16)
    return cos_t, sin_t, perm


def _head_expand(lane_off):
    r = jnp.arange(DT_LANES)[:, None]
    c = jnp.arange(D)[None, :]
    return (c // HEAD_P + lane_off == r).astype(bf16)


def _pad_rows(a, rows):
    return jnp.pad(a, ((0, rows - a.shape[0]), (0, 0)))


def _pad_lanes(a, lanes):
    return jnp.pad(a, ((0, 0), (0, lanes - a.shape[1])))


def kernel(x, c, ctx, c_ctx, ada_w, ada_b, norm1_g, norm2_g, hy_in_w, ssd_conv_w, ssd_conv_b, ssd_dt_bias, ssd_a_log, ssd_d, ssd_norm_g, hy_conv_w, hy_conv_b, hy_w1, hy_b1, hy_w2, hy_b2, hy_w3, hy_b3, hy_w4, hy_freq, hy_bias, hy_out_w, attn_qkv_w, attn_sinks, attn_out_w, ffn_w_in, ffn_w_out, final_g):
    xs = jnp.concatenate([x, ctx], axis=1)

    cc = jnp.concatenate([c, c_ctx[None, :], jnp.zeros((MOD_ROWS - BATCH - 1, D), f32)], axis=0)
    modv = _mod_call(cc, ada_w, ada_b)
    lat = modv[:, :BATCH].reshape(DEPTH, BATCH, N_MOD, D)
    cmod = jnp.broadcast_to(modv[:, BATCH].reshape(DEPTH, 1, N_MOD, D), (DEPTH, BATCH, N_MOD, D))
    pad = jnp.zeros((DEPTH, BATCH, CTX_MOD - N_MOD, D), f32)
    mods = jnp.concatenate([lat, pad, cmod, pad], axis=2)

    tabs = {}
    for seg, nb in ((SEQ, HY_NB), (CTX, 1)):
        w = _dft_table(seg // nb)
        tabs[seg] = (w.astype(bf16), w.T.astype(bf16)) + _hy_features(seg)
    e_f, e_b = _head_expand(0), _head_expand(HEADS)
    cos_t, sin_t, perm = _rope_tables()

    for layer in range(DEPTH):
        j = layer // 2
        mods_l = mods[layer]
        g1 = norm1_g[layer][None, :]
        if layer % 2 == 0:
            dtb = _pad_lanes(ssd_dt_bias[j].reshape(1, 2 * HEADS), DT_LANES)
            cw = jnp.concatenate([jnp.zeros((4, D), f32),
                                  jnp.concatenate([ssd_conv_w[j], ssd_conv_b[j][None, :]], axis=0)], axis=1)
            cw_hy = _pad_rows(jnp.concatenate([hy_conv_w[j], hy_conv_b[j][None, :]], axis=0), 8)
            zx, dt, x0, u = _inproj_even_call(xs, mods_l, g1, hy_in_w, j, _pad_rows(cw, 8), dtb,
                                              hy_in_w[j][:, COL_HY:].astype(bf16), cw_hy)

            alog = _pad_lanes(ssd_a_log[j].reshape(1, 2 * HEADS), DT_LANES)
            dskip = jnp.repeat(ssd_d[j], HEAD_P)[None, :]
            yn = _ssd_call(zx, dt, alog, dskip, ssd_norm_g[j][None, :], e_f, e_b)

            segs = []
            for seg, nb in ((SEQ, HY_NB), (CTX, 1)):
                wtab, wtab_t, zt, win = tabs[seg]
                ka, kb = _hyfilter_call(zt, _pad_rows(hy_w1[j], 128), hy_b1[j][None, :], hy_w2[j],
                                        hy_b2[j][None, :], hy_w3[j], hy_b3[j][None, :],
                                        _pad_rows(hy_freq[j], 8), hy_w4[j], win, wtab, seg, nb)
                segs.append((wtab, wtab_t, ka, kb))
            gh = _hyconv_call(u, x0, hy_bias[j][None, :], *segs)
            w_out = hy_out_w[j].astype(bf16)
            xs = _outproj_call(xs, mods_l, [yn, gh], [w_out[:D], w_out[D:]], "outproj_even")
        else:
            wq = attn_qkv_w[j]
            wk = wq[:, QW:QW + KVW].reshape(D, KV_HEADS, HEAD_P)
            wv = wq[:, QW + KVW:].reshape(D, KV_HEADS, HEAD_P)
            w_kv = jnp.concatenate([wk, wv], axis=2).reshape(D, 2 * KVW)
            w_qkv = jnp.concatenate([wq[:, :QW], w_kv], axis=1).astype(bf16)
            qkv = _nmm_call(xs, mods_l, g1, w_qkv, 512, "qkv_proj")
            o = _attn_call(qkv, attn_sinks[j], cos_t, sin_t, perm, need_ctx=layer < DEPTH - 1)
            xs = _outproj_call(xs, mods_l, [o], [attn_out_w[j].astype(bf16)], "outproj_attn")
        xs = _ffn_call(xs, mods_l, norm2_g[layer][None, :], ffn_w_in[layer].astype(bf16),
                       ffn_w_out[layer].astype(bf16),
                       final_g=final_g[None, :] if layer == DEPTH - 1 else None)
    return xs
```
